```python
import jax, jax.numpy as jnp
from jax import lax
import numpy as np

D_MODEL = 1024
BATCH = 4
SEQ = 8192
DEPTH = 1

F32 = jnp.float32
N_META = 16
EPS = 1e-5
ROPE_THETA = 500000.0
ATT_HEADS = 8
ATT_KV_HEADS = 2
ATT_HEAD_DIM = 64
ATT_WIDTH = ATT_HEADS * ATT_HEAD_DIM
IDX_HEADS = 8
IDX_DIM = 32
IDX_SCALE = (IDX_DIM ** -0.5) * (IDX_HEADS ** -0.5)
TOPK_MAX = 256
Q_BLOCK = 128
HG_HEADS = 4
HG_DK = 128
HG_DV = 128
HG_KWIDTH = HG_HEADS * HG_DK
HG_WIDTH = HG_HEADS * HG_DV
HG_CHUNK = 64
N_EXPERTS = 32
TOP_K = 4
D_FF = 1024
SWIGLU_LIMIT = 7.0
SWIGLU_ALPHA = 1.702
MOE_BLOCK = 128
IN_SPLITS = (ATT_WIDTH, ATT_KV_HEADS * ATT_HEAD_DIM, ATT_KV_HEADS * ATT_HEAD_DIM,
             IDX_HEADS * IDX_DIM, IDX_DIM, IDX_HEADS,
             HG_KWIDTH, HG_KWIDTH, HG_WIDTH, HG_WIDTH, D_MODEL, D_MODEL)
IN_COLS = sum(IN_SPLITS)

kernel_name = "hybrid_dsa_hgrn2_moe_meta"


def rmsnorm(x, g):
    xf = x.astype(F32)
    y = xf * lax.rsqrt(jnp.mean(xf * xf, axis=-1, keepdims=True) + EPS)
    return (y * g.astype(F32)).astype(x.dtype)


def partial_rope(x, pos):
    d = x.shape[-1]
    rot = d // 4
    half = rot // 2
    inv = ROPE_THETA ** (-jnp.arange(half, dtype=F32) * 2.0 / rot)
    ang = pos.astype(F32)[:, None] * inv[None, :]
    c = jnp.cos(ang)[:, None, :]
    s = jnp.sin(ang)[:, None, :]
    xf = x.astype(F32)
    x1 = xf[..., :half]
    x2 = xf[..., half:rot]
    out = jnp.concatenate([x1 * c - x2 * s, x2 * c + x1 * s, xf[..., rot:]], axis=-1)
    return out.astype(x.dtype)


def dsa_attention(q, k, v, iq, ik, iw, n_valid):
    B, Lp = q.shape[0], q.shape[1]
    nqb = Lp // Q_BLOCK
    k_sel = min(TOPK_MAX, n_valid // 4)
    groups = ATT_HEADS // ATT_KV_HEADS
    scale = ATT_HEAD_DIM ** -0.5
    key_pos = jnp.arange(Lp)
    ikf = ik.astype(F32)
    bidx = jnp.arange(B)[:, None, None]

    def to_blocks(a):
        return jnp.moveaxis(a.reshape((B, nqb, Q_BLOCK) + a.shape[2:]), 1, 0)

    def block(args):
        qb, iqb, iwb, qpos = args
        rel = jax.nn.relu(jnp.einsum('bqhd,bkd->bqhk', iqb.astype(F32), ikf))
        score = jnp.einsum('bqhk,bqh->bqk', rel, iwb.astype(F32)) * IDX_SCALE
        causal = key_pos[None, :] <= qpos[:, None]
        score = jnp.where(causal[None], score, -jnp.inf)
        _, sel = lax.top_k(score, k_sel)
        valid = sel <= qpos[None, :, None]
        kg = k[bidx, sel].astype(F32)
        vg = v[bidx, sel].astype(F32)
        qg = qb.reshape(B, Q_BLOCK, ATT_KV_HEADS, groups, ATT_HEAD_DIM).astype(F32)
        logits = jnp.einsum('bqgrd,bqkgd->bqgrk', qg, kg) * scale
        logits = jnp.where(valid[:, :, None, None, :], logits, -jnp.inf)
        p = jax.nn.softmax(logits, axis=-1)
        o = jnp.einsum('bqgrk,bqkgd->bqgrd', p, vg)
        return o.reshape(B, Q_BLOCK, ATT_WIDTH).astype(q.dtype)

    out = lax.map(block, (to_blocks(q), to_blocks(iq), to_blocks(iw), key_pos.reshape(nqb, Q_BLOCK)))
    return jnp.moveaxis(out, 0, 1).reshape(B, Lp, ATT_WIDTH)


def hgrn2_chunk(state, inp):
    q, kk, v, logf = inp
    C = q.shape[2]
    b = jnp.cumsum(logf, axis=2)
    mask = jnp.tril(jnp.ones((C, C), dtype=bool))
    diff = b[:, :, :, None, :] - b[:, :, None, :, :]
    decay = jnp.exp(jnp.where(mask[None, None, :, :, None], diff, -jnp.inf))
    scores = jnp.einsum('bhtd,bhtsd,bhsd->bhts', q, decay, kk)
    o = jnp.einsum('bhts,bhse->bhte', scores, v) + jnp.einsum('bhtd,bhde->bhte', q * jnp.exp(b), state)
    b_last = b[:, :, -1:, :]
    new_state = (jnp.exp(b_last[:, :, 0, :])[..., None] * state
                 + jnp.einsum('bhsd,bhse->bhde', kk * jnp.exp(b_last - b), v))
    return new_state, o


def hgrn2(q, f_pre, i, lb):
    B, L = q.shape[0], q.shape[1]

    def heads(a, d):
        return a.reshape(B, L, HG_HEADS, d).transpose(0, 2, 1, 3).astype(F32)

    lbh = lb.reshape(HG_HEADS, 1, HG_DK).astype(F32)
    f = lbh + (1.0 - lbh) * jax.nn.sigmoid(heads(f_pre, HG_DK))
    logf = jnp.log(f)
    kk = 1.0 - f
    qh = heads(q, HG_DK) * (HG_DK ** -0.5)
    vh = heads(i, HG_DV)
    state0 = jnp.zeros((B, HG_HEADS, HG_DK, HG_DV), F32)
    state, o_meta = hgrn2_chunk(state0, (qh[:, :, :N_META], kk[:, :, :N_META],
                                         vh[:, :, :N_META], logf[:, :, :N_META]))
    n_chunks = (L - N_META) // HG_CHUNK

    def chunks(a):
        return jnp.moveaxis(a[:, :, N_META:].reshape(B, HG_HEADS, n_chunks, HG_CHUNK, a.shape[-1]), 2, 0)

    _, o_real = lax.scan(hgrn2_chunk, state, (chunks(qh), chunks(kk), chunks(vh), chunks(logf)))
    o_real = jnp.moveaxis(o_real, 0, 2).reshape(B, HG_HEADS, L - N_META, HG_DV)
    o = jnp.concatenate([o_meta, o_real], axis=2)
    return o.transpose(0, 2, 1, 3)


def moe(x, router_w, router_b, w_gate, b_gate, w_up, b_up, w_down, b_down):
    B, L, D = x.shape
    T = B * L
    xt = x.reshape(T, D)
    logits = (xt @ router_w + router_b).astype(F32)
    top_vals, top_idx = lax.top_k(logits, TOP_K)
    gates = jax.nn.softmax(top_vals, axis=-1)
    n_assign = T * TOP_K
    expert = top_idx.reshape(-1)
    token = jnp.repeat(jnp.arange(T, dtype=jnp.int32), TOP_K)
    weight = gates.reshape(-1)
    order = jnp.argsort(expert)
    e_sorted = expert[order]
    t_sorted = token[order]
    w_sorted = weight[order]
    counts = jnp.bincount(expert, length=N_EXPERTS)
    starts = jnp.cumsum(counts) - counts
    padded = (counts + MOE_BLOCK - 1) // MOE_BLOCK * MOE_BLOCK
    pad_ends = jnp.cumsum(padded)
    pad_starts = pad_ends - padded
    dest = pad_starts[e_sorted] + (jnp.arange(n_assign) - starts[e_sorted])
    n_blocks = -(-n_assign // MOE_BLOCK) + N_EXPERTS
    cap = n_blocks * MOE_BLOCK
    slot_token = jnp.full((cap,), T, jnp.int32).at[dest].set(t_sorted)
    slot_weight = jnp.zeros((cap,), F32).at[dest].set(w_sorted)
    block_expert = jnp.minimum(
        jnp.searchsorted(pad_ends, jnp.arange(n_blocks) * MOE_BLOCK, side='right'), N_EXPERTS - 1)
    x_pad = jnp.concatenate([xt, jnp.zeros((1, D), xt.dtype)], axis=0)

    def expert_block(args):
        tok, e = args
        xb = x_pad[tok]
        g = xb @ w_gate[e] + b_gate[e]
        u = xb @ w_up[e] + b_up[e]
        g = jnp.minimum(g, SWIGLU_LIMIT)
        u = jnp.clip(u, -SWIGLU_LIMIT, SWIGLU_LIMIT)
        act = (u + 1.0) * (g * jax.nn.sigmoid(SWIGLU_ALPHA * g))
        return act @ w_down[e] + b_down[e]

    out = lax.map(expert_block, (slot_token.reshape(n_blocks, MOE_BLOCK), block_expert))
    out = out.reshape(cap, D) * slot_weight[:, None].astype(out.dtype)
    y = jnp.zeros((T + 1, D), out.dtype).at[slot_token].add(out)[:T]
    return y.reshape(B, L, D).astype(x.dtype)


def hybrid_layer(h, attn_norm_g, w_in, hgrn_norm_g, w_up_attn, w_up_hgrn, w_out, lb,
                 ffn_norm_g, router_w, router_b, w_gate, b_gate, w_up, b_up, w_down, b_down):
    B, L, _ = h.shape
    Lp = -(-L // Q_BLOCK) * Q_BLOCK
    pos = jnp.arange(Lp)
    xn = rmsnorm(h, attn_norm_g)
    proj = xn @ w_in
    (q, k, v, iq, ik, iw, hq, hf, hi, hg, ga, gh) = jnp.split(
        proj, [int(c) for c in np.cumsum(IN_SPLITS)[:-1]], axis=-1)

    def pad(a):
        return jnp.pad(a, ((0, 0), (0, Lp - L)) + ((0, 0),) * (a.ndim - 2))

    qa = partial_rope(pad(q).reshape(B, Lp, ATT_HEADS, ATT_HEAD_DIM), pos)
    ka = partial_rope(pad(k).reshape(B, Lp, ATT_KV_HEADS, ATT_HEAD_DIM), pos)
    va = pad(v).reshape(B, Lp, ATT_KV_HEADS, ATT_HEAD_DIM)
    iqa = partial_rope(pad(iq).reshape(B, Lp, IDX_HEADS, IDX_DIM), pos)
    ika = partial_rope(pad(ik)[:, :, None, :], pos)[:, :, 0, :]
    att = dsa_attention(qa, ka, va, iqa, ika, pad(iw), L)[:, :L]

    o = hgrn2(hq, hf, hi, lb)
    o = o * lax.rsqrt(jnp.mean(o * o, axis=-1, keepdims=True) + EPS)
    o = o * hgrn_norm_g.reshape(HG_HEADS, HG_DV).astype(F32)
    o = o.reshape(B, L, HG_WIDTH) * jax.nn.silu(hg.astype(F32))
    hgr = o.astype(h.dtype)

    mix = jax.nn.sigmoid(ga) * (att @ w_up_attn) + jax.nn.sigmoid(gh) * (hgr @ w_up_hgrn)
    h = h + mix @ w_out

    h = h + moe(rmsnorm(h, ffn_norm_g), router_w, router_b, w_gate, b_gate, w_up, b_up, w_down, b_down)
    return h


def setup_inputs(seed: int = 0) -> dict:
    key = jax.random.key(seed)
    ks = jax.random.split(key, 20)
    D = D_MODEL

    def nrm(k, shape, scale):
        return jax.random.normal(k, shape, F32) * scale

    return {
        "x": nrm(ks[0], (BATCH, SEQ, D), 1.0),
        "meta_tokens": nrm(ks[1], (N_META, D), 1.0),
        "attn_norm_g": 1.0 + nrm(ks[2], (DEPTH, D), 0.01),
        "w_in": nrm(ks[3], (DEPTH, D, IN_COLS), D ** -0.5),
        "hgrn_norm_g": 1.0 + nrm(ks[4], (DEPTH, HG_WIDTH), 0.01),
        "w_up_attn": nrm(ks[5], (DEPTH, ATT_WIDTH, D), ATT_WIDTH ** -0.5),
        "w_up_hgrn": nrm(ks[6], (DEPTH, HG_WIDTH, D), HG_WIDTH ** -0.5),
        "w_out": nrm(ks[7], (DEPTH, D, D), D ** -0.5),
        "hgrn_lb_logits": nrm(ks[8], (DEPTH + 1, HG_KWIDTH), 0.5),
        "ffn_norm_g": 1.0 + nrm(ks[9], (DEPTH, D), 0.01),
        "router_w": nrm(ks[10], (DEPTH, D, N_EXPERTS), D ** -0.5),
        "router_b": nrm(ks[11], (DEPTH, N_EXPERTS), 0.01),
        "w_gate": nrm(ks[12], (DEPTH, N_EXPERTS, D, D_FF), D ** -0.5),
        "b_gate": nrm(ks[13], (DEPTH, N_EXPERTS, D_FF), 0.01),
        "w_up": nrm(ks[14], (DEPTH, N_EXPERTS, D, D_FF), D ** -0.5),
        "b_up": nrm(ks[15], (DEPTH, N_EXPERTS, D_FF), 0.01),
        "w_down": nrm(ks[16], (DEPTH, N_EXPERTS, D_FF, D), D_FF ** -0.5),
        "b_down": nrm(ks[17], (DEPTH, N_EXPERTS, D), 0.01),
        "final_norm_g": 1.0 + nrm(ks[18], (D,), 0.01),
    }


def reference(x, meta_tokens, attn_norm_g, w_in, hgrn_norm_g, w_up_attn, w_up_hgrn, w_out,
              hgrn_lb_logits, ffn_norm_g, router_w, router_b, w_gate, b_gate, w_up, b_up,
              w_down, b_down, final_norm_g):
    B = x.shape[0]
    meta = jnp.broadcast_to(meta_tokens[None].astype(x.dtype), (B, N_META, D_MODEL))
    h = jnp.concatenate([meta, x], axis=1)
    lb_all = jnp.cumsum(jax.nn.softmax(hgrn_lb_logits.astype(F32), axis=0), axis=0)
    for l in range(DEPTH):
        h = hybrid_layer(h, attn_norm_g[l], w_in[l], hgrn_norm_g[l], w_up_attn[l], w_up_hgrn[l],
                         w_out[l], lb_all[l], ffn_norm_g[l], router_w[l], router_b[l],
                         w_gate[l], b_gate[l], w_up[l], b_up[l], w_down[l], b_down[l])
    return rmsnorm(h[:, N_META:], final_norm_g)
```

```python
import functools

import numpy as np
import jax
import jax.numpy as jnp
from jax import lax
from jax.experimental import pallas as pl
from jax.experimental.pallas import tpu as pltpu

F32 = jnp.float32
BF16 = jnp.bfloat16
I32 = jnp.int32

D_MODEL = 1024
N_META = 16
EPS = 1e-5
ROPE_THETA = 500000.0
ATT_HEADS = 8
ATT_KV_HEADS = 2
ATT_HEAD_DIM = 64
ATT_WIDTH = ATT_HEADS * ATT_HEAD_DIM
IDX_HEADS = 8
IDX_DIM = 32
TOPK_MAX = 256
Q_BLOCK = 128
HG_HEADS = 4
HG_DK = 128
HG_DV = 128
HG_WIDTH = HG_HEADS * HG_DV
N_EXPERTS = 32
TOP_K = 4
D_FF = 1024
SWIGLU_LIMIT = 7.0
SWIGLU_ALPHA = 1.702

LANES = 128
VMEM_LIMIT = 56 * 1024 * 1024
INT_MIN = -2 ** 31
NEG_INF = float("-inf")

HG_CHUNK = 128
FFN_BLOCK = 256


def _pick_tile(n, candidates):
    for c in candidates:
        if n % c == 0:
            return c
    raise ValueError(f"no tile for {n}")


def _dot(a, b):
    return jnp.dot(a, b, preferred_element_type=F32)


def _dot_nt(a, b):
    return lax.dot_general(a, b, (((1,), (1,)), ((), ())), preferred_element_type=F32)


def _dot_tn(a, b):
    return lax.dot_general(a, b, (((0,), (0,)), ((), ())), preferred_element_type=F32)


def _lane_iota(shape):
    return lax.broadcasted_iota(I32, shape, len(shape) - 1)


def _rope_tables(n_pos, head_dim, lane_valid):
    rot = head_dim // 4
    half = rot // 2
    inv = ROPE_THETA ** (-jnp.arange(half, dtype=F32) * 2.0 / rot)
    ang = jnp.arange(n_pos).astype(F32)[:, None] * inv[None, :]
    c = jnp.cos(ang)
    s = jnp.sin(ang)
    lane = np.arange(LANES)
    within = lane % head_dim
    valid = lane < lane_valid
    first = (within < half) & valid
    second = (within >= half) & (within < rot) & valid
    idx = np.where(within < half, within, within - half) % half
    cc = c[:, idx]
    ss = s[:, idx]
    tc = jnp.where((first | second)[None, :], cc, 1.0)
    t1 = jnp.where(first[None, :], -ss, 0.0)
    t2 = jnp.where(second[None, :], ss, 0.0)
    return jnp.stack([tc, t1, t2]).astype(F32), half


def _rope_apply(x, tab_ref, half):
    c = tab_ref[0]
    s1 = tab_ref[1]
    s2 = tab_ref[2]
    cols = []
    for j in range(x.shape[1] // LANES):
        xc = x[:, j * LANES:(j + 1) * LANES]
        cols.append(xc * c + pltpu.roll(xc, LANES - half, 1) * s1 + pltpu.roll(xc, half, 1) * s2)
    return cols[0] if len(cols) == 1 else jnp.concatenate(cols, axis=1)


_W_Q = (0, 512)
_W_KX = (512, 768)
_W_VX = (768, 1024)
_W_IQ = (1024, 1280)
_W_IK = (1280, 1408)
_W_HQ = (1408, 1920)
_W_HF = (1920, 2432)
_W_HI = (2432, 2944)
_W_HG = (2944, 3456)
_W_GA = (3456, 4480)
_W_GH = (4480, 5504)
_W_COLS = 5504


def _arrange_w_in(w_in):
    o = np.cumsum([0, 512, 128, 128, 256, 32, 8, 512, 512, 512, 512, 1024, 1024])
    seg = [w_in[:, o[i]:o[i + 1]] for i in range(12)]
    q, k, v, iq, ik, iw, hq, hf, hi, hg, ga, gh = seg
    z64 = jnp.zeros((D_MODEL, 64), w_in.dtype)
    z88 = jnp.zeros((D_MODEL, 88), w_in.dtype)
    kx = jnp.concatenate([k[:, :64], z64, k[:, 64:], z64], axis=1)
    vx = jnp.concatenate([v[:, :64], z64, v[:, 64:], z64], axis=1)
    ikx = jnp.concatenate([ik, iw, z88], axis=1)
    w = jnp.concatenate([q, kx, vx, iq, ikx, hq, hf, hi, hg, ga, gh], axis=1)
    assert w.shape[1] == _W_COLS
    return w.astype(BF16)


def _inproj_kernel(h_ref, g_ref, w_ref, tatt_ref, tkx_ref, tiq_ref, tik_ref,
                   q_ref, kx_ref, vx_ref, iq_ref, ikf_ref, ikb_ref,
                   hq_ref, hf_ref, hi_ref, hg_ref, ga_ref, gh_ref, *, half_att, half_idx):
    x = h_ref[0]
    xn = x * lax.rsqrt(jnp.mean(x * x, axis=-1, keepdims=True) + EPS) * g_ref[...]
    xb = xn.astype(BF16)

    def proj(rng):
        return _dot(xb, w_ref[:, rng[0]:rng[1]])

    q_ref[0] = _rope_apply(proj(_W_Q), tatt_ref, half_att)
    kx = _rope_apply(proj(_W_KX), tkx_ref, half_att).astype(BF16)
    kx_ref[0, 0] = kx[:, :LANES]
    kx_ref[0, 1] = kx[:, LANES:]
    vx = proj(_W_VX)
    ones_hi = jnp.where(_lane_iota(vx.shape) % LANES >= ATT_HEAD_DIM, 1.0, 0.0)
    vx = (vx + ones_hi).astype(BF16)
    vx_ref[0, 0] = vx[:, :LANES]
    vx_ref[0, 1] = vx[:, LANES:]
    iq_ref[0] = _rope_apply(proj(_W_IQ), tiq_ref, half_idx)
    ik = _rope_apply(proj(_W_IK), tik_ref, half_idx)
    ikf_ref[0] = ik
    ikb_ref[0] = ik.astype(BF16)
    hq_ref[0] = proj(_W_HQ)
    hf_ref[0] = proj(_W_HF)
    hi_ref[0] = proj(_W_HI)
    hg_ref[0] = proj(_W_HG)
    ga_ref[0] = proj(_W_GA)
    gh_ref[0] = proj(_W_GH)


def _inproj(h, g, w_arr):
    B, Lp, D = h.shape
    tm = _pick_tile(Lp, (320, 256, 128))
    tatt, half_att = _rope_tables(Lp, ATT_HEAD_DIM, LANES)
    tkx, _ = _rope_tables(Lp, ATT_HEAD_DIM, ATT_HEAD_DIM)
    tiq, half_idx = _rope_tables(Lp, IDX_DIM, LANES)
    tik, _ = _rope_tables(Lp, IDX_DIM, IDX_DIM)

    def rows(width, dtype=F32):
        return (jax.ShapeDtypeStruct((B, Lp, width), dtype),
                pl.BlockSpec((1, tm, width), lambda b, i: (b, i, 0)))

    def kv(dtype=BF16):
        return (jax.ShapeDtypeStruct((B, ATT_KV_HEADS, Lp, LANES), dtype),
                pl.BlockSpec((1, ATT_KV_HEADS, tm, LANES), lambda b, i: (b, 0, i, 0)))

    outs = [rows(512), kv(), kv(), rows(256), rows(128), rows(128, BF16),
            rows(512), rows(512), rows(512), rows(512), rows(1024), rows(1024)]
    tab_spec = pl.BlockSpec((3, tm, LANES), lambda b, i: (0, i, 0))
    return pl.pallas_call(
        functools.partial(_inproj_kernel, half_att=half_att, half_idx=half_idx),
        grid=(B, Lp // tm),
        in_specs=[pl.BlockSpec((1, tm, D), lambda b, i: (b, i, 0)),
                  pl.BlockSpec((1, D), lambda b, i: (0, 0)),
                  pl.BlockSpec((D, _W_COLS), lambda b, i: (0, 0)),
                  tab_spec, tab_spec, tab_spec, tab_spec],
        out_specs=[o[1] for o in outs],
        out_shape=[o[0] for o in outs],
        compiler_params=pltpu.CompilerParams(
            dimension_semantics=("arbitrary", "arbitrary"), vmem_limit_bytes=VMEM_LIMIT),
        name="inproj",
    )(h, g.reshape(1, D), w_arr, tatt, tkx, tiq, tik)


def _attn_kernel(q_ref, iq_ref, ikf_ref, ikb_ref, kx_ref, vx_ref, u_ref, o_ref,
                 keys_ref, acc_ref, m_ref, *, C, k_sel):
    j = pl.program_id(1)
    nck = ((j + 1) * Q_BLOCK + C - 1) // C
    lane = _lane_iota((Q_BLOCK, LANES))
    rowpos = j * Q_BLOCK + lax.broadcasted_iota(I32, (Q_BLOCK, 1), 0)

    iqf = iq_ref[0]
    wq = ikf_ref[0]
    per_blk = LANES // IDX_DIM
    iqh = []
    for h in range(IDX_HEADS):
        blk = iqf[:, (h // per_blk) * LANES:(h // per_blk + 1) * LANES]
        off = (h % per_blk) * IDX_DIM
        if off:
            blk = pltpu.roll(blk, LANES - off, 1)
        iqh.append(jnp.where(lane < IDX_DIM, blk, 0.0).astype(BF16))
    wcols = [wq[:, IDX_DIM + h:IDX_DIM + h + 1] for h in range(IDX_HEADS)]

    def causal_mask(c):
        kpos = c * C + lax.broadcasted_iota(I32, (1, C), 1)
        return kpos <= rowpos

    def phase_a(c, carry):
        ks = ikb_ref[0, pl.ds(pl.multiple_of(c * C, C), C), :]
        s = jnp.zeros((Q_BLOCK, C), F32)
        for h in range(IDX_HEADS):
            s = s + jnp.maximum(_dot_nt(iqh[h], ks), 0.0) * wcols[h]
        bits = pltpu.bitcast(s, I32)
        key = jnp.where(bits < 0, bits ^ jnp.int32(0x7FFFFFFF), bits)
        keys_ref[c] = jnp.where(causal_mask(c), key, jnp.int32(INT_MIN))
        return carry

    lax.fori_loop(0, nck, phase_a, 0)

    def count(cand, strict):
        cand_b = jnp.broadcast_to(cand, (Q_BLOCK, LANES))

        def body(c, acc):
            kc = keys_ref[c]
            for t in range(C // LANES):
                blk = kc[:, t * LANES:(t + 1) * LANES]
                hit = (blk > cand_b) if strict else (blk >= cand_b)
                acc = acc + jnp.where(hit, 1, 0)
            return acc

        acc = lax.fori_loop(0, nck, body, jnp.zeros((Q_BLOCK, LANES), I32))
        return jnp.sum(acc, axis=1, keepdims=True)

    def search(i, carry):
        prefix, nge = carry
        bit = lax.shift_left(jnp.int32(1), jnp.int32(31) - i)
        cand = prefix ^ bit
        cnt = count(cand, False)
        ok = cnt >= k_sel
        return jnp.where(ok, cand, prefix), jnp.where(ok, cnt, nge)

    thr, nge = lax.fori_loop(
        0, 32, search,
        (jnp.full((Q_BLOCK, 1), INT_MIN, I32), jnp.full((Q_BLOCK, 1), k_sel, I32)))
    ngt = count(thr, True)
    need = (k_sel - ngt).astype(F32)
    excess = jnp.where((nge > k_sel) & (thr != INT_MIN), 1, 0)
    any_excess = jnp.max(excess) > 0
    thr_b = jnp.broadcast_to(thr, (Q_BLOCK, C))
    thr_lo = jnp.maximum(thr_b, jnp.int32(INT_MIN + 1))

    groups = ATT_HEADS // ATT_KV_HEADS
    qf = q_ref[0]
    qh = []
    for h in range(ATT_HEADS):
        blk = qf[:, (h // 2) * LANES:(h // 2 + 1) * LANES]
        if h % 2:
            blk = pltpu.roll(blk, ATT_HEAD_DIM, 1)
        qh.append((jnp.where(lane < ATT_HEAD_DIM, blk, 0.0) * (ATT_HEAD_DIM ** -0.5)).astype(BF16))

    acc_ref[...] = jnp.zeros(acc_ref.shape, F32)
    m_ref[...] = jnp.full(m_ref.shape, NEG_INF, F32)

    def phase_c(c, eq_seen):
        kc = keys_ref[c]

        def bias_simple():
            return jnp.where(kc >= thr_lo, 0.0, NEG_INF), eq_seen

        def bias_ties():
            eq = (kc == thr_b) & causal_mask(c)
            eqf = jnp.where(eq, 1.0, 0.0)
            rank = eq_seen + _dot(eqf.astype(BF16), u_ref[...])
            take = (kc > thr_b) | (eq & (rank < need))
            return jnp.where(take, 0.0, NEG_INF), eq_seen + jnp.sum(eqf, axis=1, keepdims=True)

        bias, eq_next = lax.cond(any_excess, bias_ties, bias_simple)
        start = pl.multiple_of(c * C, C)
        for h in range(ATT_HEADS):
            g = h // groups
            kch = kx_ref[0, g, pl.ds(start, C), :]
            vch = vx_ref[0, g, pl.ds(start, C), :]
            lg = _dot_nt(qh[h], kch) + bias
            m_old = m_ref[h]
            m_new = jnp.maximum(m_old, jnp.max(lg, axis=1, keepdims=True))
            m_safe = jnp.where(m_new == NEG_INF, 0.0, m_new)
            p = jnp.exp(lg - m_safe[:, :1])
            alpha = jnp.exp(m_old - m_safe)
            acc_ref[h] = alpha * acc_ref[h] + _dot(p.astype(BF16), vch)
            m_ref[h] = m_new
        return eq_next

    lax.fori_loop(0, nck, phase_c, jnp.zeros((Q_BLOCK, 1), F32))

    for pair in range(ATT_HEADS // 2):
        a = acc_ref[2 * pair]
        b = acc_ref[2 * pair + 1]
        oa = a / pltpu.roll(a, ATT_HEAD_DIM, 1)
        ob = b / pltpu.roll(b, ATT_HEAD_DIM, 1)
        o_ref[0, :, pair * LANES:(pair + 1) * LANES] = jnp.where(
            lane < ATT_HEAD_DIM, oa, pltpu.roll(ob, ATT_HEAD_DIM, 1))


def _attn(q, iq, ikf, ikb, kx, vx, k_sel):
    B, Lp, _ = q.shape
    C = _pick_tile(Lp, (640, 512, 256, 128))
    nch = Lp // C
    u = jnp.asarray(np.triu(np.ones((C, C), np.float32), 1), BF16)
    qspec = lambda w: pl.BlockSpec((1, Q_BLOCK, w), lambda b, j: (b, j, 0))
    return pl.pallas_call(
        functools.partial(_attn_kernel, C=C, k_sel=k_sel),
        grid=(B, Lp // Q_BLOCK),
        in_specs=[qspec(512), qspec(256), qspec(128),
                  pl.BlockSpec((1, Lp, LANES), lambda b, j: (b, 0, 0)),
                  pl.BlockSpec((1, ATT_KV_HEADS, Lp, LANES), lambda b, j: (b, 0, 0, 0)),
                  pl.BlockSpec((1, ATT_KV_HEADS, Lp, LANES), lambda b, j: (b, 0, 0, 0)),
                  pl.BlockSpec((C, C), lambda b, j: (0, 0))],
        out_specs=qspec(512),
        out_shape=jax.ShapeDtypeStruct((B, Lp, ATT_WIDTH), F32),
        scratch_shapes=[pltpu.VMEM((nch, Q_BLOCK, C), I32),
                        pltpu.VMEM((ATT_HEADS, Q_BLOCK, LANES), F32),
                        pltpu.VMEM((ATT_HEADS, Q_BLOCK, LANES), F32)],
        compiler_params=pltpu.CompilerParams(
            dimension_semantics=("arbitrary", "arbitrary"), vmem_limit_bytes=VMEM_LIMIT),
        name="attn",
    )(q, iq, ikf, ikb, kx, vx, u)


def _hgrn_sum_matrix(C):
    nlev = int(np.log2(C))
    t = np.arange(C)[:, None]
    jj = np.arange(C)[None, :]
    mats = [(jj <= t), (jj > t)]
    for l in range(nlev):
        s = 1 << l
        mid = ((t >> (l + 1)) << (l + 1)) + s - 1
        odd = ((t >> l) & 1) == 1
        mats.append(np.where(odd, (jj > mid) & (jj <= t), (jj > t) & (jj <= mid)))
    return np.concatenate(mats, axis=0).astype(np.float32), nlev


def _hgrn_kernel(hq_ref, hf_ref, hi_ref, hg_ref, lb_ref, gn_ref, e_ref, o_ref, st_ref, *, C, nlev):
    @pl.when(pl.program_id(2) == 0)
    def _():
        st_ref[...] = jnp.zeros(st_ref.shape, F32)

    lb = lb_ref[...]
    f = lb + (1.0 - lb) * jax.nn.sigmoid(hf_ref[0])
    logf = jnp.log(f)
    kk = 1.0 - f
    q = hq_ref[0] * (HG_DK ** -0.5)
    v = hi_ref[0].astype(BF16)

    p1 = logf.astype(BF16)
    r1 = logf - p1.astype(F32)
    p2 = r1.astype(BF16)
    p3 = (r1 - p2.astype(F32)).astype(BF16)
    x3 = _dot(e_ref[...], jnp.concatenate([p1, p2, p3], axis=1))
    xs = x3[:, :LANES] + x3[:, LANES:2 * LANES] + x3[:, 2 * LANES:]

    b = xs[0:C]
    suffix = xs[C:2 * C]
    rt = lax.broadcasted_iota(I32, (C, C), 0)
    ct = lax.broadcasted_iota(I32, (C, C), 1)
    row = lax.broadcasted_iota(I32, (C, 1), 0)

    scores = jnp.where(rt == ct, _dot_nt(q.astype(BF16), kk.astype(BF16)), 0.0)
    for l in range(nlev):
        w = jnp.exp(xs[(2 + l) * C:(3 + l) * C])
        odd = ((row >> l) & 1) == 1
        a = (jnp.where(odd, q, kk) * w).astype(BF16)
        pair = (((rt >> l) & 1) == 1) & ((ct >> l) == (rt >> l) - 1)
        scores = scores + jnp.where(pair, _dot_nt(a, a), 0.0)

    st = st_ref[...]
    qb = (q * jnp.exp(b)).astype(BF16)
    o = _dot(scores.astype(BF16), v) + _dot_nt(qb, st.astype(BF16))
    ks = (kk * jnp.exp(suffix)).astype(BF16)
    st_ref[...] = jnp.exp(b[C - 1:C, :]) * st + _dot_tn(v, ks)

    o = o * lax.rsqrt(jnp.mean(o * o, axis=-1, keepdims=True) + EPS) * gn_ref[...]
    hg = hg_ref[0]
    o_ref[0] = o * (hg * jax.nn.sigmoid(hg))


def _hgrn(hq, hf, hi, hg, lb, gn):
    B, Lp, _ = hq.shape
    C = HG_CHUNK
    e_np, nlev = _hgrn_sum_matrix(C)
    e = jnp.asarray(e_np, BF16)
    blk = pl.BlockSpec((1, C, HG_DK), lambda b, h, c: (b, c, h))
    vec = pl.BlockSpec((1, HG_DK), lambda b, h, c: (0, h))
    return pl.pallas_call(
        functools.partial(_hgrn_kernel, C=C, nlev=nlev),
        grid=(B, HG_HEADS, Lp // C),
        in_specs=[blk, blk, blk, blk, vec, vec,
                  pl.BlockSpec(e.shape, lambda b, h, c: (0, 0))],
        out_specs=blk,
        out_shape=jax.ShapeDtypeStruct((B, Lp, HG_WIDTH), F32),
        scratch_shapes=[pltpu.VMEM((HG_DV, HG_DK), F32)],
        compiler_params=pltpu.CompilerParams(
            dimension_semantics=("arbitrary", "arbitrary", "arbitrary"), vmem_limit_bytes=VMEM_LIMIT),
        name="hgrn",
    )(hq, hf, hi, hg, lb.reshape(1, -1), gn.reshape(1, -1), e)


def _merge_kernel(att_ref, hgr_ref, ga_ref, gh_ref, h_ref, wa_ref, wh_ref, wo_ref, g_ref,
                  wr_ref, br_ref, lt_ref,
                  h2_ref, xrow_ref, topi_ref, gate_ref, rank_ref, cnt_ref, *, tm):
    mix = (jax.nn.sigmoid(ga_ref[0]) * _dot(att_ref[0].astype(BF16), wa_ref[...])
           + jax.nn.sigmoid(gh_ref[0]) * _dot(hgr_ref[0].astype(BF16), wh_ref[...]))
    h2 = h_ref[0] + _dot(mix.astype(BF16), wo_ref[...])
    h2_ref[0] = h2
    xn = h2 * lax.rsqrt(jnp.mean(h2 * h2, axis=-1, keepdims=True) + EPS) * g_ref[...]
    for k in range(D_MODEL // LANES):
        xrow_ref[pl.ds(k, tm, stride=8), :] = xn[:, k * LANES:(k + 1) * LANES]

    xh = xn.astype(BF16)
    xl = (xn - xh.astype(F32)).astype(BF16)
    l1 = _dot(xh, wr_ref[...])
    logits = l1[:, :LANES] + l1[:, LANES:] + _dot(xl, wr_ref[:, :LANES]) + br_ref[...]

    lane = _lane_iota((tm, LANES))
    lanef = lane.astype(F32)
    work = logits
    vals, sels, idxs = [], [], []
    for _ in range(TOP_K):
        m = jnp.max(work, axis=1, keepdims=True)
        idx = jnp.min(jnp.where(work == m, lanef, float(LANES)), axis=1, keepdims=True)
        sel = lanef == idx
        vals.append(m)
        idxs.append(idx)
        sels.append(sel)
        work = jnp.where(sel, NEG_INF, work)
    es = [jnp.exp(v - vals[0]) for v in vals]
    den = es[0] + es[1] + es[2] + es[3]

    member = jnp.zeros((tm, LANES), F32)
    for sel in sels:
        member = member + jnp.where(sel, 1.0, 0.0)
    before = _dot(lt_ref[...], member.astype(BF16))
    topi = jnp.zeros((tm, LANES), F32)
    gate = jnp.zeros((tm, LANES), F32)
    rank = jnp.zeros((tm, LANES), F32)
    for jx in range(TOP_K):
        here = lane == jx
        topi = jnp.where(here, idxs[jx], topi)
        gate = jnp.where(here, es[jx] / den, gate)
        rank = jnp.where(here, jnp.sum(jnp.where(sels[jx], before, 0.0), axis=1, keepdims=True), rank)
    topi_ref[0] = topi.astype(I32)
    gate_ref[0] = gate
    rank_ref[0] = rank.astype(I32)
    cnt_ref[0] = jnp.sum(member, axis=0, keepdims=True).astype(I32)


def _merge(att, hgr, ga, gh, h, wa, wh, wo, g, wr2, br):
    B, Lp, D = h.shape
    tm = _pick_tile(Lp, (320, 256, 128))
    nt = Lp // tm
    lt = jnp.asarray(np.tril(np.ones((tm, tm), np.float32), -1), BF16)
    row = lambda w: pl.BlockSpec((1, tm, w), lambda b, i: (b, i, 0))
    full = lambda a: pl.BlockSpec(a.shape, lambda b, i: (0,) * a.ndim)
    g2 = g.reshape(1, D)
    out_shape = [jax.ShapeDtypeStruct((B, Lp, D), F32),
                 jax.ShapeDtypeStruct((B * Lp * 8, LANES), F32),
                 jax.ShapeDtypeStruct((B, Lp, LANES), I32),
                 jax.ShapeDtypeStruct((B, Lp, LANES), F32),
                 jax.ShapeDtypeStruct((B, Lp, LANES), I32),
                 jax.ShapeDtypeStruct((B * nt, 1, LANES), I32)]
    out_specs = [row(D),
                 pl.BlockSpec((tm * 8, LANES), lambda b, i: (b * nt + i, 0)),
                 row(LANES), row(LANES), row(LANES),
                 pl.BlockSpec((1, 1, LANES), lambda b, i: (b * nt + i, 0, 0))]
    return pl.pallas_call(
        functools.partial(_merge_kernel, tm=tm),
        grid=(B, nt),
        in_specs=[row(512), row(512), row(D), row(D), row(D),
                  full(wa), full(wh), full(wo), full(g2), full(wr2), full(br), full(lt)],
        out_specs=out_specs,
        out_shape=out_shape,
        compiler_params=pltpu.CompilerParams(
            dimension_semantics=("arbitrary", "arbitrary"), vmem_limit_bytes=VMEM_LIMIT),
        name="merge",
    )(att, hgr, ga, gh, h, wa, wh, wo, g2, wr2, br, lt), tm


def _slot(topi_ref, rank_ref, base_ref, a):
    return base_ref[0, 0, topi_ref[0, 0, a]] + rank_ref[0, 0, a]


def _dispatch_kernel(topi_ref, rank_ref, base_ref, x_hbm, xs_in, xs_hbm, sem, *, tm):
    del xs_in
    i = pl.program_id(0)
    n = tm * TOP_K

    def copy(a):
        return pltpu.make_async_copy(
            x_hbm.at[i * tm + a // TOP_K], xs_hbm.at[_slot(topi_ref, rank_ref, base_ref, a)], sem)

    def issue(a, c):
        copy(a).start()
        return c

    def drain(a, c):
        copy(a).wait()
        return c

    lax.fori_loop(0, n, issue, 0)
    lax.fori_loop(0, n, drain, 0)


def _dispatch(topi, rank, base, xrow, cap, tm):
    nt = topi.shape[0]
    smem = lambda w: pl.BlockSpec((1, 1, w), lambda i: (i, 0, 0), memory_space=pltpu.SMEM)
    zeros = jnp.zeros((cap, 8, LANES), F32)
    return pl.pallas_call(
        functools.partial(_dispatch_kernel, tm=tm),
        grid=(nt,),
        in_specs=[smem(tm * TOP_K), smem(tm * TOP_K), smem(LANES),
                  pl.BlockSpec(memory_space=pl.ANY), pl.BlockSpec(memory_space=pl.ANY)],
        out_specs=pl.BlockSpec(memory_space=pl.ANY),
        out_shape=jax.ShapeDtypeStruct((cap, 8, LANES), F32),
        scratch_shapes=[pltpu.SemaphoreType.DMA(())],
        input_output_aliases={4: 0},
        compiler_params=pltpu.CompilerParams(dimension_semantics=("arbitrary",)),
        name="dispatch",
    )(topi, rank, base, xrow, zeros)


def _ffn_kernel(be_ref, nu_ref, x_ref, wg_ref, bg_ref, wu_ref, bu_ref, wd_ref, bd_ref, o_ref, *, blk):
    i = pl.program_id(0)
    nk = D_MODEL // LANES

    @pl.when(i < nu_ref[0])
    def _():
        xb = jnp.concatenate([x_ref[pl.ds(k, blk, stride=8), :] for k in range(nk)], axis=1).astype(BF16)
        g = _dot(xb, wg_ref[0]) + bg_ref[0]
        u = _dot(xb, wu_ref[0]) + bu_ref[0]
        g = jnp.minimum(g, SWIGLU_LIMIT)
        u = jnp.clip(u, -SWIGLU_LIMIT, SWIGLU_LIMIT)
        act = (u + 1.0) * (g * jax.nn.sigmoid(SWIGLU_ALPHA * g))
        out = _dot(act.astype(BF16), wd_ref[0]) + bd_ref[0]
        for k in range(nk):
            o_ref[pl.ds(k, blk, stride=8), :] = out[:, k * LANES:(k + 1) * LANES]

    @pl.when(i >= nu_ref[0])
    def _():
        o_ref[...] = jnp.zeros(o_ref.shape, F32)


def _ffn(block_expert, n_used, xs, wg, bg, wu, bu, wd, bd, blk):
    cap = xs.shape[0]
    nblk = cap // blk
    x2 = xs.reshape(cap * 8, LANES)
    wspec = pl.BlockSpec((1, D_MODEL, D_FF), lambda i, be, nu: (be[i], 0, 0))
    wdspec = pl.BlockSpec((1, D_FF, D_MODEL), lambda i, be, nu: (be[i], 0, 0))
    bspec = lambda w: pl.BlockSpec((1, 1, w), lambda i, be, nu: (be[i], 0, 0))
    rows = pl.BlockSpec((blk * 8, LANES), lambda i, be, nu: (i, 0))
    out = pl.pallas_call(
        functools.partial(_ffn_kernel, blk=blk),
        grid_spec=pltpu.PrefetchScalarGridSpec(
            num_scalar_prefetch=2, grid=(nblk,),
            in_specs=[rows, wspec, bspec(D_FF), wspec, bspec(D_FF), wdspec, bspec(D_MODEL)],
            out_specs=rows),
        out_shape=jax.ShapeDtypeStruct((cap * 8, LANES), F32),
        compiler_params=pltpu.CompilerParams(
            dimension_semantics=("arbitrary",), vmem_limit_bytes=VMEM_LIMIT),
        name="ffn",
    )(block_expert, n_used, x2, wg, bg, wu, bu, wd, bd)
    return out.reshape(cap, 8, LANES)


def _combine_kernel(topi_ref, rank_ref, base_ref, gate_ref, h2_ref, g_ref, ys_hbm, o_ref,
                    buf_ref, sem, *, tm):
    n = tm * TOP_K

    def copy(a):
        t = a // TOP_K
        jx = a % TOP_K
        dst = buf_ref.at[pl.ds(pl.multiple_of((jx * tm + t) * 8, 8), 8), :]
        return pltpu.make_async_copy(ys_hbm.at[_slot(topi_ref, rank_ref, base_ref, a)], dst, sem)

    def issue(a, c):
        copy(a).start()
        return c

    def drain(a, c):
        copy(a).wait()
        return c

    lax.fori_loop(0, n, issue, 0)
    lax.fori_loop(0, n, drain, 0)

    gate = gate_ref[0]
    h2 = h2_ref[0]
    cols = []
    for k in range(D_MODEL // LANES):
        y = h2[:, k * LANES:(k + 1) * LANES]
        for jx in range(TOP_K):
            y = y + gate[:, jx:jx + 1] * buf_ref[pl.ds(jx * tm * 8 + k, tm, stride=8), :]
        cols.append(y)
    ho = jnp.concatenate(cols, axis=1)
    o_ref[0] = ho * lax.rsqrt(jnp.mean(ho * ho, axis=-1, keepdims=True) + EPS) * g_ref[...]


def _combine(topi, rank, base, gate, h2, g, ys, tm):
    B, Lp, D = h2.shape
    nt = Lp // tm
    smem = lambda w: pl.BlockSpec((1, 1, w), lambda b, i: (b * nt + i, 0, 0), memory_space=pltpu.SMEM)
    row = lambda w: pl.BlockSpec((1, tm, w), lambda b, i: (b, i, 0))
    return pl.pallas_call(
        functools.partial(_combine_kernel, tm=tm),
        grid=(B, nt),
        in_specs=[smem(tm * TOP_K), smem(tm * TOP_K), smem(LANES), row(LANES), row(D),
                  pl.BlockSpec((1, D), lambda b, i: (0, 0)),
                  pl.BlockSpec(memory_space=pl.ANY)],
        out_specs=row(D),
        out_shape=jax.ShapeDtypeStruct((B, Lp, D), F32),
        scratch_shapes=[pltpu.VMEM((TOP_K * tm * 8, LANES), F32), pltpu.SemaphoreType.DMA(())],
        compiler_params=pltpu.CompilerParams(
            dimension_semantics=("arbitrary", "arbitrary"), vmem_limit_bytes=VMEM_LIMIT),
        name="combine",
    )(topi, rank, base, gate, h2, g.reshape(1, D), ys)


def _moe(h2, xrow, topi, gate, rank, cnt, tm, ffn_w, final_g):
    B, Lp, D = h2.shape
    T = B * Lp
    nt_total = cnt.shape[0]
    blk = FFN_BLOCK
    nblk = -(-(T * TOP_K) // blk) + N_EXPERTS
    cap = nblk * blk
    cnt2 = cnt.reshape(nt_total, LANES)
    totals = jnp.sum(cnt2, axis=0)
    padded = (totals + blk - 1) // blk * blk
    pad_ends = jnp.cumsum(padded)
    pad_starts = pad_ends - padded
    base = (pad_starts[None, :] + jnp.cumsum(cnt2, axis=0) - cnt2).astype(I32).reshape(nt_total, 1, LANES)
    block_expert = jnp.minimum(
        jnp.searchsorted(pad_ends[:N_EXPERTS], jnp.arange(nblk) * blk, side="right"), N_EXPERTS - 1).astype(I32)
    n_used = (pad_ends[N_EXPERTS - 1] // blk).astype(I32).reshape(1)
    topi_s = topi[:, :, :TOP_K].reshape(nt_total, 1, tm * TOP_K)
    rank_s = rank[:, :, :TOP_K].reshape(nt_total, 1, tm * TOP_K)

    xs = _dispatch(topi_s, rank_s, base, xrow.reshape(T, 8, LANES), cap, tm)
    ys = _ffn(block_expert, n_used, xs, *ffn_w, blk)
    return _combine(topi_s, rank_s, base, gate, h2, final_g, ys, tm)


def kernel(x, meta_tokens, attn_norm_g, w_in, hgrn_norm_g, w_up_attn, w_up_hgrn, w_out,
           hgrn_lb_logits, ffn_norm_g, router_w, router_b, w_gate, b_gate, w_up, b_up,
           w_down, b_down, final_norm_g):
    B, S, D = x.shape
    L = S + N_META
    Lp = -(-L // Q_BLOCK) * Q_BLOCK
    k_sel = min(TOPK_MAX, L // 4)
    meta = jnp.broadcast_to(meta_tokens[None].astype(x.dtype), (B, N_META, D))
    h = jnp.concatenate([meta, x, jnp.zeros((B, Lp - L, D), x.dtype)], axis=1)
    lb = jnp.cumsum(jax.nn.softmax(hgrn_lb_logits.astype(F32), axis=0), axis=0)[0]

    (q, kx, vx, iq, ikf, ikb, hq, hf, hi, hg, ga, gh) = _inproj(h, attn_norm_g[0], _arrange_w_in(w_in[0]))
    att = _attn(q, iq, ikf, ikb, kx, vx, k_sel)
    hgr = _hgrn(hq, hf, hi, hg, lb, hgrn_norm_g[0])

    wr = jnp.pad(router_w[0], ((0, 0), (0, LANES - N_EXPERTS)))
    wr_hi = wr.astype(BF16)
    wr_lo = (wr - wr_hi.astype(F32)).astype(BF16)
    br = jnp.pad(router_b[0], (0, LANES - N_EXPERTS), constant_values=-1e30).reshape(1, LANES)
    (h2, xrow, topi, gate, rank, cnt), tm = _merge(
        att, hgr, ga, gh, h, w_up_attn[0].astype(BF16), w_up_hgrn[0].astype(BF16), w_out[0].astype(BF16),
        ffn_norm_g[0], jnp.concatenate([wr_hi, wr_lo], axis=1), br)

    ffn_w = (w_gate[0].astype(BF16), b_gate[0].reshape(N_EXPERTS, 1, D_FF),
             w_up[0].astype(BF16), b_up[0].reshape(N_EXPERTS, 1, D_FF),
             w_down[0].astype(BF16), b_down[0].reshape(N_EXPERTS, 1, D_MODEL))
    out = _moe(h2, xrow, topi, gate, rank, cnt, tm, ffn_w, final_norm_g)
    return out[:, N_META:L]
```

```python
import functools

import numpy as np
import jax
import jax.numpy as jnp
from jax import lax
from jax.experimental import pallas as pl
from jax.experimental.pallas import tpu as pltpu

F32 = jnp.float32
BF16 = jnp.bfloat16
I32 = jnp.int32

D_MODEL = 1024
N_META = 16
EPS = 1e-5
ROPE_THETA = 500000.0
ATT_HEADS = 8
ATT_KV_HEADS = 2
ATT_HEAD_DIM = 64
ATT_WIDTH = ATT_HEADS * ATT_HEAD_DIM
IDX_HEADS = 8
IDX_DIM = 32
TOPK_MAX = 256
Q_BLOCK = 128
HG_HEADS = 4
HG_DK = 128
HG_DV = 128
HG_WIDTH = HG_HEADS * HG_DV
N_EXPERTS = 32
TOP_K = 4
D_FF = 1024
SWIGLU_LIMIT = 7.0
SWIGLU_ALPHA = 1.702

LANES = 128
VMEM_LIMIT = 56 * 1024 * 1024
INT_MIN = -2 ** 31
NEG_INF = float("-inf")
LOG2E = 1.4426950408889634

HG_CHUNK = 128
FFN_BLOCK = 256


def _pick_tile(n, candidates):
    for c in candidates:
        if n % c == 0:
            return c
    raise ValueError(f"no tile for {n}")


def _dot(a, b):
    return jnp.dot(a, b, preferred_element_type=F32)


def _dot_nt(a, b):
    return lax.dot_general(a, b, (((1,), (1,)), ((), ())), preferred_element_type=F32)


def _dot_tn(a, b):
    return lax.dot_general(a, b, (((0,), (0,)), ((), ())), preferred_element_type=F32)


def _lane_iota(shape):
    return lax.broadcasted_iota(I32, shape, len(shape) - 1)


def _rope_tables(n_pos, head_dim, lane_valid):
    rot = head_dim // 4
    half = rot // 2
    inv = ROPE_THETA ** (-jnp.arange(half, dtype=F32) * 2.0 / rot)
    ang = jnp.arange(n_pos).astype(F32)[:, None] * inv[None, :]
    c = jnp.cos(ang)
    s = jnp.sin(ang)
    lane = np.arange(LANES)
    within = lane % head_dim
    valid = lane < lane_valid
    first = (within < half) & valid
    second = (within >= half) & (within < rot) & valid
    idx = np.where(within < half, within, within - half) % half
    cc = c[:, idx]
    ss = s[:, idx]
    tc = jnp.where((first | second)[None, :], cc, 1.0)
    t1 = jnp.where(first[None, :], -ss, 0.0)
    t2 = jnp.where(second[None, :], ss, 0.0)
    return jnp.stack([tc, t1, t2]).astype(F32), half


def _rope_apply(x, tab_ref, half):
    c = tab_ref[0]
    s1 = tab_ref[1]
    s2 = tab_ref[2]
    cols = []
    for j in range(x.shape[1] // LANES):
        xc = x[:, j * LANES:(j + 1) * LANES]
        cols.append(xc * c + pltpu.roll(xc, LANES - half, 1) * s1 + pltpu.roll(xc, half, 1) * s2)
    return cols[0] if len(cols) == 1 else jnp.concatenate(cols, axis=1)


_W_Q = (0, 512)
_W_KX = (512, 768)
_W_VX = (768, 1024)
_W_IQ = (1024, 1280)
_W_IK = (1280, 1408)
_W_HQ = (1408, 1920)
_W_HF = (1920, 2432)
_W_HI = (2432, 2944)
_W_HG = (2944, 3456)
_W_GA = (3456, 4480)
_W_GH = (4480, 5504)
_W_COLS = 5504


def _arrange_w_in(w_in):
    o = np.cumsum([0, 512, 128, 128, 256, 32, 8, 512, 512, 512, 512, 1024, 1024])
    seg = [w_in[:, o[i]:o[i + 1]] for i in range(12)]
    q, k, v, iq, ik, iw, hq, hf, hi, hg, ga, gh = seg
    z64 = jnp.zeros((D_MODEL, 64), w_in.dtype)
    z88 = jnp.zeros((D_MODEL, 88), w_in.dtype)
    kx = jnp.concatenate([k[:, :64], z64, k[:, 64:], z64], axis=1)
    vx = jnp.concatenate([v[:, :64], z64, v[:, 64:], z64], axis=1)
    ikx = jnp.concatenate([ik, iw, z88], axis=1)
    w = jnp.concatenate([q, kx, vx, iq, ikx, hq, hf, hi, hg, ga, gh], axis=1)
    assert w.shape[1] == _W_COLS
    return w.astype(BF16)


def _inproj_kernel(h_ref, g_ref, w_ref, tatt_ref, tkx_ref, tiq_ref, tik_ref,
                   q_ref, kx_ref, vx_ref, iq_ref, ikf_ref, ikb_ref,
                   hq_ref, hf_ref, hi_ref, hg_ref, ga_ref, gh_ref, *, half_att, half_idx):
    x = h_ref[0]
    xn = x * lax.rsqrt(jnp.mean(x * x, axis=-1, keepdims=True) + EPS) * g_ref[...]
    xb = xn.astype(BF16)

    def proj(rng):
        return _dot(xb, w_ref[:, rng[0]:rng[1]])

    q_ref[0] = _rope_apply(proj(_W_Q), tatt_ref, half_att)
    kx = _rope_apply(proj(_W_KX), tkx_ref, half_att).astype(BF16)
    kx_ref[0, 0] = kx[:, :LANES]
    kx_ref[0, 1] = kx[:, LANES:]
    vx = proj(_W_VX)
    ones_hi = jnp.where(_lane_iota(vx.shape) % LANES >= ATT_HEAD_DIM, 1.0, 0.0)
    vx = (vx + ones_hi).astype(BF16)
    vx_ref[0, 0] = vx[:, :LANES]
    vx_ref[0, 1] = vx[:, LANES:]
    iq_ref[0] = _rope_apply(proj(_W_IQ), tiq_ref, half_idx)
    ik = _rope_apply(proj(_W_IK), tik_ref, half_idx)
    ikf_ref[0] = ik
    ikb_ref[0] = ik.astype(BF16)
    hq_ref[0] = proj(_W_HQ)
    hf_ref[0] = proj(_W_HF)
    hi_ref[0] = proj(_W_HI)
    hg_ref[0] = proj(_W_HG)
    ga_ref[0] = proj(_W_GA)
    gh_ref[0] = proj(_W_GH)


def _inproj(h, g, w_arr):
    B, Lp, D = h.shape
    tm = _pick_tile(Lp, (320, 256, 128))
    tatt, half_att = _rope_tables(Lp, ATT_HEAD_DIM, LANES)
    tkx, _ = _rope_tables(Lp, ATT_HEAD_DIM, ATT_HEAD_DIM)
    tiq, half_idx = _rope_tables(Lp, IDX_DIM, LANES)
    tik, _ = _rope_tables(Lp, IDX_DIM, IDX_DIM)

    def rows(width, dtype=F32):
        return (jax.ShapeDtypeStruct((B, Lp, width), dtype),
                pl.BlockSpec((1, tm, width), lambda b, i: (b, i, 0)))

    def kv(dtype=BF16):
        return (jax.ShapeDtypeStruct((B, ATT_KV_HEADS, Lp, LANES), dtype),
                pl.BlockSpec((1, ATT_KV_HEADS, tm, LANES), lambda b, i: (b, 0, i, 0)))

    outs = [rows(512), kv(), kv(), rows(256), rows(128), rows(128, BF16),
            rows(512), rows(512), rows(512), rows(512), rows(1024), rows(1024)]
    tab_spec = pl.BlockSpec((3, tm, LANES), lambda b, i: (0, i, 0))
    return pl.pallas_call(
        functools.partial(_inproj_kernel, half_att=half_att, half_idx=half_idx),
        grid=(B, Lp // tm),
        in_specs=[pl.BlockSpec((1, tm, D), lambda b, i: (b, i, 0)),
                  pl.BlockSpec((1, D), lambda b, i: (0, 0)),
                  pl.BlockSpec((D, _W_COLS), lambda b, i: (0, 0)),
                  tab_spec, tab_spec, tab_spec, tab_spec],
        out_specs=[o[1] for o in outs],
        out_shape=[o[0] for o in outs],
        compiler_params=pltpu.CompilerParams(
            dimension_semantics=("arbitrary", "arbitrary"), vmem_limit_bytes=VMEM_LIMIT),
        name="inproj",
    )(h, g.reshape(1, D), w_arr, tatt, tkx, tiq, tik)


def _attn_kernel(q_ref, iq_ref, ikf_ref, ikb_ref, kx_ref, vx_ref, u_ref, o_ref,
                 keys_ref, acc_ref, m_ref, bias_ref, lg_ref, p_ref, *, C, k_sel):
    j = pl.program_id(1)
    nck = ((j + 1) * Q_BLOCK + C - 1) // C
    lane = _lane_iota((Q_BLOCK, LANES))
    rowpos = j * Q_BLOCK + lax.broadcasted_iota(I32, (Q_BLOCK, 1), 0)

    iqf = iq_ref[0]
    wq = ikf_ref[0]
    per_blk = LANES // IDX_DIM
    iqh = []
    for h in range(IDX_HEADS):
        blk = iqf[:, (h // per_blk) * LANES:(h // per_blk + 1) * LANES]
        off = (h % per_blk) * IDX_DIM
        if off:
            blk = pltpu.roll(blk, LANES - off, 1)
        iqh.append(jnp.where(lane < IDX_DIM, blk, 0.0).astype(BF16))
    wcols = [wq[:, IDX_DIM + h:IDX_DIM + h + 1] for h in range(IDX_HEADS)]

    def causal_mask(c):
        kpos = c * C + lax.broadcasted_iota(I32, (1, C), 1)
        return kpos <= rowpos

    def phase_a(c, carry):
        ks = ikb_ref[0, pl.ds(pl.multiple_of(c * C, C), C), :]
        s = jnp.zeros((Q_BLOCK, C), F32)
        for h in range(IDX_HEADS):
            s = s + jnp.maximum(_dot_nt(iqh[h], ks), 0.0) * wcols[h]
        bits = pltpu.bitcast(s, I32)
        key = jnp.where(bits < 0, bits ^ jnp.int32(0x7FFFFFFF), bits)
        keys_ref[c] = jnp.where(causal_mask(c), key, jnp.int32(INT_MIN))
        return carry

    lax.fori_loop(0, nck, phase_a, 0)

    def count(cand, strict):
        cand_b = jnp.broadcast_to(cand, (Q_BLOCK, LANES))

        def body(c, acc):
            kc = keys_ref[c]
            for t in range(C // LANES):
                blk = kc[:, t * LANES:(t + 1) * LANES]
                hit = (blk > cand_b) if strict else (blk >= cand_b)
                acc = acc + jnp.where(hit, 1, 0)
            return acc

        acc = lax.fori_loop(0, nck, body, jnp.zeros((Q_BLOCK, LANES), I32))
        return jnp.sum(acc, axis=1, keepdims=True)

    def search(i, carry):
        prefix, nge = carry
        bit = lax.shift_left(jnp.int32(1), jnp.int32(31) - i)
        cand = prefix ^ bit
        cnt = count(cand, False)
        ok = cnt >= k_sel
        return jnp.where(ok, cand, prefix), jnp.where(ok, cnt, nge)

    thr, nge = lax.fori_loop(
        0, 32, search,
        (jnp.full((Q_BLOCK, 1), INT_MIN, I32), jnp.full((Q_BLOCK, 1), k_sel, I32)))
    ngt = count(thr, True)
    need = (k_sel - ngt).astype(F32)
    excess = jnp.where((nge > k_sel) & (thr != INT_MIN), 1, 0)
    any_excess = jnp.max(excess) > 0
    thr_b = jnp.broadcast_to(thr, (Q_BLOCK, C))
    thr_lo = jnp.maximum(thr_b, jnp.int32(INT_MIN + 1))

    groups = ATT_HEADS // ATT_KV_HEADS
    qf = q_ref[0]
    qh = []
    for h in range(ATT_HEADS):
        blk = qf[:, (h // 2) * LANES:(h // 2 + 1) * LANES]
        if h % 2:
            blk = pltpu.roll(blk, ATT_HEAD_DIM, 1)
        qh.append((jnp.where(lane < ATT_HEAD_DIM, blk, 0.0) * (LOG2E * ATT_HEAD_DIM ** -0.5)).astype(BF16))

    acc_ref[...] = jnp.zeros(acc_ref.shape, F32)
    m_ref[...] = jnp.full(m_ref.shape, NEG_INF, F32)

    def phase_c(c, eq_seen):
        kc = keys_ref[c]

        def bias_simple():
            return jnp.where(kc >= thr_lo, 0.0, NEG_INF), eq_seen

        def bias_ties():
            eq = (kc == thr_b) & causal_mask(c)
            eqf = jnp.where(eq, 1.0, 0.0)
            rank = eq_seen + _dot(eqf.astype(BF16), u_ref[...])
            take = (kc > thr_b) | (eq & (rank < need))
            return jnp.where(take, 0.0, NEG_INF), eq_seen + jnp.sum(eqf, axis=1, keepdims=True)

        bias, eq_next = lax.cond(any_excess, bias_ties, bias_simple)
        bias_ref[...] = bias
        start = pl.multiple_of(c * C, C)
        for h in range(ATT_HEADS):
            kch = kx_ref[0, h // groups, pl.ds(start, C), :]
            lg_ref[h] = _dot_nt(qh[h], kch) + bias_ref[...]
        shifts, alphas = [], []
        for h in range(ATT_HEADS):
            m_old = m_ref[h]
            m_new = jnp.maximum(m_old, jnp.max(lg_ref[h], axis=1, keepdims=True))
            m_ref[h] = m_new
            m_safe = jnp.where(m_new == NEG_INF, 0.0, m_new)
            shifts.append(m_safe[:, :1])
            alphas.append(jnp.exp2(m_old - m_safe))
        for h in range(ATT_HEADS):
            p_ref[h] = jnp.exp2(lg_ref[h] - shifts[h]).astype(BF16)
        for h in range(ATT_HEADS):
            vch = vx_ref[0, h // groups, pl.ds(start, C), :]
            acc_ref[h] = alphas[h] * acc_ref[h] + _dot(p_ref[h], vch)
        return eq_next

    lax.fori_loop(0, nck, phase_c, jnp.zeros((Q_BLOCK, 1), F32))

    for pair in range(ATT_HEADS // 2):
        a = acc_ref[2 * pair]
        b = acc_ref[2 * pair + 1]
        oa = a / pltpu.roll(a, ATT_HEAD_DIM, 1)
        ob = b / pltpu.roll(b, ATT_HEAD_DIM, 1)
        o_ref[0, :, pair * LANES:(pair + 1) * LANES] = jnp.where(
            lane < ATT_HEAD_DIM, oa, pltpu.roll(ob, ATT_HEAD_DIM, 1))


def _attn(q, iq, ikf, ikb, kx, vx, k_sel):
    B, Lp, _ = q.shape
    C = _pick_tile(Lp, (640, 512, 256, 128))
    nch = Lp // C
    u = jnp.asarray(np.triu(np.ones((C, C), np.float32), 1), BF16)
    qspec = lambda w: pl.BlockSpec((1, Q_BLOCK, w), lambda b, j: (b, j, 0))
    return pl.pallas_call(
        functools.partial(_attn_kernel, C=C, k_sel=k_sel),
        grid=(B, Lp // Q_BLOCK),
        in_specs=[qspec(512), qspec(256), qspec(128),
                  pl.BlockSpec((1, Lp, LANES), lambda b, j: (b, 0, 0)),
                  pl.BlockSpec((1, ATT_KV_HEADS, Lp, LANES), lambda b, j: (b, 0, 0, 0)),
                  pl.BlockSpec((1, ATT_KV_HEADS, Lp, LANES), lambda b, j: (b, 0, 0, 0)),
                  pl.BlockSpec((C, C), lambda b, j: (0, 0))],
        out_specs=qspec(512),
        out_shape=jax.ShapeDtypeStruct((B, Lp, ATT_WIDTH), F32),
        scratch_shapes=[pltpu.VMEM((nch, Q_BLOCK, C), I32),
                        pltpu.VMEM((ATT_HEADS, Q_BLOCK, LANES), F32),
                        pltpu.VMEM((ATT_HEADS, Q_BLOCK, LANES), F32),
                        pltpu.VMEM((Q_BLOCK, C), F32),
                        pltpu.VMEM((ATT_HEADS, Q_BLOCK, C), F32),
                        pltpu.VMEM((ATT_HEADS, Q_BLOCK, C), BF16)],
        compiler_params=pltpu.CompilerParams(
            dimension_semantics=("arbitrary", "arbitrary"), vmem_limit_bytes=VMEM_LIMIT),
        name="attn",
    )(q, iq, ikf, ikb, kx, vx, u)


def _hgrn_sum_matrix(C):
    nlev = int(np.log2(C))
    t = np.arange(C)[:, None]
    jj = np.arange(C)[None, :]
    mats = [(jj <= t), (jj > t)]
    for l in range(nlev):
        s = 1 << l
        mid = ((t >> (l + 1)) << (l + 1)) + s - 1
        odd = ((t >> l) & 1) == 1
        mats.append(np.where(odd, (jj > mid) & (jj <= t), (jj > t) & (jj <= mid)))
    return np.concatenate(mats, axis=0).astype(np.float32), nlev


def _hgrn_kernel(hq_ref, hf_ref, hi_ref, hg_ref, lb_ref, gn_ref, e_ref, o_ref, st_ref, *, C, nlev):
    @pl.when(pl.program_id(2) == 0)
    def _():
        st_ref[...] = jnp.zeros(st_ref.shape, F32)

    lb = lb_ref[...]
    f = lb + (1.0 - lb) * jax.nn.sigmoid(hf_ref[0])
    logf = jnp.log(f)
    kk = 1.0 - f
    q = hq_ref[0] * (HG_DK ** -0.5)
    v = hi_ref[0].astype(BF16)

    p1 = logf.astype(BF16)
    r1 = logf - p1.astype(F32)
    p2 = r1.astype(BF16)
    p3 = (r1 - p2.astype(F32)).astype(BF16)
    x3 = _dot(e_ref[...], jnp.concatenate([p1, p2, p3], axis=1))
    xs = x3[:, :LANES] + x3[:, LANES:2 * LANES] + x3[:, 2 * LANES:]

    b = xs[0:C]
    suffix = xs[C:2 * C]
    rt = lax.broadcasted_iota(I32, (C, C), 0)
    ct = lax.broadcasted_iota(I32, (C, C), 1)
    row = lax.broadcasted_iota(I32, (C, 1), 0)

    scores = jnp.where(rt == ct, _dot_nt(q.astype(BF16), kk.astype(BF16)), 0.0)
    for l in range(nlev):
        w = jnp.exp(xs[(2 + l) * C:(3 + l) * C])
        odd = ((row >> l) & 1) == 1
        a = (jnp.where(odd, q, kk) * w).astype(BF16)
        pair = (((rt >> l) & 1) == 1) & ((ct >> l) == (rt >> l) - 1)
        scores = scores + jnp.where(pair, _dot_nt(a, a), 0.0)

    st = st_ref[...]
    qb = (q * jnp.exp(b)).astype(BF16)
    o = _dot(scores.astype(BF16), v) + _dot_nt(qb, st.astype(BF16))
    ks = (kk * jnp.exp(suffix)).astype(BF16)
    st_ref[...] = jnp.exp(b[C - 1:C, :]) * st + _dot_tn(v, ks)

    o = o * lax.rsqrt(jnp.mean(o * o, axis=-1, keepdims=True) + EPS) * gn_ref[...]
    hg = hg_ref[0]
    o_ref[0] = o * (hg * jax.nn.sigmoid(hg))


def _hgrn(hq, hf, hi, hg, lb, gn):
    B, Lp, _ = hq.shape
    C = HG_CHUNK
    e_np, nlev = _hgrn_sum_matrix(C)
    e = jnp.asarray(e_np, BF16)
    blk = pl.BlockSpec((1, C, HG_DK), lambda b, h, c: (b, c, h))
    vec = pl.BlockSpec((1, HG_DK), lambda b, h, c: (0, h))
    return pl.pallas_call(
        functools.partial(_hgrn_kernel, C=C, nlev=nlev),
        grid=(B, HG_HEADS, Lp // C),
        in_specs=[blk, blk, blk, blk, vec, vec,
                  pl.BlockSpec(e.shape, lambda b, h, c: (0, 0))],
        out_specs=blk,
        out_shape=jax.ShapeDtypeStruct((B, Lp, HG_WIDTH), F32),
        scratch_shapes=[pltpu.VMEM((HG_DV, HG_DK), F32)],
        compiler_params=pltpu.CompilerParams(
            dimension_semantics=("arbitrary", "arbitrary", "arbitrary"), vmem_limit_bytes=VMEM_LIMIT),
        name="hgrn",
    )(hq, hf, hi, hg, lb.reshape(1, -1), gn.reshape(1, -1), e)


def _merge_kernel(att_ref, hgr_ref, ga_ref, gh_ref, h_ref, wa_ref, wh_ref, wo_ref, g_ref,
                  wr_ref, br_ref, lt_ref,
                  h2_ref, xrow_ref, topi_ref, gate_ref, rank_ref, cnt_ref, *, tm):
    mix = (jax.nn.sigmoid(ga_ref[0]) * _dot(att_ref[0].astype(BF16), wa_ref[...])
           + jax.nn.sigmoid(gh_ref[0]) * _dot(hgr_ref[0].astype(BF16), wh_ref[...]))
    h2 = h_ref[0] + _dot(mix.astype(BF16), wo_ref[...])
    h2_ref[0] = h2
    xn = h2 * lax.rsqrt(jnp.mean(h2 * h2, axis=-1, keepdims=True) + EPS) * g_ref[...]
    for k in range(D_MODEL // LANES):
        xrow_ref[pl.ds(k, tm, stride=8), :] = xn[:, k * LANES:(k + 1) * LANES]

    xh = xn.astype(BF16)
    xl = (xn - xh.astype(F32)).astype(BF16)
    l1 = _dot(xh, wr_ref[...])
    logits = l1[:, :LANES] + l1[:, LANES:] + _dot(xl, wr_ref[:, :LANES]) + br_ref[...]

    lane = _lane_iota((tm, LANES))
    lanef = lane.astype(F32)
    work = logits
    vals, sels, idxs = [], [], []
    for _ in range(TOP_K):
        m = jnp.max(work, axis=1, keepdims=True)
        idx = jnp.min(jnp.where(work == m, lanef, float(LANES)), axis=1, keepdims=True)
        sel = lanef == idx
        vals.append(m)
        idxs.append(idx)
        sels.append(sel)
        work = jnp.where(sel, NEG_INF, work)
    es = [jnp.exp(v - vals[0]) for v in vals]
    den = es[0] + es[1] + es[2] + es[3]

    member = jnp.zeros((tm, LANES), F32)
    for sel in sels:
        member = member + jnp.where(sel, 1.0, 0.0)
    before = _dot(lt_ref[...], member.astype(BF16))
    topi = jnp.zeros((tm, LANES), F32)
    gate = jnp.zeros((tm, LANES), F32)
    rank = jnp.zeros((tm, LANES), F32)
    for jx in range(TOP_K):
        here = lane == jx
        topi = jnp.where(here, idxs[jx], topi)
        gate = jnp.where(here, es[jx] / den, gate)
        rank = jnp.where(here, jnp.sum(jnp.where(sels[jx], before, 0.0), axis=1, keepdims=True), rank)
    topi_ref[0] = topi.astype(I32)
    gate_ref[0] = gate
    rank_ref[0] = rank.astype(I32)
    cnt_ref[0] = jnp.sum(member, axis=0, keepdims=True).astype(I32)


def _merge(att, hgr, ga, gh, h, wa, wh, wo, g, wr2, br):
    B, Lp, D = h.shape
    tm = _pick_tile(Lp, (320, 256, 128))
    nt = Lp // tm
    lt = jnp.asarray(np.tril(np.ones((tm, tm), np.float32), -1), BF16)
    row = lambda w: pl.BlockSpec((1, tm, w), lambda b, i: (b, i, 0))
    full = lambda a: pl.BlockSpec(a.shape, lambda b, i: (0,) * a.ndim)
    g2 = g.reshape(1, D)
    out_shape = [jax.ShapeDtypeStruct((B, Lp, D), F32),
                 jax.ShapeDtypeStruct((B * Lp * 8, LANES), F32),
                 jax.ShapeDtypeStruct((B, Lp, LANES), I32),
                 jax.ShapeDtypeStruct((B, Lp, LANES), F32),
                 jax.ShapeDtypeStruct((B, Lp, LANES), I32),
                 jax.ShapeDtypeStruct((B * nt, 1, LANES), I32)]
    out_specs = [row(D),
                 pl.BlockSpec((tm * 8, LANES), lambda b, i: (b * nt + i, 0)),
                 row(LANES), row(LANES), row(LANES),
                 pl.BlockSpec((1, 1, LANES), lambda b, i: (b * nt + i, 0, 0))]
    return pl.pallas_call(
        functools.partial(_merge_kernel, tm=tm),
        grid=(B, nt),
        in_specs=[row(512), row(512), row(D), row(D), row(D),
                  full(wa), full(wh), full(wo), full(g2), full(wr2), full(br), full(lt)],
        out_specs=out_specs,
        out_shape=out_shape,
        compiler_params=pltpu.CompilerParams(
            dimension_semantics=("arbitrary", "arbitrary"), vmem_limit_bytes=VMEM_LIMIT),
        name="merge",
    )(att, hgr, ga, gh, h, wa, wh, wo, g2, wr2, br, lt), tm


def _slots_kernel(topi_ref, rank_ref, base_ref, slot_ref):
    topi = topi_ref[0]
    rank = rank_ref[0]
    base = base_ref[0]
    lane = _lane_iota(topi.shape)
    slot = jnp.zeros(topi.shape, I32)
    for jx in range(TOP_K):
        mine = lane == topi[:, jx:jx + 1]
        first = jnp.sum(jnp.where(mine, base, 0), axis=1, keepdims=True)
        slot = jnp.where(lane == jx, first + rank[:, jx:jx + 1], slot)
    slot_ref[0] = slot


def _slots(topi, rank, base, tm):
    B, Lp, _ = topi.shape
    nt = Lp // tm
    row = pl.BlockSpec((1, tm, LANES), lambda b, i: (b, i, 0))
    return pl.pallas_call(
        _slots_kernel,
        grid=(B, nt),
        in_specs=[row, row, pl.BlockSpec((1, 1, LANES), lambda b, i: (b * nt + i, 0, 0))],
        out_specs=row,
        out_shape=jax.ShapeDtypeStruct((B, Lp, LANES), I32),
        compiler_params=pltpu.CompilerParams(dimension_semantics=("arbitrary", "arbitrary")),
        name="slots",
    )(topi, rank, base)


def _dispatch_kernel(slot_ref, x_ref, xs_in, xs_hbm, sem, *, tm):
    del xs_in

    def copy(t, jx):
        return pltpu.make_async_copy(x_ref.at[t], xs_hbm.at[slot_ref[0, 0, t * TOP_K + jx]], sem)

    def issue(t, c):
        for jx in range(TOP_K):
            copy(t, jx).start()
        return c

    def drain(t, c):
        for jx in range(TOP_K):
            copy(t, jx).wait()
        return c

    lax.fori_loop(0, tm, issue, 0)
    lax.fori_loop(0, tm, drain, 0)


def _dispatch(slot_s, xrow, cap, tm):
    nt = slot_s.shape[0]
    zeros = jnp.zeros((cap, 8, LANES), F32)
    return pl.pallas_call(
        functools.partial(_dispatch_kernel, tm=tm),
        grid=(nt,),
        in_specs=[pl.BlockSpec((1, 1, tm * TOP_K), lambda i: (i, 0, 0), memory_space=pltpu.SMEM),
                  pl.BlockSpec((tm, 8, LANES), lambda i: (i, 0, 0)),
                  pl.BlockSpec(memory_space=pl.ANY)],
        out_specs=pl.BlockSpec(memory_space=pl.ANY),
        out_shape=jax.ShapeDtypeStruct((cap, 8, LANES), F32),
        scratch_shapes=[pltpu.SemaphoreType.DMA(())],
        input_output_aliases={2: 0},
        compiler_params=pltpu.CompilerParams(dimension_semantics=("arbitrary",)),
        name="dispatch",
    )(slot_s, xrow, zeros)


def _ffn_kernel(be_ref, nu_ref, x_ref, wg_ref, bg_ref, wu_ref, bu_ref, wd_ref, bd_ref, o_ref, *, blk):
    i = pl.program_id(0)
    nk = D_MODEL // LANES

    @pl.when(i < nu_ref[0])
    def _():
        xb = jnp.concatenate([x_ref[pl.ds(k, blk, stride=8), :] for k in range(nk)], axis=1).astype(BF16)
        g = _dot(xb, wg_ref[0]) + bg_ref[0]
        u = _dot(xb, wu_ref[0]) + bu_ref[0]
        g = jnp.minimum(g, SWIGLU_LIMIT)
        u = jnp.clip(u, -SWIGLU_LIMIT, SWIGLU_LIMIT)
        act = (u + 1.0) * (g * jax.nn.sigmoid(SWIGLU_ALPHA * g))
        out = _dot(act.astype(BF16), wd_ref[0]) + bd_ref[0]
        for k in range(nk):
            o_ref[pl.ds(k, blk, stride=8), :] = out[:, k * LANES:(k + 1) * LANES]

    @pl.when(i >= nu_ref[0])
    def _():
        o_ref[...] = jnp.zeros(o_ref.shape, F32)


def _ffn(block_expert, n_used, xs, wg, bg, wu, bu, wd, bd, blk):
    cap = xs.shape[0]
    nblk = cap // blk
    x2 = xs.reshape(cap * 8, LANES)
    wspec = pl.BlockSpec((1, D_MODEL, D_FF), lambda i, be, nu: (be[i], 0, 0))
    wdspec = pl.BlockSpec((1, D_FF, D_MODEL), lambda i, be, nu: (be[i], 0, 0))
    bspec = lambda w: pl.BlockSpec((1, 1, w), lambda i, be, nu: (be[i], 0, 0))
    rows = pl.BlockSpec((blk * 8, LANES), lambda i, be, nu: (i, 0))
    out = pl.pallas_call(
        functools.partial(_ffn_kernel, blk=blk),
        grid_spec=pltpu.PrefetchScalarGridSpec(
            num_scalar_prefetch=2, grid=(nblk,),
            in_specs=[rows, wspec, bspec(D_FF), wspec, bspec(D_FF), wdspec, bspec(D_MODEL)],
            out_specs=rows),
        out_shape=jax.ShapeDtypeStruct((cap * 8, LANES), F32),
        compiler_params=pltpu.CompilerParams(
            dimension_semantics=("arbitrary",), vmem_limit_bytes=VMEM_LIMIT),
        name="ffn",
    )(block_expert, n_used, x2, wg, bg, wu, bu, wd, bd)
    return out.reshape(cap, 8, LANES)


def _combine_kernel(slot_ref, gate_ref, h2_ref, g_ref, ys_hbm, o_ref, buf_ref, sem, *, tm):
    def copy(t, jx):
        dst = buf_ref.at[pl.ds(pl.multiple_of((jx * tm + t) * 8, 8), 8), :]
        return pltpu.make_async_copy(ys_hbm.at[slot_ref[0, 0, t * TOP_K + jx]], dst, sem)

    def issue(t, c):
        for jx in range(TOP_K):
            copy(t, jx).start()
        return c

    def drain(t, c):
        for jx in range(TOP_K):
            copy(t, jx).wait()
        return c

    lax.fori_loop(0, tm, issue, 0)
    lax.fori_loop(0, tm, drain, 0)

    gate = gate_ref[0]
    h2 = h2_ref[0]
    cols = []
    for k in range(D_MODEL // LANES):
        y = h2[:, k * LANES:(k + 1) * LANES]
        for jx in range(TOP_K):
            y = y + gate[:, jx:jx + 1] * buf_ref[pl.ds(jx * tm * 8 + k, tm, stride=8), :]
        cols.append(y)
    ho = jnp.concatenate(cols, axis=1)
    o_ref[0] = ho * lax.rsqrt(jnp.mean(ho * ho, axis=-1, keepdims=True) + EPS) * g_ref[...]


def _combine(slot_s, gate, h2, g, ys, tm):
    B, Lp, D = h2.shape
    nt = Lp // tm
    row = lambda w: pl.BlockSpec((1, tm, w), lambda b, i: (b, i, 0))
    return pl.pallas_call(
        functools.partial(_combine_kernel, tm=tm),
        grid=(B, nt),
        in_specs=[pl.BlockSpec((1, 1, tm * TOP_K), lambda b, i: (b * nt + i, 0, 0), memory_space=pltpu.SMEM),
                  row(LANES), row(D),
                  pl.BlockSpec((1, D), lambda b, i: (0, 0)),
                  pl.BlockSpec(memory_space=pl.ANY)],
        out_specs=row(D),
        out_shape=jax.ShapeDtypeStruct((B, Lp, D), F32),
        scratch_shapes=[pltpu.VMEM((TOP_K * tm * 8, LANES), F32), pltpu.SemaphoreType.DMA(())],
        compiler_params=pltpu.CompilerParams(
            dimension_semantics=("arbitrary", "arbitrary"), vmem_limit_bytes=VMEM_LIMIT),
        name="combine",
    )(slot_s, gate, h2, g.reshape(1, D), ys)


def _moe(h2, xrow, topi, gate, rank, cnt, tm, ffn_w, final_g):
    B, Lp, D = h2.shape
    T = B * Lp
    nt_total = cnt.shape[0]
    blk = FFN_BLOCK
    nblk = -(-(T * TOP_K) // blk) + N_EXPERTS
    cap = nblk * blk
    cnt2 = cnt.reshape(nt_total, LANES)
    totals = jnp.sum(cnt2, axis=0)
    padded = (totals + blk - 1) // blk * blk
    pad_ends = jnp.cumsum(padded)
    pad_starts = pad_ends - padded
    base = (pad_starts[None, :] + jnp.cumsum(cnt2, axis=0) - cnt2).astype(I32).reshape(nt_total, 1, LANES)
    block_start = (jnp.arange(nblk) * blk)[:, None]
    block_expert = jnp.minimum(
        jnp.sum(pad_ends[None, :N_EXPERTS] <= block_start, axis=1), N_EXPERTS - 1).astype(I32)
    n_used = (pad_ends[N_EXPERTS - 1] // blk).astype(I32).reshape(1)
    slot = _slots(topi, rank, base, tm)
    slot_s = slot[:, :, :TOP_K].reshape(nt_total, 1, tm * TOP_K)

    xs = _dispatch(slot_s, xrow.reshape(T, 8, LANES), cap, tm)
    ys = _ffn(block_expert, n_used, xs, *ffn_w, blk)
    return _combine(slot_s, gate, h2, final_g, ys, tm)


def kernel(x, meta_tokens, attn_norm_g, w_in, hgrn_norm_g, w_up_attn, w_up_hgrn, w_out,
           hgrn_lb_logits, ffn_norm_g, router_w, router_b, w_gate, b_gate, w_up, b_up,
           w_down, b_down, final_norm_g):
    B, S, D = x.shape
    L = S + N_META
    Lp = -(-L // Q_BLOCK) * Q_BLOCK
    k_sel = min(TOPK_MAX, L // 4)
    meta = jnp.broadcast_to(meta_tokens[None].astype(x.dtype), (B, N_META, D))
    h = jnp.concatenate([meta, x, jnp.zeros((B, Lp - L, D), x.dtype)], axis=1)
    lb = jnp.cumsum(jax.nn.softmax(hgrn_lb_logits.astype(F32), axis=0), axis=0)[0]

    (q, kx, vx, iq, ikf, ikb, hq, hf, hi, hg, ga, gh) = _inproj(h, attn_norm_g[0], _arrange_w_in(w_in[0]))
    att = _attn(q, iq, ikf, ikb, kx, vx, k_sel)
    hgr = _hgrn(hq, hf, hi, hg, lb, hgrn_norm_g[0])

    wr = jnp.pad(router_w[0], ((0, 0), (0, LANES - N_EXPERTS)))
    wr_hi = wr.astype(BF16)
    wr_lo = (wr - wr_hi.astype(F32)).astype(BF16)
    br = jnp.pad(router_b[0], (0, LANES - N_EXPERTS), constant_values=-1e30).reshape(1, LANES)
    (h2, xrow, topi, gate, rank, cnt), tm = _merge(
        att, hgr, ga, gh, h, w_up_attn[0].astype(BF16), w_up_hgrn[0].astype(BF16), w_out[0].astype(BF16),
        ffn_norm_g[0], jnp.concatenate([wr_hi, wr_lo], axis=1), br)

    ffn_w = (w_gate[0].astype(BF16), b_gate[0].reshape(N_EXPERTS, 1, D_FF),
             w_up[0].astype(BF16), b_up[0].reshape(N_EXPERTS, 1, D_FF),
             w_down[0].astype(BF16), b_down[0].reshape(N_EXPERTS, 1, D_MODEL))
    out = _moe(h2, xrow, topi, gate, rank, cnt, tm, ffn_w, final_norm_g)
    return out[:, N_META:L]
```

```python
import functools

import numpy as np
import jax
import jax.numpy as jnp
from jax import lax
from jax.experimental import pallas as pl
from jax.experimental.pallas import tpu as pltpu

F32 = jnp.float32
BF16 = jnp.bfloat16
I32 = jnp.int32

D_MODEL = 1024
N_META = 16
EPS = 1e-5
ROPE_THETA = 500000.0
ATT_HEADS = 8
ATT_KV_HEADS = 2
ATT_HEAD_DIM = 64
ATT_WIDTH = ATT_HEADS * ATT_HEAD_DIM
IDX_HEADS = 8
IDX_DIM = 32
TOPK_MAX = 256
Q_BLOCK = 128
HG_HEADS = 4
HG_DK = 128
HG_DV = 128
HG_WIDTH = HG_HEADS * HG_DV
N_EXPERTS = 32
TOP_K = 4
D_FF = 1024
SWIGLU_LIMIT = 7.0
SWIGLU_ALPHA = 1.702

LANES = 128
VMEM_LIMIT = 56 * 1024 * 1024
INT_MIN = -2 ** 31
NEG_INF = float("-inf")
LOG2E = 1.4426950408889634

HG_CHUNK = 128
FFN_BLOCK = 256


def _pick_tile(n, candidates):
    for c in candidates:
        if n % c == 0:
            return c
    raise ValueError(f"no tile for {n}")


def _dot(a, b):
    return jnp.dot(a, b, preferred_element_type=F32)


def _dot_nt(a, b):
    return lax.dot_general(a, b, (((1,), (1,)), ((), ())), preferred_element_type=F32)


def _dot_tn(a, b):
    return lax.dot_general(a, b, (((0,), (0,)), ((), ())), preferred_element_type=F32)


def _lane_iota(shape):
    return lax.broadcasted_iota(I32, shape, len(shape) - 1)


def _rope_tables(n_pos, head_dim, lane_valid):
    rot = head_dim // 4
    half = rot // 2
    inv = ROPE_THETA ** (-jnp.arange(half, dtype=F32) * 2.0 / rot)
    ang = jnp.arange(n_pos).astype(F32)[:, None] * inv[None, :]
    c = jnp.cos(ang)
    s = jnp.sin(ang)
    lane = np.arange(LANES)
    within = lane % head_dim
    valid = lane < lane_valid
    first = (within < half) & valid
    second = (within >= half) & (within < rot) & valid
    idx = np.where(within < half, within, within - half) % half
    cc = c[:, idx]
    ss = s[:, idx]
    tc = jnp.where((first | second)[None, :], cc, 1.0)
    t1 = jnp.where(first[None, :], -ss, 0.0)
    t2 = jnp.where(second[None, :], ss, 0.0)
    return jnp.stack([tc, t1, t2]).astype(F32), half


def _rope_apply(x, tab_ref, half):
    c = tab_ref[0]
    s1 = tab_ref[1]
    s2 = tab_ref[2]
    cols = []
    for j in range(x.shape[1] // LANES):
        xc = x[:, j * LANES:(j + 1) * LANES]
        cols.append(xc * c + pltpu.roll(xc, LANES - half, 1) * s1 + pltpu.roll(xc, half, 1) * s2)
    return cols[0] if len(cols) == 1 else jnp.concatenate(cols, axis=1)


_W_Q = (0, 512)
_W_KX = (512, 768)
_W_VX = (768, 1024)
_W_IQ = (1024, 1280)
_W_IK = (1280, 1408)
_W_HQ = (1408, 1920)
_W_HF = (1920, 2432)
_W_HI = (2432, 2944)
_W_HG = (2944, 3456)
_W_GA = (3456, 4480)
_W_GH = (4480, 5504)
_W_COLS = 5504


def _arrange_w_in(w_in):
    o = np.cumsum([0, 512, 128, 128, 256, 32, 8, 512, 512, 512, 512, 1024, 1024])
    seg = [w_in[:, o[i]:o[i + 1]] for i in range(12)]
    q, k, v, iq, ik, iw, hq, hf, hi, hg, ga, gh = seg
    z64 = jnp.zeros((D_MODEL, 64), w_in.dtype)
    z88 = jnp.zeros((D_MODEL, 88), w_in.dtype)
    kx = jnp.concatenate([k[:, :64], z64, k[:, 64:], z64], axis=1)
    vx = jnp.concatenate([v[:, :64], z64, v[:, 64:], z64], axis=1)
    ikx = jnp.concatenate([ik, iw, z88], axis=1)
    w = jnp.concatenate([q, kx, vx, iq, ikx, hq, hf, hi, hg, ga, gh], axis=1)
    assert w.shape[1] == _W_COLS
    return w.astype(BF16)


def _inproj_kernel(h_ref, g_ref, w_ref, tatt_ref, tkx_ref, tiq_ref, tik_ref,
                   q_ref, kx_ref, vx_ref, iq_ref, ikf_ref, ikb_ref,
                   hq_ref, hf_ref, hi_ref, hg_ref, ga_ref, gh_ref, *, half_att, half_idx):
    x = h_ref[0]
    xn = x * lax.rsqrt(jnp.mean(x * x, axis=-1, keepdims=True) + EPS) * g_ref[...]
    xb = xn.astype(BF16)

    def proj(rng):
        return _dot(xb, w_ref[:, rng[0]:rng[1]])

    q_ref[0] = _rope_apply(proj(_W_Q), tatt_ref, half_att)
    kx = _rope_apply(proj(_W_KX), tkx_ref, half_att).astype(BF16)
    kx_ref[0, 0] = kx[:, :LANES]
    kx_ref[0, 1] = kx[:, LANES:]
    vx = proj(_W_VX)
    ones_hi = jnp.where(_lane_iota(vx.shape) % LANES >= ATT_HEAD_DIM, 1.0, 0.0)
    vx = (vx + ones_hi).astype(BF16)
    vx_ref[0, 0] = vx[:, :LANES]
    vx_ref[0, 1] = vx[:, LANES:]
    iq_ref[0] = _rope_apply(proj(_W_IQ), tiq_ref, half_idx)
    ik = _rope_apply(proj(_W_IK), tik_ref, half_idx)
    ikf_ref[0] = ik
    ikb_ref[0] = ik.astype(BF16)
    hq_ref[0] = proj(_W_HQ)
    hf_ref[0] = proj(_W_HF)
    hi_ref[0] = proj(_W_HI)
    hg_ref[0] = proj(_W_HG)
    ga_ref[0] = proj(_W_GA)
    gh_ref[0] = proj(_W_GH)


def _inproj(h, g, w_arr):
    B, Lp, D = h.shape
    tm = _pick_tile(Lp, (320, 256, 128))
    tatt, half_att = _rope_tables(Lp, ATT_HEAD_DIM, LANES)
    tkx, _ = _rope_tables(Lp, ATT_HEAD_DIM, ATT_HEAD_DIM)
    tiq, half_idx = _rope_tables(Lp, IDX_DIM, LANES)
    tik, _ = _rope_tables(Lp, IDX_DIM, IDX_DIM)

    def rows(width, dtype=F32):
        return (jax.ShapeDtypeStruct((B, Lp, width), dtype),
                pl.BlockSpec((1, tm, width), lambda b, i: (b, i, 0)))

    def kv(dtype=BF16):
        return (jax.ShapeDtypeStruct((B, ATT_KV_HEADS, Lp, LANES), dtype),
                pl.BlockSpec((1, ATT_KV_HEADS, tm, LANES), lambda b, i: (b, 0, i, 0)))

    outs = [rows(512), kv(), kv(), rows(256), rows(128), rows(128, BF16),
            rows(512), rows(512), rows(512), rows(512), rows(1024), rows(1024)]
    tab_spec = pl.BlockSpec((3, tm, LANES), lambda b, i: (0, i, 0))
    return pl.pallas_call(
        functools.partial(_inproj_kernel, half_att=half_att, half_idx=half_idx),
        grid=(B, Lp // tm),
        in_specs=[pl.BlockSpec((1, tm, D), lambda b, i: (b, i, 0)),
                  pl.BlockSpec((1, D), lambda b, i: (0, 0)),
                  pl.BlockSpec((D, _W_COLS), lambda b, i: (0, 0)),
                  tab_spec, tab_spec, tab_spec, tab_spec],
        out_specs=[o[1] for o in outs],
        out_shape=[o[0] for o in outs],
        compiler_params=pltpu.CompilerParams(
            dimension_semantics=("arbitrary", "arbitrary"), vmem_limit_bytes=VMEM_LIMIT),
        name="inproj",
    )(h, g.reshape(1, D), w_arr, tatt, tkx, tiq, tik)


def _attn_kernel(q_ref, iq_ref, ikf_ref, ikb_ref, kx_ref, vt_ref, ltri_ref, o_ref,
                 keys_ref, acc_ref, m_ref, bias_ref, lg_ref, p_ref, *, C, k_sel):
    j = pl.program_id(1)
    nck = ((j + 1) * Q_BLOCK + C - 1) // C
    lane = _lane_iota((Q_BLOCK, LANES))
    qpos = j * Q_BLOCK + lax.broadcasted_iota(I32, (1, Q_BLOCK), 1)
    nsub = C // 8
    nacc = 8

    def pairs(heads):
        return [jnp.concatenate([heads[2 * p], heads[2 * p + 1]], axis=0) for p in range(len(heads) // 2)]

    iqf = iq_ref[0]
    per_blk = LANES // IDX_DIM
    iqh = []
    for h in range(IDX_HEADS):
        blk = iqf[:, (h // per_blk) * LANES:(h // per_blk + 1) * LANES]
        off = (h % per_blk) * IDX_DIM
        if off:
            blk = pltpu.roll(blk, LANES - off, 1)
        iqh.append(jnp.where(lane < IDX_DIM, blk, 0.0).astype(BF16))
    iqpair = pairs(iqh)
    wt = ikf_ref[0].T
    wrow = [wt[IDX_DIM + h:IDX_DIM + h + 1, :] for h in range(IDX_HEADS)]

    def causal_mask(c):
        kpos = c * C + lax.broadcasted_iota(I32, (C, 1), 0)
        return kpos <= qpos

    def phase_a(c, carry):
        ks = ikb_ref[0, pl.ds(pl.multiple_of(c * C, C), C), :]
        s = jnp.zeros((C, Q_BLOCK), F32)
        for p in range(IDX_HEADS // 2):
            d2 = _dot_nt(ks, iqpair[p])
            s = (s + jnp.maximum(d2[:, :LANES], 0.0) * wrow[2 * p]
                 + jnp.maximum(d2[:, LANES:], 0.0) * wrow[2 * p + 1])
        bits = pltpu.bitcast(s, I32)
        key = jnp.where(bits < 0, bits ^ jnp.int32(0x7FFFFFFF), bits)
        keys_ref[c] = jnp.where(causal_mask(c), key, jnp.int32(INT_MIN))
        return carry

    lax.fori_loop(0, nck, phase_a, 0)

    def count(cand, strict):
        def body(c, acc):
            kc = keys_ref[c]
            hit = (kc > cand) if strict else (kc >= cand)
            return acc + jnp.sum(jnp.where(hit, 1, 0).reshape(nacc, nsub // nacc, 8, LANES), axis=1)

        acc = lax.fori_loop(0, nck, body, jnp.zeros((nacc, 8, LANES), I32))
        return jnp.sum(jnp.sum(acc, axis=0), axis=0, keepdims=True)

    keeps_all = qpos < k_sel

    def unsettled(nge):
        return jnp.max(jnp.where((nge == k_sel) | keeps_all, 0, 1)) > 0

    def search_more(carry):
        i, _, nge = carry
        return (i < 32) & unsettled(nge)

    def search(carry):
        i, prefix, nge = carry
        bit = lax.shift_left(jnp.int32(1), jnp.int32(31) - i)
        cand = prefix ^ bit
        cnt = count(cand, False)
        ok = cnt >= k_sel
        return i + 1, jnp.where(ok, cand, prefix), jnp.where(ok, cnt, nge)

    _, thr, nge = lax.while_loop(
        search_more, search,
        (jnp.int32(0), jnp.full((1, Q_BLOCK), INT_MIN, I32), jnp.zeros((1, Q_BLOCK), I32)))
    excess = jnp.where((nge > k_sel) & (thr != INT_MIN), 1, 0)
    any_excess = jnp.max(excess) > 0
    need = lax.cond(any_excess,
                    lambda: (k_sel - count(thr, True)).astype(F32),
                    lambda: jnp.zeros((1, Q_BLOCK), F32))
    thr_lo = jnp.maximum(thr, jnp.int32(INT_MIN + 1))

    qf = q_ref[0]
    qh = []
    for h in range(ATT_HEADS):
        blk = qf[:, (h // 2) * LANES:(h // 2 + 1) * LANES]
        if h % 2:
            blk = pltpu.roll(blk, ATT_HEAD_DIM, 1)
        qh.append((jnp.where(lane < ATT_HEAD_DIM, blk, 0.0) * (LOG2E * ATT_HEAD_DIM ** -0.5)).astype(BF16))
    qpair = pairs(qh)
    kv_of_pair = [(2 * p) // (ATT_HEADS // ATT_KV_HEADS) for p in range(ATT_HEADS // 2)]

    acc_ref[...] = jnp.zeros(acc_ref.shape, F32)
    m_ref[...] = jnp.full(m_ref.shape, NEG_INF, F32)

    def phase_c(c, eq_seen):
        kc = keys_ref[c]

        def bias_simple():
            return jnp.where(kc >= thr_lo, 0.0, NEG_INF), eq_seen

        def bias_ties():
            eq = (kc == thr) & causal_mask(c)
            eqf = jnp.where(eq, 1.0, 0.0)
            rank = eq_seen + _dot(ltri_ref[...], eqf.astype(BF16))
            take = (kc > thr) | (eq & (rank < need))
            return jnp.where(take, 0.0, NEG_INF), eq_seen + jnp.sum(eqf, axis=0, keepdims=True)

        bias, eq_next = lax.cond(any_excess, bias_ties, bias_simple)
        bias_ref[...] = bias
        start = pl.multiple_of(c * C, C)
        for p in range(ATT_HEADS // 2):
            lg2 = _dot_nt(kx_ref[0, kv_of_pair[p], pl.ds(start, C), :], qpair[p])
            lg_ref[2 * p] = lg2[:, :LANES] + bias_ref[...]
            lg_ref[2 * p + 1] = lg2[:, LANES:] + bias_ref[...]
        shifts, alphas = [], []
        for h in range(ATT_HEADS):
            m_old = m_ref[h]
            mc = jnp.max(lg_ref[h].reshape(nsub, 8, LANES), axis=0)
            m_new = jnp.maximum(m_old, jnp.max(mc, axis=0, keepdims=True))
            m_ref[h] = m_new
            m_safe = jnp.where(m_new == NEG_INF, 0.0, m_new)
            shifts.append(m_safe[:1])
            alphas.append(jnp.exp2(m_old - m_safe)[:1])
        for h in range(ATT_HEADS):
            p_ref[h // 2, :, (h % 2) * LANES:(h % 2 + 1) * LANES] = (
                jnp.exp2(lg_ref[h] - shifts[h]).astype(BF16))
        for p in range(ATT_HEADS // 2):
            o2 = _dot(vt_ref[0, kv_of_pair[p], c], p_ref[p])
            acc_ref[2 * p] = alphas[2 * p] * acc_ref[2 * p] + o2[:, :LANES]
            acc_ref[2 * p + 1] = alphas[2 * p + 1] * acc_ref[2 * p + 1] + o2[:, LANES:]
        return eq_next

    lax.fori_loop(0, nck, phase_c, jnp.zeros((1, Q_BLOCK), F32))

    for p in range(ATT_HEADS // 2):
        a = acc_ref[2 * p]
        b = acc_ref[2 * p + 1]
        ot = jnp.concatenate([a[:ATT_HEAD_DIM] / a[ATT_HEAD_DIM:ATT_HEAD_DIM + 1],
                              b[:ATT_HEAD_DIM] / b[ATT_HEAD_DIM:ATT_HEAD_DIM + 1]], axis=0)
        o_ref[0, :, p * LANES:(p + 1) * LANES] = ot.T


def _attn(q, iq, ikf, ikb, kx, vx, k_sel):
    B, Lp, _ = q.shape
    C = _pick_tile(Lp, (640, 512, 256, 128))
    nch = Lp // C
    ltri = jnp.asarray(np.tril(np.ones((C, C), np.float32), -1), BF16)
    vt = vx.reshape(B, ATT_KV_HEADS, nch, C, LANES).transpose(0, 1, 2, 4, 3)
    qspec = lambda w: pl.BlockSpec((1, Q_BLOCK, w), lambda b, j: (b, j, 0))
    return pl.pallas_call(
        functools.partial(_attn_kernel, C=C, k_sel=k_sel),
        grid=(B, Lp // Q_BLOCK),
        in_specs=[qspec(512), qspec(256), qspec(128),
                  pl.BlockSpec((1, Lp, LANES), lambda b, j: (b, 0, 0)),
                  pl.BlockSpec((1, ATT_KV_HEADS, Lp, LANES), lambda b, j: (b, 0, 0, 0)),
                  pl.BlockSpec((1, ATT_KV_HEADS, nch, LANES, C), lambda b, j: (b, 0, 0, 0, 0)),
                  pl.BlockSpec((C, C), lambda b, j: (0, 0))],
        out_specs=qspec(512),
        out_shape=jax.ShapeDtypeStruct((B, Lp, ATT_WIDTH), F32),
        scratch_shapes=[pltpu.VMEM((nch, C, Q_BLOCK), I32),
                        pltpu.VMEM((ATT_HEADS, LANES, Q_BLOCK), F32),
                        pltpu.VMEM((ATT_HEADS, 8, Q_BLOCK), F32),
                        pltpu.VMEM((C, Q_BLOCK), F32),
                        pltpu.VMEM((ATT_HEADS, C, Q_BLOCK), F32),
                        pltpu.VMEM((ATT_HEADS // 2, C, 2 * Q_BLOCK), BF16)],
        compiler_params=pltpu.CompilerParams(
            dimension_semantics=("arbitrary", "arbitrary"), vmem_limit_bytes=VMEM_LIMIT),
        name="attn",
    )(q, iq, ikf, ikb, kx, vt, ltri)


def _hgrn_sum_matrix(C):
    nlev = int(np.log2(C))
    t = np.arange(C)[:, None]
    jj = np.arange(C)[None, :]
    mats = [(jj <= t), (jj > t)]
    for l in range(nlev):
        s = 1 << l
        mid = ((t >> (l + 1)) << (l + 1)) + s - 1
        odd = ((t >> l) & 1) == 1
        mats.append(np.where(odd, (jj > mid) & (jj <= t), (jj > t) & (jj <= mid)))
    return np.concatenate(mats, axis=0).astype(np.float32), nlev


def _hgrn_kernel(hq_ref, hf_ref, hi_ref, hg_ref, lb_ref, gn_ref, e_ref, o_ref, st_ref, *, C, nlev):
    @pl.when(pl.program_id(1) == 0)
    def _():
        st_ref[...] = jnp.zeros(st_ref.shape, F32)

    for h in range(HG_HEADS):
        _hgrn_head(hq_ref, hf_ref, hi_ref, hg_ref, lb_ref, gn_ref, e_ref, o_ref, st_ref, h, C, nlev)


def _hgrn_head(hq_ref, hf_ref, hi_ref, hg_ref, lb_ref, gn_ref, e_ref, o_ref, st_ref, h, C, nlev):
    cols = slice(h * HG_DK, (h + 1) * HG_DK)
    lb = lb_ref[:, cols]
    f = lb + (1.0 - lb) * jax.nn.sigmoid(hf_ref[0, :, cols])
    logf = jnp.log(f)
    kk = 1.0 - f
    q = hq_ref[0, :, cols] * (HG_DK ** -0.5)
    v = hi_ref[0, :, cols].astype(BF16)

    p1 = logf.astype(BF16)
    r1 = logf - p1.astype(F32)
    p2 = r1.astype(BF16)
    p3 = (r1 - p2.astype(F32)).astype(BF16)
    x3 = _dot(e_ref[...], jnp.concatenate([p1, p2, p3], axis=1))
    xs = x3[:, :LANES] + x3[:, LANES:2 * LANES] + x3[:, 2 * LANES:]

    b = xs[0:C]
    suffix = xs[C:2 * C]
    rt = lax.broadcasted_iota(I32, (C, C), 0)
    ct = lax.broadcasted_iota(I32, (C, C), 1)
    row = lax.broadcasted_iota(I32, (C, 1), 0)

    scores = jnp.where(rt == ct, _dot_nt(q.astype(BF16), kk.astype(BF16)), 0.0)
    for l in range(nlev):
        w = jnp.exp(xs[(2 + l) * C:(3 + l) * C])
        odd = ((row >> l) & 1) == 1
        a = (jnp.where(odd, q, kk) * w).astype(BF16)
        pair = (((rt >> l) & 1) == 1) & ((ct >> l) == (rt >> l) - 1)
        scores = scores + jnp.where(pair, _dot_nt(a, a), 0.0)

    st = st_ref[h]
    qb = (q * jnp.exp(b)).astype(BF16)
    o = _dot(scores.astype(BF16), v) + _dot_nt(qb, st.astype(BF16))
    ks = (kk * jnp.exp(suffix)).astype(BF16)
    st_ref[h] = jnp.exp(b[C - 1:C, :]) * st + _dot_tn(v, ks)

    o = o * lax.rsqrt(jnp.mean(o * o, axis=-1, keepdims=True) + EPS) * gn_ref[:, cols]
    hg = hg_ref[0, :, cols]
    o_ref[0, :, cols] = o * (hg * jax.nn.sigmoid(hg))


def _hgrn(hq, hf, hi, hg, lb, gn):
    B, Lp, _ = hq.shape
    C = HG_CHUNK
    e_np, nlev = _hgrn_sum_matrix(C)
    e = jnp.asarray(e_np, BF16)
    blk = pl.BlockSpec((1, C, HG_WIDTH), lambda b, c: (b, c, 0))
    vec = pl.BlockSpec((1, HG_WIDTH), lambda b, c: (0, 0))
    return pl.pallas_call(
        functools.partial(_hgrn_kernel, C=C, nlev=nlev),
        grid=(B, Lp // C),
        in_specs=[blk, blk, blk, blk, vec, vec,
                  pl.BlockSpec(e.shape, lambda b, c: (0, 0))],
        out_specs=blk,
        out_shape=jax.ShapeDtypeStruct((B, Lp, HG_WIDTH), F32),
        scratch_shapes=[pltpu.VMEM((HG_HEADS, HG_DV, HG_DK), F32)],
        compiler_params=pltpu.CompilerParams(
            dimension_semantics=("arbitrary", "arbitrary"), vmem_limit_bytes=VMEM_LIMIT),
        name="hgrn",
    )(hq, hf, hi, hg, lb.reshape(1, -1), gn.reshape(1, -1), e)


def _merge_kernel(att_ref, hgr_ref, ga_ref, gh_ref, h_ref, wa_ref, wh_ref, wo_ref, g_ref,
                  wr_ref, br_ref, lt_ref,
                  h2_ref, xrow_ref, topi_ref, gate_ref, rank_ref, cnt_ref, *, tm):
    mix = (jax.nn.sigmoid(ga_ref[0]) * _dot(att_ref[0].astype(BF16), wa_ref[...])
           + jax.nn.sigmoid(gh_ref[0]) * _dot(hgr_ref[0].astype(BF16), wh_ref[...]))
    h2 = h_ref[0] + _dot(mix.astype(BF16), wo_ref[...])
    h2_ref[0] = h2
    xn = h2 * lax.rsqrt(jnp.mean(h2 * h2, axis=-1, keepdims=True) + EPS) * g_ref[...]
    for k in range(D_MODEL // LANES):
        xrow_ref[pl.ds(k, tm, stride=8), :] = xn[:, k * LANES:(k + 1) * LANES]

    xh = xn.astype(BF16)
    xl = (xn - xh.astype(F32)).astype(BF16)
    l1 = _dot(xh, wr_ref[...])
    logits = l1[:, :LANES] + l1[:, LANES:] + _dot(xl, wr_ref[:, :LANES]) + br_ref[...]

    lane = _lane_iota((tm, LANES))
    lanef = lane.astype(F32)
    work = logits
    vals, sels, idxs = [], [], []
    for _ in range(TOP_K):
        m = jnp.max(work, axis=1, keepdims=True)
        idx = jnp.min(jnp.where(work == m, lanef, float(LANES)), axis=1, keepdims=True)
        sel = lanef == idx
        vals.append(m)
        idxs.append(idx)
        sels.append(sel)
        work = jnp.where(sel, NEG_INF, work)
    es = [jnp.exp(v - vals[0]) for v in vals]
    den = es[0] + es[1] + es[2] + es[3]

    member = jnp.zeros((tm, LANES), F32)
    for sel in sels:
        member = member + jnp.where(sel, 1.0, 0.0)
    before = _dot(lt_ref[...], member.astype(BF16))
    topi = jnp.zeros((tm, LANES), F32)
    gate = jnp.zeros((tm, LANES), F32)
    rank = jnp.zeros((tm, LANES), F32)
    for jx in range(TOP_K):
        here = lane == jx
        topi = jnp.where(here, idxs[jx], topi)
        gate = jnp.where(here, es[jx] / den, gate)
        rank = jnp.where(here, jnp.sum(jnp.where(sels[jx], before, 0.0), axis=1, keepdims=True), rank)
    topi_ref[0] = topi.astype(I32)
    gate_ref[0] = gate
    rank_ref[0] = rank.astype(I32)
    cnt_ref[0] = jnp.sum(member, axis=0, keepdims=True).astype(I32)


def _merge(att, hgr, ga, gh, h, wa, wh, wo, g, wr2, br):
    B, Lp, D = h.shape
    tm = _pick_tile(Lp, (320, 256, 128))
    nt = Lp // tm
    lt = jnp.asarray(np.tril(np.ones((tm, tm), np.float32), -1), BF16)
    row = lambda w: pl.BlockSpec((1, tm, w), lambda b, i: (b, i, 0))
    full = lambda a: pl.BlockSpec(a.shape, lambda b, i: (0,) * a.ndim)
    g2 = g.reshape(1, D)
    out_shape = [jax.ShapeDtypeStruct((B, Lp, D), F32),
                 jax.ShapeDtypeStruct((B * Lp * 8, LANES), F32),
                 jax.ShapeDtypeStruct((B, Lp, LANES), I32),
                 jax.ShapeDtypeStruct((B, Lp, LANES), F32),
                 jax.ShapeDtypeStruct((B, Lp, LANES), I32),
                 jax.ShapeDtypeStruct((B * nt, 1, LANES), I32)]
    out_specs = [row(D),
                 pl.BlockSpec((tm * 8, LANES), lambda b, i: (b * nt + i, 0)),
                 row(LANES), row(LANES), row(LANES),
                 pl.BlockSpec((1, 1, LANES), lambda b, i: (b * nt + i, 0, 0))]
    return pl.pallas_call(
        functools.partial(_merge_kernel, tm=tm),
        grid=(B, nt),
        in_specs=[row(512), row(512), row(D), row(D), row(D),
                  full(wa), full(wh), full(wo), full(g2), full(wr2), full(br), full(lt)],
        out_specs=out_specs,
        out_shape=out_shape,
        compiler_params=pltpu.CompilerParams(
            dimension_semantics=("arbitrary", "arbitrary"), vmem_limit_bytes=VMEM_LIMIT),
        name="merge",
    )(att, hgr, ga, gh, h, wa, wh, wo, g2, wr2, br, lt), tm


def _slots_kernel(topi_ref, rank_ref, base_ref, slot_ref):
    topi = topi_ref[0]
    rank = rank_ref[0]
    base = base_ref[0]
    lane = _lane_iota(topi.shape)
    slot = jnp.zeros(topi.shape, I32)
    for jx in range(TOP_K):
        mine = lane == topi[:, jx:jx + 1]
        first = jnp.sum(jnp.where(mine, base, 0), axis=1, keepdims=True)
        slot = jnp.where(lane == jx, first + rank[:, jx:jx + 1], slot)
    slot_ref[0] = slot


def _slots(topi, rank, base, tm):
    B, Lp, _ = topi.shape
    nt = Lp // tm
    row = pl.BlockSpec((1, tm, LANES), lambda b, i: (b, i, 0))
    return pl.pallas_call(
        _slots_kernel,
        grid=(B, nt),
        in_specs=[row, row, pl.BlockSpec((1, 1, LANES), lambda b, i: (b * nt + i, 0, 0))],
        out_specs=row,
        out_shape=jax.ShapeDtypeStruct((B, Lp, LANES), I32),
        compiler_params=pltpu.CompilerParams(dimension_semantics=("arbitrary", "arbitrary")),
        name="slots",
    )(topi, rank, base)


def _dispatch_kernel(pad_lo_ref, pad_n_ref, nu_ref, slot_ref, x_ref, xs_hbm, zblk_ref, sem, *, tm, blk, nblk):
    @pl.when(pl.program_id(0) == 0)
    def _():
        zblk_ref[...] = jnp.zeros(zblk_ref.shape, F32)

        def zblock(b):
            return pltpu.make_async_copy(zblk_ref, xs_hbm.at[pl.ds(b * blk, blk)], sem)

        def bissue(b, c):
            zblock(b).start()
            return c

        def bdrain(b, c):
            zblock(b).wait()
            return c

        lax.fori_loop(nu_ref[0], nblk, bissue, 0)
        lax.fori_loop(nu_ref[0], nblk, bdrain, 0)

        def per_expert(e, c):
            def zcopy(r):
                return pltpu.make_async_copy(zblk_ref.at[0], xs_hbm.at[pad_lo_ref[e] + r], sem)

            def zissue(r, c2):
                zcopy(r).start()
                return c2

            def zdrain(r, c2):
                zcopy(r).wait()
                return c2

            lax.fori_loop(0, pad_n_ref[e], zissue, 0)
            lax.fori_loop(0, pad_n_ref[e], zdrain, 0)
            return c

        lax.fori_loop(0, N_EXPERTS, per_expert, 0)

    def copy(t, jx):
        return pltpu.make_async_copy(x_ref.at[t], xs_hbm.at[slot_ref[0, 0, t * TOP_K + jx]], sem)

    def issue(t, c):
        for jx in range(TOP_K):
            copy(t, jx).start()
        return c

    def drain(t, c):
        for jx in range(TOP_K):
            copy(t, jx).wait()
        return c

    lax.fori_loop(0, tm, issue, 0)
    lax.fori_loop(0, tm, drain, 0)


def _dispatch(pad_lo, pad_n, n_used, slot_s, xrow, cap, tm, blk):
    nt = slot_s.shape[0]
    return pl.pallas_call(
        functools.partial(_dispatch_kernel, tm=tm, blk=blk, nblk=cap // blk),
        grid_spec=pltpu.PrefetchScalarGridSpec(
            num_scalar_prefetch=3, grid=(nt,),
            in_specs=[pl.BlockSpec((1, 1, tm * TOP_K), lambda i, lo, n, nu: (i, 0, 0), memory_space=pltpu.SMEM),
                      pl.BlockSpec((tm, 8, LANES), lambda i, lo, n, nu: (i, 0, 0))],
            out_specs=pl.BlockSpec(memory_space=pl.ANY),
            scratch_shapes=[pltpu.VMEM((blk, 8, LANES), F32), pltpu.SemaphoreType.DMA(())]),
        out_shape=jax.ShapeDtypeStruct((cap, 8, LANES), F32),
        compiler_params=pltpu.CompilerParams(dimension_semantics=("arbitrary",)),
        name="dispatch",
    )(pad_lo, pad_n, n_used, slot_s, xrow)


def _ffn_kernel(be_ref, nu_ref, x_ref, wg_ref, bg_ref, wu_ref, bu_ref, wd_ref, bd_ref, o_ref,
                wgb_ref, wub_ref, wdb_ref, *, blk):
    i = pl.program_id(0)
    nk = D_MODEL // LANES
    used = i < nu_ref[0]
    new_expert = (i == 0) | (be_ref[i] != be_ref[jnp.maximum(i - 1, 0)])

    @pl.when(used & new_expert)
    def _():
        rows = 128

        def cast(r, c):
            sl = pl.ds(pl.multiple_of(r * rows, rows), rows)
            wgb_ref[sl, :] = wg_ref[0, sl, :].astype(BF16)
            wub_ref[sl, :] = wu_ref[0, sl, :].astype(BF16)
            wdb_ref[sl, :] = wd_ref[0, sl, :].astype(BF16)
            return c

        lax.fori_loop(0, D_MODEL // rows, cast, 0)

    @pl.when(used)
    def _():
        xb = jnp.concatenate([x_ref[pl.ds(k, blk, stride=8), :] for k in range(nk)], axis=1).astype(BF16)
        g = _dot(xb, wgb_ref[...]) + bg_ref[0]
        u = _dot(xb, wub_ref[...]) + bu_ref[0]
        g = jnp.minimum(g, SWIGLU_LIMIT)
        u = jnp.clip(u, -SWIGLU_LIMIT, SWIGLU_LIMIT)
        act = (u + 1.0) * (g * jax.nn.sigmoid(SWIGLU_ALPHA * g))
        out = _dot(act.astype(BF16), wdb_ref[...]) + bd_ref[0]
        for k in range(nk):
            o_ref[pl.ds(k, blk, stride=8), :] = out[:, k * LANES:(k + 1) * LANES]

    @pl.when(i >= nu_ref[0])
    def _():
        o_ref[...] = jnp.zeros(o_ref.shape, F32)


def _ffn(block_expert, n_used, xs, wg, bg, wu, bu, wd, bd, blk):
    cap = xs.shape[0]
    nblk = cap // blk
    x2 = xs.reshape(cap * 8, LANES)
    wspec = pl.BlockSpec((1, D_MODEL, D_FF), lambda i, be, nu: (be[i], 0, 0))
    wdspec = pl.BlockSpec((1, D_FF, D_MODEL), lambda i, be, nu: (be[i], 0, 0))
    bspec = lambda w: pl.BlockSpec((1, 1, w), lambda i, be, nu: (be[i], 0, 0))
    rows = pl.BlockSpec((blk * 8, LANES), lambda i, be, nu: (i, 0))
    rows_in = pl.BlockSpec((blk * 8, LANES), lambda i, be, nu: (jnp.minimum(i, nu[0] - 1), 0))
    assert D_FF == D_MODEL
    wscratch = pltpu.VMEM((D_MODEL, D_FF), BF16)
    out = pl.pallas_call(
        functools.partial(_ffn_kernel, blk=blk),
        grid_spec=pltpu.PrefetchScalarGridSpec(
            num_scalar_prefetch=2, grid=(nblk,),
            in_specs=[rows_in, wspec, bspec(D_FF), wspec, bspec(D_FF), wdspec, bspec(D_MODEL)],
            out_specs=rows,
            scratch_shapes=[wscratch, wscratch, wscratch]),
        out_shape=jax.ShapeDtypeStruct((cap * 8, LANES), F32),
        compiler_params=pltpu.CompilerParams(
            dimension_semantics=("arbitrary",), vmem_limit_bytes=VMEM_LIMIT),
        name="ffn",
    )(block_expert, n_used, x2, wg, bg, wu, bu, wd, bd)
    return out.reshape(cap, 8, LANES)


def _combine_kernel(slot_ref, gate_ref, h2_ref, g_ref, ys_hbm, o_ref, buf_ref, sem, *, tm):
    def copy(t, jx):
        dst = buf_ref.at[pl.ds(pl.multiple_of((jx * tm + t) * 8, 8), 8), :]
        return pltpu.make_async_copy(ys_hbm.at[slot_ref[0, 0, t * TOP_K + jx]], dst, sem)

    def issue(t, c):
        for jx in range(TOP_K):
            copy(t, jx).start()
        return c

    def drain(t, c):
        for jx in range(TOP_K):
            copy(t, jx).wait()
        return c

    lax.fori_loop(0, tm, issue, 0)
    lax.fori_loop(0, tm, drain, 0)

    gate = gate_ref[0]
    h2 = h2_ref[0]
    cols = []
    for k in range(D_MODEL // LANES):
        y = h2[:, k * LANES:(k + 1) * LANES]
        for jx in range(TOP_K):
            y = y + gate[:, jx:jx + 1] * buf_ref[pl.ds(jx * tm * 8 + k, tm, stride=8), :]
        cols.append(y)
    ho = jnp.concatenate(cols, axis=1)
    o_ref[0] = ho * lax.rsqrt(jnp.mean(ho * ho, axis=-1, keepdims=True) + EPS) * g_ref[...]


def _combine(slot_s, gate, h2, g, ys, tm):
    B, Lp, D = h2.shape
    nt = Lp // tm
    row = lambda w: pl.BlockSpec((1, tm, w), lambda b, i: (b, i, 0))
    return pl.pallas_call(
        functools.partial(_combine_kernel, tm=tm),
        grid=(B, nt),
        in_specs=[pl.BlockSpec((1, 1, tm * TOP_K), lambda b, i: (b * nt + i, 0, 0), memory_space=pltpu.SMEM),
                  row(LANES), row(D),
                  pl.BlockSpec((1, D), lambda b, i: (0, 0)),
                  pl.BlockSpec(memory_space=pl.ANY)],
        out_specs=row(D),
        out_shape=jax.ShapeDtypeStruct((B, Lp, D), F32),
        scratch_shapes=[pltpu.VMEM((TOP_K * tm * 8, LANES), F32), pltpu.SemaphoreType.DMA(())],
        compiler_params=pltpu.CompilerParams(
            dimension_semantics=("arbitrary", "arbitrary"), vmem_limit_bytes=VMEM_LIMIT),
        name="combine",
    )(slot_s, gate, h2, g.reshape(1, D), ys)


def _moe(h2, xrow, topi, gate, rank, cnt, tm, ffn_w, final_g):
    B, Lp, D = h2.shape
    T = B * Lp
    nt_total = cnt.shape[0]
    blk = FFN_BLOCK
    nblk = -(-(T * TOP_K) // blk) + N_EXPERTS
    cap = nblk * blk
    cnt2 = cnt.reshape(nt_total, LANES)
    totals = jnp.sum(cnt2, axis=0)
    padded = (totals + blk - 1) // blk * blk
    pad_ends = jnp.cumsum(padded)
    pad_starts = pad_ends - padded
    base = (pad_starts[None, :] + jnp.cumsum(cnt2, axis=0) - cnt2).astype(I32).reshape(nt_total, 1, LANES)
    block_start = (jnp.arange(nblk) * blk)[:, None]
    block_expert = jnp.minimum(
        jnp.sum(pad_ends[None, :N_EXPERTS] <= block_start, axis=1), N_EXPERTS - 1).astype(I32)
    n_used = (pad_ends[N_EXPERTS - 1] // blk).astype(I32).reshape(1)
    slot = _slots(topi, rank, base, tm)
    slot_s = slot[:, :, :TOP_K].reshape(nt_total, 1, tm * TOP_K)

    pad_lo = (pad_starts + totals)[:N_EXPERTS].astype(I32)
    pad_n = (padded - totals)[:N_EXPERTS].astype(I32)
    xs = _dispatch(pad_lo, pad_n, n_used, slot_s, xrow.reshape(T, 8, LANES), cap, tm, blk)
    ys = _ffn(block_expert, n_used, xs, *ffn_w, blk)
    return _combine(slot_s, gate, h2, final_g, ys, tm)


def kernel(x, meta_tokens, attn_norm_g, w_in, hgrn_norm_g, w_up_attn, w_up_hgrn, w_out,
           hgrn_lb_logits, ffn_norm_g, router_w, router_b, w_gate, b_gate, w_up, b_up,
           w_down, b_down, final_norm_g):
    B, S, D = x.shape
    L = S + N_META
    Lp = -(-L // Q_BLOCK) * Q_BLOCK
    k_sel = min(TOPK_MAX, L // 4)
    meta = jnp.broadcast_to(meta_tokens[None].astype(x.dtype), (B, N_META, D))
    h = jnp.concatenate([meta, x, jnp.zeros((B, Lp - L, D), x.dtype)], axis=1)
    lb = jnp.cumsum(jax.nn.softmax(hgrn_lb_logits.astype(F32), axis=0), axis=0)[0]

    (q, kx, vx, iq, ikf, ikb, hq, hf, hi, hg, ga, gh) = _inproj(h, attn_norm_g[0], _arrange_w_in(w_in[0]))
    att = _attn(q, iq, ikf, ikb, kx, vx, k_sel)
    hgr = _hgrn(hq, hf, hi, hg, lb, hgrn_norm_g[0])

    wr = jnp.pad(router_w[0], ((0, 0), (0, LANES - N_EXPERTS)))
    wr_hi = wr.astype(BF16)
    wr_lo = (wr - wr_hi.astype(F32)).astype(BF16)
    br = jnp.pad(router_b[0], (0, LANES - N_EXPERTS), constant_values=-1e30).reshape(1, LANES)
    (h2, xrow, topi, gate, rank, cnt), tm = _merge(
        att, hgr, ga, gh, h, w_up_attn[0].astype(BF16), w_up_hgrn[0].astype(BF16), w_out[0].astype(BF16),
        ffn_norm_g[0], jnp.concatenate([wr_hi, wr_lo], axis=1), br)

    ffn_w = (w_gate[0], b_gate[0].reshape(N_EXPERTS, 1, D_FF),
             w_up[0], b_up[0].reshape(N_EXPERTS, 1, D_FF),
             w_down[0], b_down[0].reshape(N_EXPERTS, 1, D_MODEL))
    out = _moe(h2, xrow, topi, gate, rank, cnt, tm, ffn_w, final_norm_g)
    return out[:, N_META:L]
```

```python
import functools

import numpy as np
import jax
import jax.numpy as jnp
from jax import lax
from jax.experimental import pallas as pl
from jax.experimental.pallas import tpu as pltpu

F32 = jnp.float32
BF16 = jnp.bfloat16
I32 = jnp.int32

D_MODEL = 1024
N_META = 16
EPS = 1e-5
ROPE_THETA = 500000.0
ATT_HEADS = 8
ATT_KV_HEADS = 2
ATT_HEAD_DIM = 64
ATT_WIDTH = ATT_HEADS * ATT_HEAD_DIM
IDX_HEADS = 8
IDX_DIM = 32
TOPK_MAX = 256
Q_BLOCK = 128
HG_HEADS = 4
HG_DK = 128
HG_DV = 128
HG_WIDTH = HG_HEADS * HG_DV
N_EXPERTS = 32
TOP_K = 4
D_FF = 1024
SWIGLU_LIMIT = 7.0
SWIGLU_ALPHA = 1.702

LANES = 128
VMEM_LIMIT = 56 * 1024 * 1024
INT_MIN = -2 ** 31
NEG_INF = float("-inf")
LOG2E = 1.4426950408889634

HG_CHUNK = 128
FFN_BLOCK = 256


def _pick_tile(n, candidates):
    for c in candidates:
        if n % c == 0:
            return c
    raise ValueError(f"no tile for {n}")


def _dot(a, b):
    return jnp.dot(a, b, preferred_element_type=F32)


def _dot_nt(a, b):
    return lax.dot_general(a, b, (((1,), (1,)), ((), ())), preferred_element_type=F32)


def _dot_tn(a, b):
    return lax.dot_general(a, b, (((0,), (0,)), ((), ())), preferred_element_type=F32)


def _lane_iota(shape):
    return lax.broadcasted_iota(I32, shape, len(shape) - 1)


def _rope_tables(n_pos, head_dim, lane_valid):
    rot = head_dim // 4
    half = rot // 2
    inv = ROPE_THETA ** (-jnp.arange(half, dtype=F32) * 2.0 / rot)
    ang = jnp.arange(n_pos).astype(F32)[:, None] * inv[None, :]
    c = jnp.cos(ang)
    s = jnp.sin(ang)
    lane = np.arange(LANES)
    within = lane % head_dim
    valid = lane < lane_valid
    first = (within < half) & valid
    second = (within >= half) & (within < rot) & valid
    idx = np.where(within < half, within, within - half) % half
    cc = c[:, idx]
    ss = s[:, idx]
    tc = jnp.where((first | second)[None, :], cc, 1.0)
    t1 = jnp.where(first[None, :], -ss, 0.0)
    t2 = jnp.where(second[None, :], ss, 0.0)
    return jnp.stack([tc, t1, t2]).astype(F32), half


def _rope_apply(x, tab_ref, half):
    c = tab_ref[0]
    s1 = tab_ref[1]
    s2 = tab_ref[2]
    cols = []
    for j in range(x.shape[1] // LANES):
        xc = x[:, j * LANES:(j + 1) * LANES]
        cols.append(xc * c + pltpu.roll(xc, LANES - half, 1) * s1 + pltpu.roll(xc, half, 1) * s2)
    return cols[0] if len(cols) == 1 else jnp.concatenate(cols, axis=1)


_W_Q = (0, 512)
_W_KX = (512, 768)
_W_VX = (768, 1024)
_W_IQ = (1024, 1280)
_W_IK = (1280, 1408)
_W_HQ = (1408, 1920)
_W_HF = (1920, 2432)
_W_HI = (2432, 2944)
_W_HG = (2944, 3456)
_W_GA = (3456, 4480)
_W_GH = (4480, 5504)
_W_COLS = 5504


def _arrange_w_in(w_in):
    o = np.cumsum([0, 512, 128, 128, 256, 32, 8, 512, 512, 512, 512, 1024, 1024])
    seg = [w_in[:, o[i]:o[i + 1]] for i in range(12)]
    q, k, v, iq, ik, iw, hq, hf, hi, hg, ga, gh = seg
    z64 = jnp.zeros((D_MODEL, 64), w_in.dtype)
    z88 = jnp.zeros((D_MODEL, 88), w_in.dtype)
    kx = jnp.concatenate([k[:, :64], z64, k[:, 64:], z64], axis=1)
    vx = jnp.concatenate([v[:, :64], z64, v[:, 64:], z64], axis=1)
    ikx = jnp.concatenate([ik, iw, z88], axis=1)
    w = jnp.concatenate([q, kx, vx, iq, ikx, hq, hf, hi, hg, ga, gh], axis=1)
    assert w.shape[1] == _W_COLS
    return w.astype(BF16)


def _inproj_kernel(h_ref, g_ref, w_ref, tatt_ref, tkx_ref, tiq_ref, tik_ref,
                   q_ref, kx_ref, vx_ref, iq_ref, ikf_ref, ikb_ref,
                   hq_ref, hf_ref, hi_ref, hg_ref, ga_ref, gh_ref, *, half_att, half_idx):
    x = h_ref[0]
    xn = x * lax.rsqrt(jnp.mean(x * x, axis=-1, keepdims=True) + EPS) * g_ref[...]
    xb = xn.astype(BF16)

    def proj(rng):
        return _dot(xb, w_ref[:, rng[0]:rng[1]])

    q_ref[0] = _rope_apply(proj(_W_Q), tatt_ref, half_att)
    kx = _rope_apply(proj(_W_KX), tkx_ref, half_att).astype(BF16)
    kx_ref[0, 0] = kx[:, :LANES]
    kx_ref[0, 1] = kx[:, LANES:]
    vx = proj(_W_VX)
    ones_hi = jnp.where(_lane_iota(vx.shape) % LANES >= ATT_HEAD_DIM, 1.0, 0.0)
    vx = (vx + ones_hi).astype(BF16)
    vx_ref[0, 0] = vx[:, :LANES]
    vx_ref[0, 1] = vx[:, LANES:]
    iq_ref[0] = _rope_apply(proj(_W_IQ), tiq_ref, half_idx)
    ik = _rope_apply(proj(_W_IK), tik_ref, half_idx)
    ikf_ref[0] = ik
    ikb_ref[0] = ik.astype(BF16)
    hq_ref[0] = proj(_W_HQ)
    hf_ref[0] = proj(_W_HF)
    hi_ref[0] = proj(_W_HI)
    hg_ref[0] = proj(_W_HG)
    ga_ref[0] = proj(_W_GA)
    gh_ref[0] = proj(_W_GH)


def _inproj(h, g, w_arr):
    B, Lp, D = h.shape
    tm = _pick_tile(Lp, (320, 256, 128))
    tatt, half_att = _rope_tables(Lp, ATT_HEAD_DIM, LANES)
    tkx, _ = _rope_tables(Lp, ATT_HEAD_DIM, ATT_HEAD_DIM)
    tiq, half_idx = _rope_tables(Lp, IDX_DIM, LANES)
    tik, _ = _rope_tables(Lp, IDX_DIM, IDX_DIM)

    def rows(width, dtype=F32):
        return (jax.ShapeDtypeStruct((B, Lp, width), dtype),
                pl.BlockSpec((1, tm, width), lambda b, i: (b, i, 0)))

    def kv(dtype=BF16):
        return (jax.ShapeDtypeStruct((B, ATT_KV_HEADS, Lp, LANES), dtype),
                pl.BlockSpec((1, ATT_KV_HEADS, tm, LANES), lambda b, i: (b, 0, i, 0)))

    outs = [rows(512), kv(), kv(), rows(256), rows(128), rows(128, BF16),
            rows(512), rows(512), rows(512), rows(512), rows(1024), rows(1024)]
    tab_spec = pl.BlockSpec((3, tm, LANES), lambda b, i: (0, i, 0))
    return pl.pallas_call(
        functools.partial(_inproj_kernel, half_att=half_att, half_idx=half_idx),
        grid=(B, Lp // tm),
        in_specs=[pl.BlockSpec((1, tm, D), lambda b, i: (b, i, 0)),
                  pl.BlockSpec((1, D), lambda b, i: (0, 0)),
                  pl.BlockSpec((D, _W_COLS), lambda b, i: (0, 0)),
                  tab_spec, tab_spec, tab_spec, tab_spec],
        out_specs=[o[1] for o in outs],
        out_shape=[o[0] for o in outs],
        compiler_params=pltpu.CompilerParams(
            dimension_semantics=("arbitrary", "arbitrary"), vmem_limit_bytes=VMEM_LIMIT),
        name="inproj",
    )(h, g.reshape(1, D), w_arr, tatt, tkx, tiq, tik)


def _attn_kernel(q_ref, iq_ref, ikf_ref, ikb_ref, kx_ref, vt_ref, ltri_ref, o_ref,
                 keys_ref, acc_ref, m_ref, bias_ref, lg_ref, p_ref, *, C, k_sel):
    j = pl.program_id(1)
    nck = ((j + 1) * Q_BLOCK + C - 1) // C
    lane = _lane_iota((Q_BLOCK, LANES))
    qpos = j * Q_BLOCK + lax.broadcasted_iota(I32, (1, Q_BLOCK), 1)
    nsub = C // 8
    nacc = 8

    def pairs(heads):
        return [jnp.concatenate([heads[2 * p], heads[2 * p + 1]], axis=0) for p in range(len(heads) // 2)]

    iqf = iq_ref[0]
    per_blk = LANES // IDX_DIM
    iqh = []
    for h in range(IDX_HEADS):
        blk = iqf[:, (h // per_blk) * LANES:(h // per_blk + 1) * LANES]
        off = (h % per_blk) * IDX_DIM
        if off:
            blk = pltpu.roll(blk, LANES - off, 1)
        iqh.append(jnp.where(lane < IDX_DIM, blk, 0.0).astype(BF16))
    iqpair = pairs(iqh)
    wt = ikf_ref[0].T
    wrow = [wt[IDX_DIM + h:IDX_DIM + h + 1, :] for h in range(IDX_HEADS)]

    def causal_mask(c):
        kpos = c * C + lax.broadcasted_iota(I32, (C, 1), 0)
        return kpos <= qpos

    def phase_a(c, carry):
        ks = ikb_ref[0, pl.ds(pl.multiple_of(c * C, C), C), :]
        s = jnp.zeros((C, Q_BLOCK), F32)
        for p in range(IDX_HEADS // 2):
            d2 = _dot_nt(ks, iqpair[p])
            s = (s + jnp.maximum(d2[:, :LANES], 0.0) * wrow[2 * p]
                 + jnp.maximum(d2[:, LANES:], 0.0) * wrow[2 * p + 1])
        bits = pltpu.bitcast(s, I32)
        key = jnp.where(bits < 0, bits ^ jnp.int32(0x7FFFFFFF), bits)
        keys_ref[c] = jnp.where(causal_mask(c), key, jnp.int32(INT_MIN))
        return carry

    lax.fori_loop(0, nck, phase_a, 0)

    def count(cand, strict):
        def body(c, acc):
            kc = keys_ref[c]
            hit = (kc > cand) if strict else (kc >= cand)
            return acc + jnp.sum(jnp.where(hit, 1, 0).reshape(nacc, nsub // nacc, 8, LANES), axis=1)

        acc = lax.fori_loop(0, nck, body, jnp.zeros((nacc, 8, LANES), I32))
        return jnp.sum(jnp.sum(acc, axis=0), axis=0, keepdims=True)

    def search(i, carry):
        prefix, nge = carry
        bit = lax.shift_left(jnp.int32(1), jnp.int32(31) - i)
        cand = prefix ^ bit
        cnt = count(cand, False)
        ok = cnt >= k_sel
        return jnp.where(ok, cand, prefix), jnp.where(ok, cnt, nge)

    thr, nge = lax.fori_loop(
        0, 32, search,
        (jnp.full((1, Q_BLOCK), INT_MIN, I32), jnp.zeros((1, Q_BLOCK), I32)))
    excess = jnp.where((nge > k_sel) & (thr != INT_MIN), 1, 0)
    any_excess = jnp.max(excess) > 0
    need = lax.cond(any_excess,
                    lambda: (k_sel - count(thr, True)).astype(F32),
                    lambda: jnp.zeros((1, Q_BLOCK), F32))
    thr_lo = jnp.maximum(thr, jnp.int32(INT_MIN + 1))

    qf = q_ref[0]
    qh = []
    for h in range(ATT_HEADS):
        blk = qf[:, (h // 2) * LANES:(h // 2 + 1) * LANES]
        if h % 2:
            blk = pltpu.roll(blk, ATT_HEAD_DIM, 1)
        qh.append((jnp.where(lane < ATT_HEAD_DIM, blk, 0.0) * (LOG2E * ATT_HEAD_DIM ** -0.5)).astype(BF16))
    qpair = pairs(qh)
    kv_of_pair = [(2 * p) // (ATT_HEADS // ATT_KV_HEADS) for p in range(ATT_HEADS // 2)]

    acc_ref[...] = jnp.zeros(acc_ref.shape, F32)
    m_ref[...] = jnp.full(m_ref.shape, NEG_INF, F32)
    p_ref[...] = jnp.zeros(p_ref.shape, BF16)
    npair = ATT_HEADS // 2

    def to_mask(c, eq_seen):
        kc = keys_ref[c]

        def bias_simple():
            return jnp.where(kc >= thr_lo, 0.0, NEG_INF), eq_seen

        def bias_ties():
            eq = (kc == thr) & causal_mask(c)
            eqf = jnp.where(eq, 1.0, 0.0)
            rank = eq_seen + _dot(ltri_ref[...], eqf.astype(BF16))
            take = (kc > thr) | (eq & (rank < need))
            return jnp.where(take, 0.0, NEG_INF), eq_seen + jnp.sum(eqf, axis=0, keepdims=True)

        bias, eq_next = lax.cond(any_excess, bias_ties, bias_simple)
        keys_ref[c] = pltpu.bitcast(bias, I32)
        return eq_next

    lax.fori_loop(0, nck, to_mask, jnp.zeros((1, Q_BLOCK), F32))

    def logits(c_any, slot):
        c = jnp.minimum(c_any, nck - 1)
        bias = pltpu.bitcast(keys_ref[c], F32) + jnp.where(c_any < nck, 0.0, NEG_INF)
        bias_ref[slot] = bias
        start = pl.multiple_of(c * C, C)
        for p in range(npair):
            lg2 = _dot_nt(kx_ref[0, kv_of_pair[p], pl.ds(start, C), :], qpair[p])
            lg_ref[slot * ATT_HEADS + 2 * p] = lg2[:, :LANES] + bias_ref[slot]
            lg_ref[slot * ATT_HEADS + 2 * p + 1] = lg2[:, LANES:] + bias_ref[slot]

    def weighted_values(c_any, slot, alpha):
        c = jnp.clip(c_any, 0, nck - 1)
        for p in range(npair):
            o2 = _dot(vt_ref[0, kv_of_pair[p], c], p_ref[slot * npair + p])
            for s in range(2):
                h = 2 * p + s
                acc_ref[h] = alpha[h:h + 1] * acc_ref[h] + o2[:, s * LANES:(s + 1) * LANES]

    def softmax(slot):
        shifts, alphas = [], []
        for h in range(ATT_HEADS):
            m_old = m_ref[h]
            mc = jnp.max(lg_ref[slot * ATT_HEADS + h].reshape(nsub, 8, LANES), axis=0)
            m_new = jnp.maximum(m_old, jnp.max(mc, axis=0, keepdims=True))
            m_ref[h] = m_new
            m_safe = jnp.where(m_new == NEG_INF, 0.0, m_new)
            shifts.append(m_safe[:1])
            alphas.append(jnp.exp2(m_old - m_safe)[:1])
        for h in range(ATT_HEADS):
            p_ref[slot * npair + h // 2, :, (h % 2) * LANES:(h % 2 + 1) * LANES] = (
                jnp.exp2(lg_ref[slot * ATT_HEADS + h] - shifts[h]).astype(BF16))
        return jnp.concatenate(alphas, axis=0)

    def step(c, cur, alpha_prev):
        nxt = 1 - cur
        logits(c + 1, nxt)
        weighted_values(c - 1, nxt, alpha_prev)
        return softmax(cur)

    def phase_c(i, alpha_prev):
        return step(2 * i + 1, 1, step(2 * i, 0, alpha_prev))

    logits(0, 0)
    alpha_last = lax.fori_loop(0, (nck + 1) // 2, phase_c, jnp.ones((ATT_HEADS, Q_BLOCK), F32))
    weighted_values(2 * ((nck + 1) // 2) - 1, 1, alpha_last)

    for p in range(ATT_HEADS // 2):
        a = acc_ref[2 * p]
        b = acc_ref[2 * p + 1]
        ot = jnp.concatenate([a[:ATT_HEAD_DIM] / a[ATT_HEAD_DIM:ATT_HEAD_DIM + 1],
                              b[:ATT_HEAD_DIM] / b[ATT_HEAD_DIM:ATT_HEAD_DIM + 1]], axis=0)
        o_ref[0, :, p * LANES:(p + 1) * LANES] = ot.T


def _attn(q, iq, ikf, ikb, kx, vx, k_sel):
    B, Lp, _ = q.shape
    C = _pick_tile(Lp, (640, 512, 256, 128))
    nch = Lp // C
    ltri = jnp.asarray(np.tril(np.ones((C, C), np.float32), -1), BF16)
    vt = vx.reshape(B, ATT_KV_HEADS, nch, C, LANES).transpose(0, 1, 2, 4, 3)
    qspec = lambda w: pl.BlockSpec((1, Q_BLOCK, w), lambda b, j: (b, j, 0))
    return pl.pallas_call(
        functools.partial(_attn_kernel, C=C, k_sel=k_sel),
        grid=(B, Lp // Q_BLOCK),
        in_specs=[qspec(512), qspec(256), qspec(128),
                  pl.BlockSpec((1, Lp, LANES), lambda b, j: (b, 0, 0)),
                  pl.BlockSpec((1, ATT_KV_HEADS, Lp, LANES), lambda b, j: (b, 0, 0, 0)),
                  pl.BlockSpec((1, ATT_KV_HEADS, nch, LANES, C), lambda b, j: (b, 0, 0, 0, 0)),
                  pl.BlockSpec((C, C), lambda b, j: (0, 0))],
        out_specs=qspec(512),
        out_shape=jax.ShapeDtypeStruct((B, Lp, ATT_WIDTH), F32),
        scratch_shapes=[pltpu.VMEM((nch, C, Q_BLOCK), I32),
                        pltpu.VMEM((ATT_HEADS, LANES, Q_BLOCK), F32),
                        pltpu.VMEM((ATT_HEADS, 8, Q_BLOCK), F32),
                        pltpu.VMEM((2, C, Q_BLOCK), F32),
                        pltpu.VMEM((2 * ATT_HEADS, C, Q_BLOCK), F32),
                        pltpu.VMEM((2 * (ATT_HEADS // 2), C, 2 * Q_BLOCK), BF16)],
        compiler_params=pltpu.CompilerParams(
            dimension_semantics=("arbitrary", "arbitrary"), vmem_limit_bytes=VMEM_LIMIT),
        name="attn",
    )(q, iq, ikf, ikb, kx, vt, ltri)


def _hgrn_sum_matrix(C):
    nlev = int(np.log2(C))
    t = np.arange(C)[:, None]
    jj = np.arange(C)[None, :]
    mats = [(jj <= t), (jj > t)]
    for l in range(nlev):
        s = 1 << l
        mid = ((t >> (l + 1)) << (l + 1)) + s - 1
        odd = ((t >> l) & 1) == 1
        mats.append(np.where(odd, (jj > mid) & (jj <= t), (jj > t) & (jj <= mid)))
    return np.concatenate(mats, axis=0).astype(np.float32), nlev


def _hgrn_kernel(hq_ref, hf_ref, hi_ref, hg_ref, lb_ref, gn_ref, e_ref, o_ref, st_ref, *, C, nlev):
    @pl.when(pl.program_id(1) == 0)
    def _():
        st_ref[...] = jnp.zeros(st_ref.shape, F32)

    for h in range(HG_HEADS):
        _hgrn_head(hq_ref, hf_ref, hi_ref, hg_ref, lb_ref, gn_ref, e_ref, o_ref, st_ref, h, C, nlev)


def _hgrn_head(hq_ref, hf_ref, hi_ref, hg_ref, lb_ref, gn_ref, e_ref, o_ref, st_ref, h, C, nlev):
    cols = slice(h * HG_DK, (h + 1) * HG_DK)
    lb = lb_ref[:, cols]
    f = lb + (1.0 - lb) * jax.nn.sigmoid(hf_ref[0, :, cols])
    logf = jnp.log(f)
    kk = 1.0 - f
    q = hq_ref[0, :, cols] * (HG_DK ** -0.5)
    v = hi_ref[0, :, cols].astype(BF16)

    p1 = logf.astype(BF16)
    r1 = logf - p1.astype(F32)
    p2 = r1.astype(BF16)
    p3 = (r1 - p2.astype(F32)).astype(BF16)
    x3 = _dot(e_ref[...], jnp.concatenate([p1, p2, p3], axis=1))
    xs = x3[:, :LANES] + x3[:, LANES:2 * LANES] + x3[:, 2 * LANES:]

    b = xs[0:C]
    suffix = xs[C:2 * C]
    rt = lax.broadcasted_iota(I32, (C, C), 0)
    ct = lax.broadcasted_iota(I32, (C, C), 1)
    row = lax.broadcasted_iota(I32, (C, 1), 0)

    scores = jnp.where(rt == ct, _dot_nt(q.astype(BF16), kk.astype(BF16)), 0.0)
    for l in range(nlev):
        w = jnp.exp(xs[(2 + l) * C:(3 + l) * C])
        odd = ((row >> l) & 1) == 1
        a = (jnp.where(odd, q, kk) * w).astype(BF16)
        pair = (((rt >> l) & 1) == 1) & ((ct >> l) == (rt >> l) - 1)
        scores = scores + jnp.where(pair, _dot_nt(a, a), 0.0)

    st = st_ref[h]
    qb = (q * jnp.exp(b)).astype(BF16)
    o = _dot(scores.astype(BF16), v) + _dot_nt(qb, st.astype(BF16))
    ks = (kk * jnp.exp(suffix)).astype(BF16)
    st_ref[h] = jnp.exp(b[C - 1:C, :]) * st + _dot_tn(v, ks)

    o = o * lax.rsqrt(jnp.mean(o * o, axis=-1, keepdims=True) + EPS) * gn_ref[:, cols]
    hg = hg_ref[0, :, cols]
    o_ref[0, :, cols] = o * (hg * jax.nn.sigmoid(hg))


def _hgrn(hq, hf, hi, hg, lb, gn):
    B, Lp, _ = hq.shape
    C = HG_CHUNK
    e_np, nlev = _hgrn_sum_matrix(C)
    e = jnp.asarray(e_np, BF16)
    blk = pl.BlockSpec((1, C, HG_WIDTH), lambda b, c: (b, c, 0))
    vec = pl.BlockSpec((1, HG_WIDTH), lambda b, c: (0, 0))
    return pl.pallas_call(
        functools.partial(_hgrn_kernel, C=C, nlev=nlev),
        grid=(B, Lp // C),
        in_specs=[blk, blk, blk, blk, vec, vec,
                  pl.BlockSpec(e.shape, lambda b, c: (0, 0))],
        out_specs=blk,
        out_shape=jax.ShapeDtypeStruct((B, Lp, HG_WIDTH), F32),
        scratch_shapes=[pltpu.VMEM((HG_HEADS, HG_DV, HG_DK), F32)],
        compiler_params=pltpu.CompilerParams(
            dimension_semantics=("arbitrary", "arbitrary"), vmem_limit_bytes=VMEM_LIMIT),
        name="hgrn",
    )(hq, hf, hi, hg, lb.reshape(1, -1), gn.reshape(1, -1), e)


def _merge_kernel(att_ref, hgr_ref, ga_ref, gh_ref, h_ref, wa_ref, wh_ref, wo_ref, g_ref,
                  wr_ref, br_ref, lt_ref,
                  h2_ref, xrow_ref, topi_ref, gate_ref, rank_ref, cnt_ref, *, tm):
    mix = (jax.nn.sigmoid(ga_ref[0]) * _dot(att_ref[0].astype(BF16), wa_ref[...])
           + jax.nn.sigmoid(gh_ref[0]) * _dot(hgr_ref[0].astype(BF16), wh_ref[...]))
    h2 = h_ref[0] + _dot(mix.astype(BF16), wo_ref[...])
    h2_ref[0] = h2
    xn = h2 * lax.rsqrt(jnp.mean(h2 * h2, axis=-1, keepdims=True) + EPS) * g_ref[...]
    for k in range(D_MODEL // LANES):
        xrow_ref[pl.ds(k, tm, stride=8), :] = xn[:, k * LANES:(k + 1) * LANES]

    xh = xn.astype(BF16)
    xl = (xn - xh.astype(F32)).astype(BF16)
    l1 = _dot(xh, wr_ref[...])
    logits = l1[:, :LANES] + l1[:, LANES:] + _dot(xl, wr_ref[:, :LANES]) + br_ref[...]

    lane = _lane_iota((tm, LANES))
    lanef = lane.astype(F32)
    work = logits
    vals, sels, idxs = [], [], []
    for _ in range(TOP_K):
        m = jnp.max(work, axis=1, keepdims=True)
        idx = jnp.min(jnp.where(work == m, lanef, float(LANES)), axis=1, keepdims=True)
        sel = lanef == idx
        vals.append(m)
        idxs.append(idx)
        sels.append(sel)
        work = jnp.where(sel, NEG_INF, work)
    es = [jnp.exp(v - vals[0]) for v in vals]
    den = es[0] + es[1] + es[2] + es[3]

    member = jnp.zeros((tm, LANES), F32)
    for sel in sels:
        member = member + jnp.where(sel, 1.0, 0.0)
    before = _dot(lt_ref[...], member.astype(BF16))
    topi = jnp.zeros((tm, LANES), F32)
    gate = jnp.zeros((tm, LANES), F32)
    rank = jnp.zeros((tm, LANES), F32)
    for jx in range(TOP_K):
        here = lane == jx
        topi = jnp.where(here, idxs[jx], topi)
        gate = jnp.where(here, es[jx] / den, gate)
        rank = jnp.where(here, jnp.sum(jnp.where(sels[jx], before, 0.0), axis=1, keepdims=True), rank)
    topi_ref[0] = topi.astype(I32)
    gate_ref[0] = gate
    rank_ref[0] = rank.astype(I32)
    cnt_ref[0] = jnp.sum(member, axis=0, keepdims=True).astype(I32)


def _merge(att, hgr, ga, gh, h, wa, wh, wo, g, wr2, br):
    B, Lp, D = h.shape
    tm = _pick_tile(Lp, (320, 256, 128))
    nt = Lp // tm
    lt = jnp.asarray(np.tril(np.ones((tm, tm), np.float32), -1), BF16)
    row = lambda w: pl.BlockSpec((1, tm, w), lambda b, i: (b, i, 0))
    full = lambda a: pl.BlockSpec(a.shape, lambda b, i: (0,) * a.ndim)
    g2 = g.reshape(1, D)
    out_shape = [jax.ShapeDtypeStruct((B, Lp, D), F32),
                 jax.ShapeDtypeStruct((B * Lp * 8, LANES), F32),
                 jax.ShapeDtypeStruct((B, Lp, LANES), I32),
                 jax.ShapeDtypeStruct((B, Lp, LANES), F32),
                 jax.ShapeDtypeStruct((B, Lp, LANES), I32),
                 jax.ShapeDtypeStruct((B * nt, 1, LANES), I32)]
    out_specs = [row(D),
                 pl.BlockSpec((tm * 8, LANES), lambda b, i: (b * nt + i, 0)),
                 row(LANES), row(LANES), row(LANES),
                 pl.BlockSpec((1, 1, LANES), lambda b, i: (b * nt + i, 0, 0))]
    return pl.pallas_call(
        functools.partial(_merge_kernel, tm=tm),
        grid=(B, nt),
        in_specs=[row(512), row(512), row(D), row(D), row(D),
                  full(wa), full(wh), full(wo), full(g2), full(wr2), full(br), full(lt)],
        out_specs=out_specs,
        out_shape=out_shape,
        compiler_params=pltpu.CompilerParams(
            dimension_semantics=("arbitrary", "arbitrary"), vmem_limit_bytes=VMEM_LIMIT),
        name="merge",
    )(att, hgr, ga, gh, h, wa, wh, wo, g2, wr2, br, lt), tm


def _slots_kernel(topi_ref, rank_ref, base_ref, slot_ref):
    topi = topi_ref[0]
    rank = rank_ref[0]
    base = base_ref[0]
    lane = _lane_iota(topi.shape)
    slot = jnp.zeros(topi.shape, I32)
    for jx in range(TOP_K):
        mine = lane == topi[:, jx:jx + 1]
        first = jnp.sum(jnp.where(mine, base, 0), axis=1, keepdims=True)
        slot = jnp.where(lane == jx, first + rank[:, jx:jx + 1], slot)
    slot_ref[0] = slot


def _slots(topi, rank, base, tm):
    B, Lp, _ = topi.shape
    nt = Lp // tm
    row = pl.BlockSpec((1, tm, LANES), lambda b, i: (b, i, 0))
    return pl.pallas_call(
        _slots_kernel,
        grid=(B, nt),
        in_specs=[row, row, pl.BlockSpec((1, 1, LANES), lambda b, i: (b * nt + i, 0, 0))],
        out_specs=row,
        out_shape=jax.ShapeDtypeStruct((B, Lp, LANES), I32),
        compiler_params=pltpu.CompilerParams(dimension_semantics=("arbitrary", "arbitrary")),
        name="slots",
    )(topi, rank, base)


def _dispatch_kernel(pad_lo_ref, pad_n_ref, nu_ref, slot_ref, x_ref, xs_hbm, zblk_ref, sem, *, tm, blk, nblk):
    @pl.when(pl.program_id(0) == 0)
    def _():
        zblk_ref[...] = jnp.zeros(zblk_ref.shape, F32)

        def zblock(b):
            return pltpu.make_async_copy(zblk_ref, xs_hbm.at[pl.ds(b * blk, blk)], sem)

        def bissue(b, c):
            zblock(b).start()
            return c

        def bdrain(b, c):
            zblock(b).wait()
            return c

        lax.fori_loop(nu_ref[0], nblk, bissue, 0)
        lax.fori_loop(nu_ref[0], nblk, bdrain, 0)

        def per_expert(e, c):
            def zcopy(r):
                return pltpu.make_async_copy(zblk_ref.at[0], xs_hbm.at[pad_lo_ref[e] + r], sem)

            def zissue(r, c2):
                zcopy(r).start()
                return c2

            def zdrain(r, c2):
                zcopy(r).wait()
                return c2

            lax.fori_loop(0, pad_n_ref[e], zissue, 0)
            lax.fori_loop(0, pad_n_ref[e], zdrain, 0)
            return c

        lax.fori_loop(0, N_EXPERTS, per_expert, 0)

    def copy(t, jx):
        return pltpu.make_async_copy(x_ref.at[t], xs_hbm.at[slot_ref[0, 0, t * TOP_K + jx]], sem)

    def issue(t, c):
        for jx in range(TOP_K):
            copy(t, jx).start()
        return c

    def drain(t, c):
        for jx in range(TOP_K):
            copy(t, jx).wait()
        return c

    lax.fori_loop(0, tm, issue, 0)
    lax.fori_loop(0, tm, drain, 0)


def _dispatch(pad_lo, pad_n, n_used, slot_s, xrow, cap, tm, blk):
    nt = slot_s.shape[0]
    return pl.pallas_call(
        functools.partial(_dispatch_kernel, tm=tm, blk=blk, nblk=cap // blk),
        grid_spec=pltpu.PrefetchScalarGridSpec(
            num_scalar_prefetch=3, grid=(nt,),
            in_specs=[pl.BlockSpec((1, 1, tm * TOP_K), lambda i, lo, n, nu: (i, 0, 0), memory_space=pltpu.SMEM),
                      pl.BlockSpec((tm, 8, LANES), lambda i, lo, n, nu: (i, 0, 0))],
            out_specs=pl.BlockSpec(memory_space=pl.ANY),
            scratch_shapes=[pltpu.VMEM((blk, 8, LANES), F32), pltpu.SemaphoreType.DMA(())]),
        out_shape=jax.ShapeDtypeStruct((cap, 8, LANES), F32),
        compiler_params=pltpu.CompilerParams(dimension_semantics=("arbitrary",)),
        name="dispatch",
    )(pad_lo, pad_n, n_used, slot_s, xrow)


def _ffn_kernel(be_ref, nu_ref, x_ref, wg_ref, bg_ref, wu_ref, bu_ref, wd_ref, bd_ref, o_ref,
                wgb_ref, wub_ref, wdb_ref, *, blk):
    i = pl.program_id(0)
    nk = D_MODEL // LANES
    used = i < nu_ref[0]
    new_expert = (i == 0) | (be_ref[i] != be_ref[jnp.maximum(i - 1, 0)])

    @pl.when(used & new_expert)
    def _():
        rows = 128

        def cast(r, c):
            sl = pl.ds(pl.multiple_of(r * rows, rows), rows)
            wgb_ref[sl, :] = wg_ref[0, sl, :].astype(BF16)
            wub_ref[sl, :] = wu_ref[0, sl, :].astype(BF16)
            wdb_ref[sl, :] = wd_ref[0, sl, :].astype(BF16)
            return c

        lax.fori_loop(0, D_MODEL // rows, cast, 0)

    @pl.when(used)
    def _():
        xb = jnp.concatenate([x_ref[pl.ds(k, blk, stride=8), :] for k in range(nk)], axis=1).astype(BF16)
        g = _dot(xb, wgb_ref[...]) + bg_ref[0]
        u = _dot(xb, wub_ref[...]) + bu_ref[0]
        g = jnp.minimum(g, SWIGLU_LIMIT)
        u = jnp.clip(u, -SWIGLU_LIMIT, SWIGLU_LIMIT)
        act = (u + 1.0) * (g * jax.nn.sigmoid(SWIGLU_ALPHA * g))
        out = _dot(act.astype(BF16), wdb_ref[...]) + bd_ref[0]
        for k in range(nk):
            o_ref[pl.ds(k, blk, stride=8), :] = out[:, k * LANES:(k + 1) * LANES]

    @pl.when(i >= nu_ref[0])
    def _():
        o_ref[...] = jnp.zeros(o_ref.shape, F32)


def _ffn(block_expert, n_used, xs, wg, bg, wu, bu, wd, bd, blk):
    cap = xs.shape[0]
    nblk = cap // blk
    x2 = xs.reshape(cap * 8, LANES)
    wspec = pl.BlockSpec((1, D_MODEL, D_FF), lambda i, be, nu: (be[i], 0, 0))
    wdspec = pl.BlockSpec((1, D_FF, D_MODEL), lambda i, be, nu: (be[i], 0, 0))
    bspec = lambda w: pl.BlockSpec((1, 1, w), lambda i, be, nu: (be[i], 0, 0))
    rows = pl.BlockSpec((blk * 8, LANES), lambda i, be, nu: (i, 0))
    rows_in = pl.BlockSpec((blk * 8, LANES), lambda i, be, nu: (jnp.minimum(i, nu[0] - 1), 0))
    assert D_FF == D_MODEL
    wscratch = pltpu.VMEM((D_MODEL, D_FF), BF16)
    out = pl.pallas_call(
        functools.partial(_ffn_kernel, blk=blk),
        grid_spec=pltpu.PrefetchScalarGridSpec(
            num_scalar_prefetch=2, grid=(nblk,),
            in_specs=[rows_in, wspec, bspec(D_FF), wspec, bspec(D_FF), wdspec, bspec(D_MODEL)],
            out_specs=rows,
            scratch_shapes=[wscratch, wscratch, wscratch]),
        out_shape=jax.ShapeDtypeStruct((cap * 8, LANES), F32),
        compiler_params=pltpu.CompilerParams(
            dimension_semantics=("arbitrary",), vmem_limit_bytes=VMEM_LIMIT),
        name="ffn",
    )(block_expert, n_used, x2, wg, bg, wu, bu, wd, bd)
    return out.reshape(cap, 8, LANES)


def _combine_kernel(slot_ref, gate_ref, h2_ref, g_ref, ys_hbm, o_ref, buf_ref, sem, *, tm):
    def copy(t, jx):
        dst = buf_ref.at[pl.ds(pl.multiple_of((jx * tm + t) * 8, 8), 8), :]
        return pltpu.make_async_copy(ys_hbm.at[slot_ref[0, 0, t * TOP_K + jx]], dst, sem)

    def issue(t, c):
        for jx in range(TOP_K):
            copy(t, jx).start()
        return c

    def drain(t, c):
        for jx in range(TOP_K):
            copy(t, jx).wait()
        return c

    lax.fori_loop(0, tm, issue, 0)
    lax.fori_loop(0, tm, drain, 0)

    gate = gate_ref[0]
    h2 = h2_ref[0]
    cols = []
    for k in range(D_MODEL // LANES):
        y = h2[:, k * LANES:(k + 1) * LANES]
        for jx in range(TOP_K):
            y = y + gate[:, jx:jx + 1] * buf_ref[pl.ds(jx * tm * 8 + k, tm, stride=8), :]
        cols.append(y)
    ho = jnp.concatenate(cols, axis=1)
    o_ref[0] = ho * lax.rsqrt(jnp.mean(ho * ho, axis=-1, keepdims=True) + EPS) * g_ref[...]


def _combine(slot_s, gate, h2, g, ys, tm):
    B, Lp, D = h2.shape
    nt = Lp // tm
    row = lambda w: pl.BlockSpec((1, tm, w), lambda b, i: (b, i, 0))
    return pl.pallas_call(
        functools.partial(_combine_kernel, tm=tm),
        grid=(B, nt),
        in_specs=[pl.BlockSpec((1, 1, tm * TOP_K), lambda b, i: (b * nt + i, 0, 0), memory_space=pltpu.SMEM),
                  row(LANES), row(D),
                  pl.BlockSpec((1, D), lambda b, i: (0, 0)),
                  pl.BlockSpec(memory_space=pl.ANY)],
        out_specs=row(D),
        out_shape=jax.ShapeDtypeStruct((B, Lp, D), F32),
        scratch_shapes=[pltpu.VMEM((TOP_K * tm * 8, LANES), F32), pltpu.SemaphoreType.DMA(())],
        compiler_params=pltpu.CompilerParams(
            dimension_semantics=("arbitrary", "arbitrary"), vmem_limit_bytes=VMEM_LIMIT),
        name="combine",
    )(slot_s, gate, h2, g.reshape(1, D), ys)


def _moe(h2, xrow, topi, gate, rank, cnt, tm, ffn_w, final_g):
    B, Lp, D = h2.shape
    T = B * Lp
    nt_total = cnt.shape[0]
    blk = FFN_BLOCK
    nblk = -(-(T * TOP_K) // blk) + N_EXPERTS
    cap = nblk * blk
    cnt2 = cnt.reshape(nt_total, LANES)
    totals = jnp.sum(cnt2, axis=0)
    padded = (totals + blk - 1) // blk * blk
    pad_ends = jnp.cumsum(padded)
    pad_starts = pad_ends - padded
    base = (pad_starts[None, :] + jnp.cumsum(cnt2, axis=0) - cnt2).astype(I32).reshape(nt_total, 1, LANES)
    block_start = (jnp.arange(nblk) * blk)[:, None]
    block_expert = jnp.minimum(
        jnp.sum(pad_ends[None, :N_EXPERTS] <= block_start, axis=1), N_EXPERTS - 1).astype(I32)
    n_used = (pad_ends[N_EXPERTS - 1] // blk).astype(I32).reshape(1)
    slot = _slots(topi, rank, base, tm)
    slot_s = slot[:, :, :TOP_K].reshape(nt_total, 1, tm * TOP_K)

    pad_lo = (pad_starts + totals)[:N_EXPERTS].astype(I32)
    pad_n = (padded - totals)[:N_EXPERTS].astype(I32)
    xs = _dispatch(pad_lo, pad_n, n_used, slot_s, xrow.reshape(T, 8, LANES), cap, tm, blk)
    ys = _ffn(block_expert, n_used, xs, *ffn_w, blk)
    return _combine(slot_s, gate, h2, final_g, ys, tm)


def kernel(x, meta_tokens, attn_norm_g, w_in, hgrn_norm_g, w_up_attn, w_up_hgrn, w_out,
           hgrn_lb_logits, ffn_norm_g, router_w, router_b, w_gate, b_gate, w_up, b_up,
           w_down, b_down, final_norm_g):
    B, S, D = x.shape
    L = S + N_META
    Lp = -(-L // Q_BLOCK) * Q_BLOCK
    k_sel = min(TOPK_MAX, L // 4)
    meta = jnp.broadcast_to(meta_tokens[None].astype(x.dtype), (B, N_META, D))
    h = jnp.concatenate([meta, x, jnp.zeros((B, Lp - L, D), x.dtype)], axis=1)
    lb = jnp.cumsum(jax.nn.softmax(hgrn_lb_logits.astype(F32), axis=0), axis=0)[0]

    (q, kx, vx, iq, ikf, ikb, hq, hf, hi, hg, ga, gh) = _inproj(h, attn_norm_g[0], _arrange_w_in(w_in[0]))
    att = _attn(q, iq, ikf, ikb, kx, vx, k_sel)
    hgr = _hgrn(hq, hf, hi, hg, lb, hgrn_norm_g[0])

    wr = jnp.pad(router_w[0], ((0, 0), (0, LANES - N_EXPERTS)))
    wr_hi = wr.astype(BF16)
    wr_lo = (wr - wr_hi.astype(F32)).astype(BF16)
    br = jnp.pad(router_b[0], (0, LANES - N_EXPERTS), constant_values=-1e30).reshape(1, LANES)
    (h2, xrow, topi, gate, rank, cnt), tm = _merge(
        att, hgr, ga, gh, h, w_up_attn[0].astype(BF16), w_up_hgrn[0].astype(BF16), w_out[0].astype(BF16),
        ffn_norm_g[0], jnp.concatenate([wr_hi, wr_lo], axis=1), br)

    ffn_w = (w_gate[0], b_gate[0].reshape(N_EXPERTS, 1, D_FF),
             w_up[0], b_up[0].reshape(N_EXPERTS, 1, D_FF),
             w_down[0], b_down[0].reshape(N_EXPERTS, 1, D_MODEL))
    out = _moe(h2, xrow, topi, gate, rank, cnt, tm, ffn_w, final_norm_g)
    return out[:, N_META:L]
```

```python
import functools

import numpy as np
import jax
import jax.numpy as jnp
from jax import lax
from jax.experimental import pallas as pl
from jax.experimental.pallas import tpu as pltpu

F32 = jnp.float32
BF16 = jnp.bfloat16
I32 = jnp.int32

D_MODEL = 1024
N_META = 16
EPS = 1e-5
ROPE_THETA = 500000.0
ATT_HEADS = 8
ATT_KV_HEADS = 2
ATT_HEAD_DIM = 64
ATT_WIDTH = ATT_HEADS * ATT_HEAD_DIM
IDX_HEADS = 8
IDX_DIM = 32
TOPK_MAX = 256
Q_BLOCK = 128
HG_HEADS = 4
HG_DK = 128
HG_DV = 128
HG_WIDTH = HG_HEADS * HG_DV
N_EXPERTS = 32
TOP_K = 4
D_FF = 1024
SWIGLU_LIMIT = 7.0
SWIGLU_ALPHA = 1.702

LANES = 128
VMEM_LIMIT = 56 * 1024 * 1024
INT_MIN = -2 ** 31
NEG_INF = float("-inf")
LOG2E = 1.4426950408889634

HG_CHUNK = 128
FFN_BLOCK = 256


def _pick_tile(n, candidates):
    for c in candidates:
        if n % c == 0:
            return c
    raise ValueError(f"no tile for {n}")


def _dot(a, b):
    return jnp.dot(a, b, preferred_element_type=F32)


def _dot_nt(a, b):
    return lax.dot_general(a, b, (((1,), (1,)), ((), ())), preferred_element_type=F32)


def _dot_tn(a, b):
    return lax.dot_general(a, b, (((0,), (0,)), ((), ())), preferred_element_type=F32)


def _lane_iota(shape):
    return lax.broadcasted_iota(I32, shape, len(shape) - 1)


def _rope_tables(n_pos, head_dim, lane_valid):
    rot = head_dim // 4
    half = rot // 2
    inv = ROPE_THETA ** (-jnp.arange(half, dtype=F32) * 2.0 / rot)
    ang = jnp.arange(n_pos).astype(F32)[:, None] * inv[None, :]
    c = jnp.cos(ang)
    s = jnp.sin(ang)
    lane = np.arange(LANES)
    within = lane % head_dim
    valid = lane < lane_valid
    first = (within < half) & valid
    second = (within >= half) & (within < rot) & valid
    idx = np.where(within < half, within, within - half) % half
    cc = c[:, idx]
    ss = s[:, idx]
    tc = jnp.where((first | second)[None, :], cc, 1.0)
    t1 = jnp.where(first[None, :], -ss, 0.0)
    t2 = jnp.where(second[None, :], ss, 0.0)
    return jnp.stack([tc, t1, t2]).astype(F32), half


def _rope_apply(x, tab_ref, half):
    c = tab_ref[0]
    s1 = tab_ref[1]
    s2 = tab_ref[2]
    cols = []
    for j in range(x.shape[1] // LANES):
        xc = x[:, j * LANES:(j + 1) * LANES]
        cols.append(xc * c + pltpu.roll(xc, LANES - half, 1) * s1 + pltpu.roll(xc, half, 1) * s2)
    return cols[0] if len(cols) == 1 else jnp.concatenate(cols, axis=1)


_W_Q = (0, 512)
_W_KX = (512, 768)
_W_VX = (768, 1024)
_W_IQ = (1024, 1280)
_W_IK = (1280, 1408)
_W_HQ = (1408, 1920)
_W_HF = (1920, 2432)
_W_HI = (2432, 2944)
_W_HG = (2944, 3456)
_W_GA = (3456, 4480)
_W_GH = (4480, 5504)
_W_COLS = 5504


def _arrange_w_in(w_in):
    o = np.cumsum([0, 512, 128, 128, 256, 32, 8, 512, 512, 512, 512, 1024, 1024])
    seg = [w_in[:, o[i]:o[i + 1]] for i in range(12)]
    q, k, v, iq, ik, iw, hq, hf, hi, hg, ga, gh = seg
    z64 = jnp.zeros((D_MODEL, 64), w_in.dtype)
    z88 = jnp.zeros((D_MODEL, 88), w_in.dtype)
    kx = jnp.concatenate([k[:, :64], z64, k[:, 64:], z64], axis=1)
    vx = jnp.concatenate([v[:, :64], z64, v[:, 64:], z64], axis=1)
    ikx = jnp.concatenate([ik, iw, z88], axis=1)
    w = jnp.concatenate([q, kx, vx, iq, ikx, hq, hf, hi, hg, ga, gh], axis=1)
    assert w.shape[1] == _W_COLS
    return w.astype(BF16)


def _inproj_kernel(h_ref, g_ref, w_ref, tatt_ref, tkx_ref, tiq_ref, tik_ref,
                   q_ref, kx_ref, vx_ref, iq_ref, ikf_ref, ikb_ref,
                   hq_ref, hf_ref, hi_ref, hg_ref, ga_ref, gh_ref, *, half_att, half_idx):
    x = h_ref[0]
    xn = x * lax.rsqrt(jnp.mean(x * x, axis=-1, keepdims=True) + EPS) * g_ref[...]
    xb = xn.astype(BF16)

    def proj(rng):
        return _dot(xb, w_ref[:, rng[0]:rng[1]])

    q_ref[0] = _rope_apply(proj(_W_Q), tatt_ref, half_att)
    kx = _rope_apply(proj(_W_KX), tkx_ref, half_att).astype(BF16)
    kx_ref[0, 0] = kx[:, :LANES]
    kx_ref[0, 1] = kx[:, LANES:]
    vx = proj(_W_VX)
    ones_hi = jnp.where(_lane_iota(vx.shape) % LANES >= ATT_HEAD_DIM, 1.0, 0.0)
    vx = (vx + ones_hi).astype(BF16)
    vx_ref[0, 0] = vx[:, :LANES]
    vx_ref[0, 1] = vx[:, LANES:]
    iq_ref[0] = _rope_apply(proj(_W_IQ), tiq_ref, half_idx)
    ik = _rope_apply(proj(_W_IK), tik_ref, half_idx)
    ikf_ref[0] = ik
    ikb_ref[0] = ik.astype(BF16)
    hq_ref[0] = proj(_W_HQ)
    hf_ref[0] = proj(_W_HF)
    hi_ref[0] = proj(_W_HI)
    hg_ref[0] = proj(_W_HG)
    ga_ref[0] = proj(_W_GA)
    gh_ref[0] = proj(_W_GH)


def _inproj(h, g, w_arr):
    B, Lp, D = h.shape
    tm = _pick_tile(Lp, (320, 256, 128))
    tatt, half_att = _rope_tables(Lp, ATT_HEAD_DIM, LANES)
    tkx, _ = _rope_tables(Lp, ATT_HEAD_DIM, ATT_HEAD_DIM)
    tiq, half_idx = _rope_tables(Lp, IDX_DIM, LANES)
    tik, _ = _rope_tables(Lp, IDX_DIM, IDX_DIM)

    def rows(width, dtype=F32):
        return (jax.ShapeDtypeStruct((B, Lp, width), dtype),
                pl.BlockSpec((1, tm, width), lambda b, i: (b, i, 0)))

    def kv(dtype=BF16):
        return (jax.ShapeDtypeStruct((B, ATT_KV_HEADS, Lp, LANES), dtype),
                pl.BlockSpec((1, ATT_KV_HEADS, tm, LANES), lambda b, i: (b, 0, i, 0)))

    outs = [rows(512), kv(), kv(), rows(256), rows(128), rows(128, BF16),
            rows(512), rows(512), rows(512), rows(512), rows(1024), rows(1024)]
    tab_spec = pl.BlockSpec((3, tm, LANES), lambda b, i: (0, i, 0))
    return pl.pallas_call(
        functools.partial(_inproj_kernel, half_att=half_att, half_idx=half_idx),
        grid=(B, Lp // tm),
        in_specs=[pl.BlockSpec((1, tm, D), lambda b, i: (b, i, 0)),
                  pl.BlockSpec((1, D), lambda b, i: (0, 0)),
                  pl.BlockSpec((D, _W_COLS), lambda b, i: (0, 0)),
                  tab_spec, tab_spec, tab_spec, tab_spec],
        out_specs=[o[1] for o in outs],
        out_shape=[o[0] for o in outs],
        compiler_params=pltpu.CompilerParams(
            dimension_semantics=("arbitrary", "arbitrary"), vmem_limit_bytes=VMEM_LIMIT),
        name="inproj",
    )(h, g.reshape(1, D), w_arr, tatt, tkx, tiq, tik)


def _attn_kernel(q_ref, iq_ref, ikf_ref, ikb_ref, kx_ref, vt_ref, ltri_ref, o_ref,
                 keys_ref, acc_ref, m_ref, bias_ref, lg_ref, p_ref, *, C, k_sel):
    j = pl.program_id(1)
    nck = ((j + 1) * Q_BLOCK + C - 1) // C
    lane = _lane_iota((Q_BLOCK, LANES))
    qpos = j * Q_BLOCK + lax.broadcasted_iota(I32, (1, Q_BLOCK), 1)
    nsub = C // 8
    nacc = 8

    def pairs(heads):
        return [jnp.concatenate([heads[2 * p], heads[2 * p + 1]], axis=0) for p in range(len(heads) // 2)]

    iqf = iq_ref[0]
    per_blk = LANES // IDX_DIM
    iqh = []
    for h in range(IDX_HEADS):
        blk = iqf[:, (h // per_blk) * LANES:(h // per_blk + 1) * LANES]
        off = (h % per_blk) * IDX_DIM
        if off:
            blk = pltpu.roll(blk, LANES - off, 1)
        iqh.append(jnp.where(lane < IDX_DIM, blk, 0.0).astype(BF16))
    iqpair = pairs(iqh)
    wt = ikf_ref[0].T
    wrow = [wt[IDX_DIM + h:IDX_DIM + h + 1, :] for h in range(IDX_HEADS)]

    def causal_mask(c):
        kpos = c * C + lax.broadcasted_iota(I32, (C, 1), 0)
        return kpos <= qpos

    def phase_a(c, carry):
        ks = ikb_ref[0, pl.ds(pl.multiple_of(c * C, C), C), :]
        s = jnp.zeros((C, Q_BLOCK), F32)
        for p in range(IDX_HEADS // 2):
            d2 = _dot_nt(ks, iqpair[p])
            s = (s + jnp.maximum(d2[:, :LANES], 0.0) * wrow[2 * p]
                 + jnp.maximum(d2[:, LANES:], 0.0) * wrow[2 * p + 1])
        bits = pltpu.bitcast(s, I32)
        key = jnp.where(bits < 0, bits ^ jnp.int32(0x7FFFFFFF), bits)
        keys_ref[c] = jnp.where(causal_mask(c), key, jnp.int32(INT_MIN))
        return carry

    lax.fori_loop(0, nck, phase_a, 0)

    def count(cand, strict):
        def body(c, acc):
            kc = keys_ref[c]
            hit = (kc > cand) if strict else (kc >= cand)
            return acc + jnp.sum(jnp.where(hit, 1, 0).reshape(nacc, nsub // nacc, 8, LANES), axis=1)

        acc = lax.fori_loop(0, nck, body, jnp.zeros((nacc, 8, LANES), I32))
        return jnp.sum(jnp.sum(acc, axis=0), axis=0, keepdims=True)

    def search(i, carry):
        prefix, nge = carry
        bit = lax.shift_left(jnp.int32(1), jnp.int32(31) - i)
        cand = prefix ^ bit
        cnt = count(cand, False)
        ok = cnt >= k_sel
        return jnp.where(ok, cand, prefix), jnp.where(ok, cnt, nge)

    thr, nge = lax.fori_loop(
        0, 32, search,
        (jnp.full((1, Q_BLOCK), INT_MIN, I32), jnp.zeros((1, Q_BLOCK), I32)))
    excess = jnp.where((nge > k_sel) & (thr != INT_MIN), 1, 0)
    any_excess = jnp.max(excess) > 0
    need = lax.cond(any_excess,
                    lambda: (k_sel - count(thr, True)).astype(F32),
                    lambda: jnp.zeros((1, Q_BLOCK), F32))
    thr_lo = jnp.maximum(thr, jnp.int32(INT_MIN + 1))

    qf = q_ref[0]
    qh = []
    for h in range(ATT_HEADS):
        blk = qf[:, (h // 2) * LANES:(h // 2 + 1) * LANES]
        if h % 2:
            blk = pltpu.roll(blk, ATT_HEAD_DIM, 1)
        qh.append((jnp.where(lane < ATT_HEAD_DIM, blk, 0.0) * (LOG2E * ATT_HEAD_DIM ** -0.5)).astype(BF16))
    qpair = pairs(qh)
    kv_of_pair = [(2 * p) // (ATT_HEADS // ATT_KV_HEADS) for p in range(ATT_HEADS // 2)]

    acc_ref[...] = jnp.zeros(acc_ref.shape, F32)
    m_ref[...] = jnp.full(m_ref.shape, NEG_INF, F32)
    p_ref[...] = jnp.zeros(p_ref.shape, BF16)
    npair = ATT_HEADS // 2

    def to_mask(c, eq_seen):
        kc = keys_ref[c]

        def bias_simple():
            return jnp.where(kc >= thr_lo, 0.0, NEG_INF), eq_seen

        def bias_ties():
            eq = (kc == thr) & causal_mask(c)
            eqf = jnp.where(eq, 1.0, 0.0)
            rank = eq_seen + _dot(ltri_ref[...], eqf.astype(BF16))
            take = (kc > thr) | (eq & (rank < need))
            return jnp.where(take, 0.0, NEG_INF), eq_seen + jnp.sum(eqf, axis=0, keepdims=True)

        bias, eq_next = lax.cond(any_excess, bias_ties, bias_simple)
        keys_ref[c] = pltpu.bitcast(bias, I32)
        return eq_next

    lax.fori_loop(0, nck, to_mask, jnp.zeros((1, Q_BLOCK), F32))

    def logits(c_any, slot):
        c = jnp.minimum(c_any, nck - 1)
        bias = pltpu.bitcast(keys_ref[c], F32) + jnp.where(c_any < nck, 0.0, NEG_INF)
        bias_ref[slot] = bias
        start = pl.multiple_of(c * C, C)
        for p in range(npair):
            lg2 = _dot_nt(kx_ref[0, kv_of_pair[p], pl.ds(start, C), :], qpair[p])
            lg_ref[slot * ATT_HEADS + 2 * p] = lg2[:, :LANES] + bias_ref[slot]
            lg_ref[slot * ATT_HEADS + 2 * p + 1] = lg2[:, LANES:] + bias_ref[slot]

    def weighted_values(c_any, slot, alpha):
        c = jnp.clip(c_any, 0, nck - 1)
        for p in range(npair):
            o2 = _dot(vt_ref[0, kv_of_pair[p], c], p_ref[slot * npair + p])
            for s in range(2):
                h = 2 * p + s
                acc_ref[h] = alpha[h:h + 1] * acc_ref[h] + o2[:, s * LANES:(s + 1) * LANES]

    def softmax(slot):
        shifts, alphas = [], []
        for h in range(ATT_HEADS):
            m_old = m_ref[h]
            mc = jnp.max(lg_ref[slot * ATT_HEADS + h].reshape(nsub, 8, LANES), axis=0)
            m_new = jnp.maximum(m_old, jnp.max(mc, axis=0, keepdims=True))
            m_ref[h] = m_new
            m_safe = jnp.where(m_new == NEG_INF, 0.0, m_new)
            shifts.append(m_safe[:1])
            alphas.append(jnp.exp2(m_old - m_safe)[:1])
        for h in range(ATT_HEADS):
            p_ref[slot * npair + h // 2, :, (h % 2) * LANES:(h % 2 + 1) * LANES] = (
                jnp.exp2(lg_ref[slot * ATT_HEADS + h] - shifts[h]).astype(BF16))
        return jnp.concatenate(alphas, axis=0)

    def step(c, cur, alpha_prev):
        nxt = 1 - cur
        weighted_values(c - 1, nxt, alpha_prev)
        logits(c + 1, nxt)
        return softmax(cur)

    def phase_c(i, alpha_prev):
        return step(2 * i + 1, 1, step(2 * i, 0, alpha_prev))

    logits(0, 0)
    alpha_last = lax.fori_loop(0, (nck + 1) // 2, phase_c, jnp.ones((ATT_HEADS, Q_BLOCK), F32))
    weighted_values(2 * ((nck + 1) // 2) - 1, 1, alpha_last)

    for p in range(ATT_HEADS // 2):
        a = acc_ref[2 * p]
        b = acc_ref[2 * p + 1]
        ot = jnp.concatenate([a[:ATT_HEAD_DIM] / a[ATT_HEAD_DIM:ATT_HEAD_DIM + 1],
                              b[:ATT_HEAD_DIM] / b[ATT_HEAD_DIM:ATT_HEAD_DIM + 1]], axis=0)
        o_ref[0, :, p * LANES:(p + 1) * LANES] = ot.T


def _attn(q, iq, ikf, ikb, kx, vx, k_sel):
    B, Lp, _ = q.shape
    C = _pick_tile(Lp, (640, 512, 256, 128))
    nch = Lp // C
    ltri = jnp.asarray(np.tril(np.ones((C, C), np.float32), -1), BF16)
    vt = vx.reshape(B, ATT_KV_HEADS, nch, C, LANES).transpose(0, 1, 2, 4, 3)
    qspec = lambda w: pl.BlockSpec((1, Q_BLOCK, w), lambda b, j: (b, j, 0))
    return pl.pallas_call(
        functools.partial(_attn_kernel, C=C, k_sel=k_sel),
        grid=(B, Lp // Q_BLOCK),
        in_specs=[qspec(512), qspec(256), qspec(128),
                  pl.BlockSpec((1, Lp, LANES), lambda b, j: (b, 0, 0)),
                  pl.BlockSpec((1, ATT_KV_HEADS, Lp, LANES), lambda b, j: (b, 0, 0, 0)),
                  pl.BlockSpec((1, ATT_KV_HEADS, nch, LANES, C), lambda b, j: (b, 0, 0, 0, 0)),
                  pl.BlockSpec((C, C), lambda b, j: (0, 0))],
        out_specs=qspec(512),
        out_shape=jax.ShapeDtypeStruct((B, Lp, ATT_WIDTH), F32),
        scratch_shapes=[pltpu.VMEM((nch, C, Q_BLOCK), I32),
                        pltpu.VMEM((ATT_HEADS, LANES, Q_BLOCK), F32),
                        pltpu.VMEM((ATT_HEADS, 8, Q_BLOCK), F32),
                        pltpu.VMEM((2, C, Q_BLOCK), F32),
                        pltpu.VMEM((2 * ATT_HEADS, C, Q_BLOCK), F32),
                        pltpu.VMEM((2 * (ATT_HEADS // 2), C, 2 * Q_BLOCK), BF16)],
        compiler_params=pltpu.CompilerParams(
            dimension_semantics=("arbitrary", "arbitrary"), vmem_limit_bytes=VMEM_LIMIT),
        name="attn",
    )(q, iq, ikf, ikb, kx, vt, ltri)


def _hgrn_sum_matrix(C):
    nlev = int(np.log2(C))
    t = np.arange(C)[:, None]
    jj = np.arange(C)[None, :]
    mats = [(jj <= t), (jj > t)]
    for l in range(nlev):
        s = 1 << l
        mid = ((t >> (l + 1)) << (l + 1)) + s - 1
        odd = ((t >> l) & 1) == 1
        mats.append(np.where(odd, (jj > mid) & (jj <= t), (jj > t) & (jj <= mid)))
    return np.concatenate(mats, axis=0).astype(np.float32), nlev


def _hgrn_kernel(hq_ref, hf_ref, hi_ref, hg_ref, lb_ref, gn_ref, e_ref, o_ref, st_ref, *, C, nlev):
    @pl.when(pl.program_id(1) == 0)
    def _():
        st_ref[...] = jnp.zeros(st_ref.shape, F32)

    for h in range(HG_HEADS):
        _hgrn_head(hq_ref, hf_ref, hi_ref, hg_ref, lb_ref, gn_ref, e_ref, o_ref, st_ref, h, C, nlev)


def _hgrn_head(hq_ref, hf_ref, hi_ref, hg_ref, lb_ref, gn_ref, e_ref, o_ref, st_ref, h, C, nlev):
    cols = slice(h * HG_DK, (h + 1) * HG_DK)
    lb = lb_ref[:, cols]
    f = lb + (1.0 - lb) * jax.nn.sigmoid(hf_ref[0, :, cols])
    logf = jnp.log(f)
    kk = 1.0 - f
    q = hq_ref[0, :, cols] * (HG_DK ** -0.5)
    v = hi_ref[0, :, cols].astype(BF16)

    p1 = logf.astype(BF16)
    r1 = logf - p1.astype(F32)
    p2 = r1.astype(BF16)
    p3 = (r1 - p2.astype(F32)).astype(BF16)
    x3 = _dot(e_ref[...], jnp.concatenate([p1, p2, p3], axis=1))
    xs = x3[:, :LANES] + x3[:, LANES:2 * LANES] + x3[:, 2 * LANES:]

    b = xs[0:C]
    suffix = xs[C:2 * C]
    rt = lax.broadcasted_iota(I32, (C, C), 0)
    ct = lax.broadcasted_iota(I32, (C, C), 1)
    row = lax.broadcasted_iota(I32, (C, 1), 0)

    scores = jnp.where(rt == ct, _dot_nt(q.astype(BF16), kk.astype(BF16)), 0.0)
    for l in range(nlev):
        w = jnp.exp(xs[(2 + l) * C:(3 + l) * C])
        odd = ((row >> l) & 1) == 1
        a = (jnp.where(odd, q, kk) * w).astype(BF16)
        pair = (((rt >> l) & 1) == 1) & ((ct >> l) == (rt >> l) - 1)
        scores = scores + jnp.where(pair, _dot_nt(a, a), 0.0)

    st = st_ref[h]
    qb = (q * jnp.exp(b)).astype(BF16)
    o = _dot(scores.astype(BF16), v) + _dot_nt(qb, st.astype(BF16))
    ks = (kk * jnp.exp(suffix)).astype(BF16)
    st_ref[h] = jnp.exp(b[C - 1:C, :]) * st + _dot_tn(v, ks)

    o = o * lax.rsqrt(jnp.mean(o * o, axis=-1, keepdims=True) + EPS) * gn_ref[:, cols]
    hg = hg_ref[0, :, cols]
    o_ref[0, :, cols] = o * (hg * jax.nn.sigmoid(hg))


def _hgrn(hq, hf, hi, hg, lb, gn):
    B, Lp, _ = hq.shape
    C = HG_CHUNK
    e_np, nlev = _hgrn_sum_matrix(C)
    e = jnp.asarray(e_np, BF16)
    blk = pl.BlockSpec((1, C, HG_WIDTH), lambda b, c: (b, c, 0))
    vec = pl.BlockSpec((1, HG_WIDTH), lambda b, c: (0, 0))
    return pl.pallas_call(
        functools.partial(_hgrn_kernel, C=C, nlev=nlev),
        grid=(B, Lp // C),
        in_specs=[blk, blk, blk, blk, vec, vec,
                  pl.BlockSpec(e.shape, lambda b, c: (0, 0))],
        out_specs=blk,
        out_shape=jax.ShapeDtypeStruct((B, Lp, HG_WIDTH), F32),
        scratch_shapes=[pltpu.VMEM((HG_HEADS, HG_DV, HG_DK), F32)],
        compiler_params=pltpu.CompilerParams(
            dimension_semantics=("arbitrary", "arbitrary"), vmem_limit_bytes=VMEM_LIMIT),
        name="hgrn",
    )(hq, hf, hi, hg, lb.reshape(1, -1), gn.reshape(1, -1), e)


def _merge_kernel(att_ref, hgr_ref, ga_ref, gh_ref, h_ref, wa_ref, wh_ref, wo_ref, g_ref,
                  wr_ref, br_ref, lt_ref,
                  h2_ref, xrow_ref, topi_ref, gate_ref, rank_ref, cnt_ref, *, tm):
    mix = (jax.nn.sigmoid(ga_ref[0]) * _dot(att_ref[0].astype(BF16), wa_ref[...])
           + jax.nn.sigmoid(gh_ref[0]) * _dot(hgr_ref[0].astype(BF16), wh_ref[...]))
    h2 = h_ref[0] + _dot(mix.astype(BF16), wo_ref[...])
    h2_ref[0] = h2
    xn = h2 * lax.rsqrt(jnp.mean(h2 * h2, axis=-1, keepdims=True) + EPS) * g_ref[...]
    for k in range(D_MODEL // LANES):
        xrow_ref[pl.ds(k, tm, stride=8), :] = xn[:, k * LANES:(k + 1) * LANES]

    xh = xn.astype(BF16)
    xl = (xn - xh.astype(F32)).astype(BF16)
    l1 = _dot(xh, wr_ref[...])
    logits = l1[:, :LANES] + l1[:, LANES:] + _dot(xl, wr_ref[:, :LANES]) + br_ref[...]

    lane = _lane_iota((tm, LANES))
    lanef = lane.astype(F32)
    work = logits
    vals, sels, idxs = [], [], []
    for _ in range(TOP_K):
        m = jnp.max(work, axis=1, keepdims=True)
        idx = jnp.min(jnp.where(work == m, lanef, float(LANES)), axis=1, keepdims=True)
        sel = lanef == idx
        vals.append(m)
        idxs.append(idx)
        sels.append(sel)
        work = jnp.where(sel, NEG_INF, work)
    es = [jnp.exp(v - vals[0]) for v in vals]
    den = es[0] + es[1] + es[2] + es[3]

    member = jnp.zeros((tm, LANES), F32)
    for sel in sels:
        member = member + jnp.where(sel, 1.0, 0.0)
    before = _dot(lt_ref[...], member.astype(BF16))
    topi = jnp.zeros((tm, LANES), F32)
    gate = jnp.zeros((tm, LANES), F32)
    rank = jnp.zeros((tm, LANES), F32)
    for jx in range(TOP_K):
        here = lane == jx
        topi = jnp.where(here, idxs[jx], topi)
        gate = jnp.where(here, es[jx] / den, gate)
        rank = jnp.where(here, jnp.sum(jnp.where(sels[jx], before, 0.0), axis=1, keepdims=True), rank)
    topi_ref[0] = topi.astype(I32)
    gate_ref[0] = gate
    rank_ref[0] = rank.astype(I32)
    cnt_ref[0] = jnp.sum(member, axis=0, keepdims=True).astype(I32)


def _merge(att, hgr, ga, gh, h, wa, wh, wo, g, wr2, br):
    B, Lp, D = h.shape
    tm = _pick_tile(Lp, (320, 256, 128))
    nt = Lp // tm
    lt = jnp.asarray(np.tril(np.ones((tm, tm), np.float32), -1), BF16)
    row = lambda w: pl.BlockSpec((1, tm, w), lambda b, i: (b, i, 0))
    full = lambda a: pl.BlockSpec(a.shape, lambda b, i: (0,) * a.ndim)
    g2 = g.reshape(1, D)
    out_shape = [jax.ShapeDtypeStruct((B, Lp, D), F32),
                 jax.ShapeDtypeStruct((B * Lp * 8, LANES), F32),
                 jax.ShapeDtypeStruct((B, Lp, LANES), I32),
                 jax.ShapeDtypeStruct((B, Lp, LANES), F32),
                 jax.ShapeDtypeStruct((B, Lp, LANES), I32),
                 jax.ShapeDtypeStruct((B * nt, 1, LANES), I32)]
    out_specs = [row(D),
                 pl.BlockSpec((tm * 8, LANES), lambda b, i: (b * nt + i, 0)),
                 row(LANES), row(LANES), row(LANES),
                 pl.BlockSpec((1, 1, LANES), lambda b, i: (b * nt + i, 0, 0))]
    return pl.pallas_call(
        functools.partial(_merge_kernel, tm=tm),
        grid=(B, nt),
        in_specs=[row(512), row(512), row(D), row(D), row(D),
                  full(wa), full(wh), full(wo), full(g2), full(wr2), full(br), full(lt)],
        out_specs=out_specs,
        out_shape=out_shape,
        compiler_params=pltpu.CompilerParams(
            dimension_semantics=("arbitrary", "arbitrary"), vmem_limit_bytes=VMEM_LIMIT),
        name="merge",
    )(att, hgr, ga, gh, h, wa, wh, wo, g2, wr2, br, lt), tm


def _slots_kernel(topi_ref, rank_ref, base_ref, slot_ref):
    topi = topi_ref[0]
    rank = rank_ref[0]
    base = base_ref[0]
    lane = _lane_iota(topi.shape)
    slot = jnp.zeros(topi.shape, I32)
    for jx in range(TOP_K):
        mine = lane == topi[:, jx:jx + 1]
        first = jnp.sum(jnp.where(mine, base, 0), axis=1, keepdims=True)
        slot = jnp.where(lane == jx, first + rank[:, jx:jx + 1], slot)
    slot_ref[0] = slot


def _slots(topi, rank, base, tm):
    B, Lp, _ = topi.shape
    nt = Lp // tm
    row = pl.BlockSpec((1, tm, LANES), lambda b, i: (b, i, 0))
    return pl.pallas_call(
        _slots_kernel,
        grid=(B, nt),
        in_specs=[row, row, pl.BlockSpec((1, 1, LANES), lambda b, i: (b * nt + i, 0, 0))],
        out_specs=row,
        out_shape=jax.ShapeDtypeStruct((B, Lp, LANES), I32),
        compiler_params=pltpu.CompilerParams(dimension_semantics=("arbitrary", "arbitrary")),
        name="slots",
    )(topi, rank, base)


def _dispatch_kernel(pad_lo_ref, pad_n_ref, nu_ref, slot_ref, x_ref, xs_hbm, zblk_ref, sem, *, tm, blk, nblk):
    @pl.when(pl.program_id(0) == 0)
    def _():
        zblk_ref[...] = jnp.zeros(zblk_ref.shape, F32)

        def zblock(b):
            return pltpu.make_async_copy(zblk_ref, xs_hbm.at[pl.ds(b * blk, blk)], sem)

        def bissue(b, c):
            zblock(b).start()
            return c

        def bdrain(b, c):
            zblock(b).wait()
            return c

        lax.fori_loop(nu_ref[0], nblk, bissue, 0)
        lax.fori_loop(nu_ref[0], nblk, bdrain, 0)

        def per_expert(e, c):
            def zcopy(r):
                return pltpu.make_async_copy(zblk_ref.at[0], xs_hbm.at[pad_lo_ref[e] + r], sem)

            def zissue(r, c2):
                zcopy(r).start()
                return c2

            def zdrain(r, c2):
                zcopy(r).wait()
                return c2

            lax.fori_loop(0, pad_n_ref[e], zissue, 0)
            lax.fori_loop(0, pad_n_ref[e], zdrain, 0)
            return c

        lax.fori_loop(0, N_EXPERTS, per_expert, 0)

    def copy(t, jx):
        return pltpu.make_async_copy(x_ref.at[t], xs_hbm.at[slot_ref[0, 0, t * TOP_K + jx]], sem)

    def issue(t, c):
        for jx in range(TOP_K):
            copy(t, jx).start()
        return c

    def drain(t, c):
        for jx in range(TOP_K):
            copy(t, jx).wait()
        return c

    lax.fori_loop(0, tm, issue, 0)
    lax.fori_loop(0, tm, drain, 0)


def _dispatch(pad_lo, pad_n, n_used, slot_s, xrow, cap, tm, blk):
    nt = slot_s.shape[0]
    return pl.pallas_call(
        functools.partial(_dispatch_kernel, tm=tm, blk=blk, nblk=cap // blk),
        grid_spec=pltpu.PrefetchScalarGridSpec(
            num_scalar_prefetch=3, grid=(nt,),
            in_specs=[pl.BlockSpec((1, 1, tm * TOP_K), lambda i, lo, n, nu: (i, 0, 0), memory_space=pltpu.SMEM),
                      pl.BlockSpec((tm, 8, LANES), lambda i, lo, n, nu: (i, 0, 0))],
            out_specs=pl.BlockSpec(memory_space=pl.ANY),
            scratch_shapes=[pltpu.VMEM((blk, 8, LANES), F32), pltpu.SemaphoreType.DMA(())]),
        out_shape=jax.ShapeDtypeStruct((cap, 8, LANES), F32),
        compiler_params=pltpu.CompilerParams(dimension_semantics=("arbitrary",)),
        name="dispatch",
    )(pad_lo, pad_n, n_used, slot_s, xrow)


def _ffn_kernel(be_ref, nu_ref, x_ref, wg_ref, bg_ref, wu_ref, bu_ref, wd_ref, bd_ref, o_ref,
                wgb_ref, wub_ref, wdb_ref, *, blk):
    i = pl.program_id(0)
    nk = D_MODEL // LANES
    used = i < nu_ref[0]
    new_expert = (i == 0) | (be_ref[i] != be_ref[jnp.maximum(i - 1, 0)])

    @pl.when(used & new_expert)
    def _():
        rows = 128

        def cast(r, c):
            sl = pl.ds(pl.multiple_of(r * rows, rows), rows)
            wgb_ref[sl, :] = wg_ref[0, sl, :].astype(BF16)
            wub_ref[sl, :] = wu_ref[0, sl, :].astype(BF16)
            wdb_ref[sl, :] = wd_ref[0, sl, :].astype(BF16)
            return c

        lax.fori_loop(0, D_MODEL // rows, cast, 0)

    @pl.when(used)
    def _():
        xb = jnp.concatenate([x_ref[pl.ds(k, blk, stride=8), :] for k in range(nk)], axis=1).astype(BF16)
        g = _dot(xb, wgb_ref[...]) + bg_ref[0]
        u = _dot(xb, wub_ref[...]) + bu_ref[0]
        g = jnp.minimum(g, SWIGLU_LIMIT)
        u = jnp.clip(u, -SWIGLU_LIMIT, SWIGLU_LIMIT)
        act = (u + 1.0) * (g * jax.nn.sigmoid(SWIGLU_ALPHA * g))
        out = _dot(act.astype(BF16), wdb_ref[...]) + bd_ref[0]
        for k in range(nk):
            o_ref[pl.ds(k, blk, stride=8), :] = out[:, k * LANES:(k + 1) * LANES]

    @pl.when(i >= nu_ref[0])
    def _():
        o_ref[...] = jnp.zeros(o_ref.shape, F32)


def _ffn(block_expert, n_used, xs, wg, bg, wu, bu, wd, bd, blk):
    cap = xs.shape[0]
    nblk = cap // blk
    x2 = xs.reshape(cap * 8, LANES)
    wspec = pl.BlockSpec((1, D_MODEL, D_FF), lambda i, be, nu: (be[i], 0, 0))
    wdspec = pl.BlockSpec((1, D_FF, D_MODEL), lambda i, be, nu: (be[i], 0, 0))
    bspec = lambda w: pl.BlockSpec((1, 1, w), lambda i, be, nu: (be[i], 0, 0))
    rows = pl.BlockSpec((blk * 8, LANES), lambda i, be, nu: (i, 0))
    rows_in = pl.BlockSpec((blk * 8, LANES), lambda i, be, nu: (jnp.minimum(i, nu[0] - 1), 0))
    assert D_FF == D_MODEL
    wscratch = pltpu.VMEM((D_MODEL, D_FF), BF16)
    out = pl.pallas_call(
        functools.partial(_ffn_kernel, blk=blk),
        grid_spec=pltpu.PrefetchScalarGridSpec(
            num_scalar_prefetch=2, grid=(nblk,),
            in_specs=[rows_in, wspec, bspec(D_FF), wspec, bspec(D_FF), wdspec, bspec(D_MODEL)],
            out_specs=rows,
            scratch_shapes=[wscratch, wscratch, wscratch]),
        out_shape=jax.ShapeDtypeStruct((cap * 8, LANES), F32),
        compiler_params=pltpu.CompilerParams(
            dimension_semantics=("arbitrary",), vmem_limit_bytes=VMEM_LIMIT),
        name="ffn",
    )(block_expert, n_used, x2, wg, bg, wu, bu, wd, bd)
    return out.reshape(cap, 8, LANES)


def _combine_kernel(slot_ref, gate_ref, h2_ref, g_ref, ys_hbm, o_ref, buf_ref, sem, *, tm):
    def copy(t, jx):
        dst = buf_ref.at[pl.ds(pl.multiple_of((jx * tm + t) * 8, 8), 8), :]
        return pltpu.make_async_copy(ys_hbm.at[slot_ref[0, 0, t * TOP_K + jx]], dst, sem)

    def issue(t, c):
        for jx in range(TOP_K):
            copy(t, jx).start()
        return c

    def drain(t, c):
        for jx in range(TOP_K):
            copy(t, jx).wait()
        return c

    lax.fori_loop(0, tm, issue, 0)
    lax.fori_loop(0, tm, drain, 0)

    gate = gate_ref[0]
    h2 = h2_ref[0]
    cols = []
    for k in range(D_MODEL // LANES):
        y = h2[:, k * LANES:(k + 1) * LANES]
        for jx in range(TOP_K):
            y = y + gate[:, jx:jx + 1] * buf_ref[pl.ds(jx * tm * 8 + k, tm, stride=8), :]
        cols.append(y)
    ho = jnp.concatenate(cols, axis=1)
    o_ref[0] = ho * lax.rsqrt(jnp.mean(ho * ho, axis=-1, keepdims=True) + EPS) * g_ref[...]


def _combine(slot_s, gate, h2, g, ys, tm):
    B, Lp, D = h2.shape
    nt = Lp // tm
    row = lambda w: pl.BlockSpec((1, tm, w), lambda b, i: (b, i, 0))
    return pl.pallas_call(
        functools.partial(_combine_kernel, tm=tm),
        grid=(B, nt),
        in_specs=[pl.BlockSpec((1, 1, tm * TOP_K), lambda b, i: (b * nt + i, 0, 0), memory_space=pltpu.SMEM),
                  row(LANES), row(D),
                  pl.BlockSpec((1, D), lambda b, i: (0, 0)),
                  pl.BlockSpec(memory_space=pl.ANY)],
        out_specs=row(D),
        out_shape=jax.ShapeDtypeStruct((B, Lp, D), F32),
        scratch_shapes=[pltpu.VMEM((TOP_K * tm * 8, LANES), F32), pltpu.SemaphoreType.DMA(())],
        compiler_params=pltpu.CompilerParams(
            dimension_semantics=("arbitrary", "arbitrary"), vmem_limit_bytes=VMEM_LIMIT),
        name="combine",
    )(slot_s, gate, h2, g.reshape(1, D), ys)


def _moe(h2, xrow, topi, gate, rank, cnt, tm, ffn_w, final_g):
    B, Lp, D = h2.shape
    T = B * Lp
    nt_total = cnt.shape[0]
    blk = FFN_BLOCK
    nblk = -(-(T * TOP_K) // blk) + N_EXPERTS
    cap = nblk * blk
    cnt2 = cnt.reshape(nt_total, LANES)
    totals = jnp.sum(cnt2, axis=0)
    padded = (totals + blk - 1) // blk * blk
    pad_ends = jnp.cumsum(padded)
    pad_starts = pad_ends - padded
    base = (pad_starts[None, :] + jnp.cumsum(cnt2, axis=0) - cnt2).astype(I32).reshape(nt_total, 1, LANES)
    block_start = (jnp.arange(nblk) * blk)[:, None]
    block_expert = jnp.minimum(
        jnp.sum(pad_ends[None, :N_EXPERTS] <= block_start, axis=1), N_EXPERTS - 1).astype(I32)
    n_used = (pad_ends[N_EXPERTS - 1] // blk).astype(I32).reshape(1)
    slot = _slots(topi, rank, base, tm)
    slot_s = slot[:, :, :TOP_K].reshape(nt_total, 1, tm * TOP_K)

    pad_lo = (pad_starts + totals)[:N_EXPERTS].astype(I32)
    pad_n = (padded - totals)[:N_EXPERTS].astype(I32)
    xs = _dispatch(pad_lo, pad_n, n_used, slot_s, xrow.reshape(T, 8, LANES), cap, tm, blk)
    ys = _ffn(block_expert, n_used, xs, *ffn_w, blk)
    return _combine(slot_s, gate, h2, final_g, ys, tm)


def kernel(x, meta_tokens, attn_norm_g, w_in, hgrn_norm_g, w_up_attn, w_up_hgrn, w_out,
           hgrn_lb_logits, ffn_norm_g, router_w, router_b, w_gate, b_gate, w_up, b_up,
           w_down, b_down, final_norm_g):
    B, S, D = x.shape
    L = S + N_META
    Lp = -(-L // Q_BLOCK) * Q_BLOCK
    k_sel = min(TOPK_MAX, L // 4)
    meta = jnp.broadcast_to(meta_tokens[None].astype(x.dtype), (B, N_META, D))
    h = jnp.concatenate([meta, x, jnp.zeros((B, Lp - L, D), x.dtype)], axis=1)
    lb = jnp.cumsum(jax.nn.softmax(hgrn_lb_logits.astype(F32), axis=0), axis=0)[0]

    (q, kx, vx, iq, ikf, ikb, hq, hf, hi, hg, ga, gh) = _inproj(h, attn_norm_g[0], _arrange_w_in(w_in[0]))
    att = _attn(q, iq, ikf, ikb, kx, vx, k_sel)
    hgr = _hgrn(hq, hf, hi, hg, lb, hgrn_norm_g[0])

    wr = jnp.pad(router_w[0], ((0, 0), (0, LANES - N_EXPERTS)))
    wr_hi = wr.astype(BF16)
    wr_lo = (wr - wr_hi.astype(F32)).astype(BF16)
    br = jnp.pad(router_b[0], (0, LANES - N_EXPERTS), constant_values=-1e30).reshape(1, LANES)
    (h2, xrow, topi, gate, rank, cnt), tm = _merge(
        att, hgr, ga, gh, h, w_up_attn[0].astype(BF16), w_up_hgrn[0].astype(BF16), w_out[0].astype(BF16),
        ffn_norm_g[0], jnp.concatenate([wr_hi, wr_lo], axis=1), br)

    ffn_w = (w_gate[0], b_gate[0].reshape(N_EXPERTS, 1, D_FF),
             w_up[0], b_up[0].reshape(N_EXPERTS, 1, D_FF),
             w_down[0], b_down[0].reshape(N_EXPERTS, 1, D_MODEL))
    out = _moe(h2, xrow, topi, gate, rank, cnt, tm, ffn_w, final_norm_g)
    return out[:, N_META:L]
```

```python
import functools

import numpy as np
import jax
import jax.numpy as jnp
from jax import lax
from jax.experimental import pallas as pl
from jax.experimental.pallas import tpu as pltpu

F32 = jnp.float32
BF16 = jnp.bfloat16
I32 = jnp.int32

D_MODEL = 1024
N_META = 16
EPS = 1e-5
ROPE_THETA = 500000.0
ATT_HEADS = 8
ATT_KV_HEADS = 2
ATT_HEAD_DIM = 64
ATT_WIDTH = ATT_HEADS * ATT_HEAD_DIM
IDX_HEADS = 8
IDX_DIM = 32
TOPK_MAX = 256
Q_BLOCK = 128
HG_HEADS = 4
HG_DK = 128
HG_DV = 128
HG_WIDTH = HG_HEADS * HG_DV
N_EXPERTS = 32
TOP_K = 4
D_FF = 1024
SWIGLU_LIMIT = 7.0
SWIGLU_ALPHA = 1.702

LANES = 128
VMEM_LIMIT = 56 * 1024 * 1024
INT_MIN = -2 ** 31
NEG_INF = float("-inf")
LOG2E = 1.4426950408889634

HG_CHUNK = 128
FFN_BLOCK = 256


def _pick_tile(n, candidates):
    for c in candidates:
        if n % c == 0:
            return c
    raise ValueError(f"no tile for {n}")


def _dot(a, b):
    return jnp.dot(a, b, preferred_element_type=F32)


def _dot_nt(a, b):
    return lax.dot_general(a, b, (((1,), (1,)), ((), ())), preferred_element_type=F32)


def _dot_tn(a, b):
    return lax.dot_general(a, b, (((0,), (0,)), ((), ())), preferred_element_type=F32)


def _lane_iota(shape):
    return lax.broadcasted_iota(I32, shape, len(shape) - 1)


def _rope_tables(n_pos, head_dim, lane_valid):
    rot = head_dim // 4
    half = rot // 2
    inv = ROPE_THETA ** (-jnp.arange(half, dtype=F32) * 2.0 / rot)
    ang = jnp.arange(n_pos).astype(F32)[:, None] * inv[None, :]
    c = jnp.cos(ang)
    s = jnp.sin(ang)
    lane = np.arange(LANES)
    within = lane % head_dim
    valid = lane < lane_valid
    first = (within < half) & valid
    second = (within >= half) & (within < rot) & valid
    idx = np.where(within < half, within, within - half) % half
    cc = c[:, idx]
    ss = s[:, idx]
    tc = jnp.where((first | second)[None, :], cc, 1.0)
    t1 = jnp.where(first[None, :], -ss, 0.0)
    t2 = jnp.where(second[None, :], ss, 0.0)
    return jnp.stack([tc, t1, t2]).astype(F32), half


def _rope_apply(x, tab_ref, half):
    c = tab_ref[0]
    s1 = tab_ref[1]
    s2 = tab_ref[2]
    cols = []
    for j in range(x.shape[1] // LANES):
        xc = x[:, j * LANES:(j + 1) * LANES]
        cols.append(xc * c + pltpu.roll(xc, LANES - half, 1) * s1 + pltpu.roll(xc, half, 1) * s2)
    return cols[0] if len(cols) == 1 else jnp.concatenate(cols, axis=1)


_W_Q = (0, 512)
_W_KX = (512, 768)
_W_VX = (768, 1024)
_W_IQ = (1024, 1280)
_W_IK = (1280, 1408)
_W_HQ = (1408, 1920)
_W_HF = (1920, 2432)
_W_HI = (2432, 2944)
_W_HG = (2944, 3456)
_W_GA = (3456, 4480)
_W_GH = (4480, 5504)
_W_COLS = 5504


def _arrange_w_in(w_in):
    o = np.cumsum([0, 512, 128, 128, 256, 32, 8, 512, 512, 512, 512, 1024, 1024])
    seg = [w_in[:, o[i]:o[i + 1]] for i in range(12)]
    q, k, v, iq, ik, iw, hq, hf, hi, hg, ga, gh = seg
    z64 = jnp.zeros((D_MODEL, 64), w_in.dtype)
    z88 = jnp.zeros((D_MODEL, 88), w_in.dtype)
    kx = jnp.concatenate([k[:, :64], z64, k[:, 64:], z64], axis=1)
    vx = jnp.concatenate([v[:, :64], z64, v[:, 64:], z64], axis=1)
    ikx = jnp.concatenate([ik, iw, z88], axis=1)
    w = jnp.concatenate([q, kx, vx, iq, ikx, hq, hf, hi, hg, ga, gh], axis=1)
    assert w.shape[1] == _W_COLS
    return w.astype(BF16)


def _inproj_kernel(h_ref, g_ref, w_ref, tatt_ref, tkx_ref, tiq_ref, tik_ref,
                   q_ref, kx_ref, vx_ref, iq_ref, ikf_ref, ikb_ref,
                   hq_ref, hf_ref, hi_ref, hg_ref, ga_ref, gh_ref, *, half_att, half_idx):
    x = h_ref[0]
    xn = x * lax.rsqrt(jnp.mean(x * x, axis=-1, keepdims=True) + EPS) * g_ref[...]
    xb = xn.astype(BF16)

    def proj(rng):
        return _dot(xb, w_ref[:, rng[0]:rng[1]])

    q_ref[0] = _rope_apply(proj(_W_Q), tatt_ref, half_att)
    kx = _rope_apply(proj(_W_KX), tkx_ref, half_att).astype(BF16)
    kx_ref[0, 0] = kx[:, :LANES]
    kx_ref[0, 1] = kx[:, LANES:]
    vx = proj(_W_VX)
    ones_hi = jnp.where(_lane_iota(vx.shape) % LANES >= ATT_HEAD_DIM, 1.0, 0.0)
    vx = (vx + ones_hi).astype(BF16)
    vx_ref[0, 0] = vx[:, :LANES]
    vx_ref[0, 1] = vx[:, LANES:]
    iq_ref[0] = _rope_apply(proj(_W_IQ), tiq_ref, half_idx)
    ik = _rope_apply(proj(_W_IK), tik_ref, half_idx)
    ikf_ref[0] = ik
    ikb_ref[0] = ik.astype(BF16)
    hq_ref[0] = proj(_W_HQ)
    hf_ref[0] = proj(_W_HF)
    hi_ref[0] = proj(_W_HI)
    hg_ref[0] = proj(_W_HG)
    ga_ref[0] = proj(_W_GA)
    gh_ref[0] = proj(_W_GH)


def _inproj(h, g, w_arr):
    B, Lp, D = h.shape
    tm = _pick_tile(Lp, (320, 256, 128))
    tatt, half_att = _rope_tables(Lp, ATT_HEAD_DIM, LANES)
    tkx, _ = _rope_tables(Lp, ATT_HEAD_DIM, ATT_HEAD_DIM)
    tiq, half_idx = _rope_tables(Lp, IDX_DIM, LANES)
    tik, _ = _rope_tables(Lp, IDX_DIM, IDX_DIM)

    def rows(width, dtype=F32):
        return (jax.ShapeDtypeStruct((B, Lp, width), dtype),
                pl.BlockSpec((1, tm, width), lambda b, i: (b, i, 0)))

    def kv(dtype=BF16):
        return (jax.ShapeDtypeStruct((B, ATT_KV_HEADS, Lp, LANES), dtype),
                pl.BlockSpec((1, ATT_KV_HEADS, tm, LANES), lambda b, i: (b, 0, i, 0)))

    outs = [rows(512), kv(), kv(), rows(256), rows(128), rows(128, BF16),
            rows(512), rows(512), rows(512), rows(512), rows(1024), rows(1024)]
    tab_spec = pl.BlockSpec((3, tm, LANES), lambda b, i: (0, i, 0))
    return pl.pallas_call(
        functools.partial(_inproj_kernel, half_att=half_att, half_idx=half_idx),
        grid=(B, Lp // tm),
        in_specs=[pl.BlockSpec((1, tm, D), lambda b, i: (b, i, 0)),
                  pl.BlockSpec((1, D), lambda b, i: (0, 0)),
                  pl.BlockSpec((D, _W_COLS), lambda b, i: (0, 0)),
                  tab_spec, tab_spec, tab_spec, tab_spec],
        out_specs=[o[1] for o in outs],
        out_shape=[o[0] for o in outs],
        compiler_params=pltpu.CompilerParams(
            dimension_semantics=("arbitrary", "arbitrary"), vmem_limit_bytes=VMEM_LIMIT),
        name="inproj",
    )(h, g.reshape(1, D), w_arr, tatt, tkx, tiq, tik)


def _attn_kernel(q_ref, iq_ref, ikf_ref, ikb_ref, kx_ref, vt_ref, ltri_ref, o_ref,
                 keys_ref, acc_ref, m_ref, bias_ref, lg_ref, p_ref, *, C, k_sel):
    j = pl.program_id(1)
    nck = ((j + 1) * Q_BLOCK + C - 1) // C
    lane = _lane_iota((Q_BLOCK, LANES))
    qpos = j * Q_BLOCK + lax.broadcasted_iota(I32, (1, Q_BLOCK), 1)
    nsub = C // 8
    nacc = 8

    def pairs(heads):
        return [jnp.concatenate([heads[2 * p], heads[2 * p + 1]], axis=0) for p in range(len(heads) // 2)]

    iqf = iq_ref[0]
    per_blk = LANES // IDX_DIM
    iqh = []
    for h in range(IDX_HEADS):
        blk = iqf[:, (h // per_blk) * LANES:(h // per_blk + 1) * LANES]
        off = (h % per_blk) * IDX_DIM
        if off:
            blk = pltpu.roll(blk, LANES - off, 1)
        iqh.append(jnp.where(lane < IDX_DIM, blk, 0.0).astype(BF16))
    iqpair = pairs(iqh)
    wt = ikf_ref[0].T
    wrow = [wt[IDX_DIM + h:IDX_DIM + h + 1, :] for h in range(IDX_HEADS)]

    def causal_mask(c):
        kpos = c * C + lax.broadcasted_iota(I32, (C, 1), 0)
        return kpos <= qpos

    def phase_a(c, carry):
        ks = ikb_ref[0, pl.ds(pl.multiple_of(c * C, C), C), :]
        s = jnp.zeros((C, Q_BLOCK), F32)
        for p in range(IDX_HEADS // 2):
            d2 = _dot_nt(ks, iqpair[p])
            s = (s + jnp.maximum(d2[:, :LANES], 0.0) * wrow[2 * p]
                 + jnp.maximum(d2[:, LANES:], 0.0) * wrow[2 * p + 1])
        bits = pltpu.bitcast(s, I32)
        key = jnp.where(bits < 0, bits ^ jnp.int32(0x7FFFFFFF), bits)
        keys_ref[c] = jnp.where(causal_mask(c), key, jnp.int32(INT_MIN))
        return carry

    lax.fori_loop(0, nck, phase_a, 0)

    def count(cand, strict):
        def body(c, acc):
            kc = keys_ref[c]
            hit = (kc > cand) if strict else (kc >= cand)
            return acc + jnp.sum(jnp.where(hit, 1, 0).reshape(nacc, nsub // nacc, 8, LANES), axis=1)

        acc = lax.fori_loop(0, nck, body, jnp.zeros((nacc, 8, LANES), I32))
        return jnp.sum(jnp.sum(acc, axis=0), axis=0, keepdims=True)

    def search(i, carry):
        prefix, nge = carry
        bit = lax.shift_left(jnp.int32(1), jnp.int32(31) - i)
        cand = prefix ^ bit
        cnt = count(cand, False)
        ok = cnt >= k_sel
        return jnp.where(ok, cand, prefix), jnp.where(ok, cnt, nge)

    thr, nge = lax.fori_loop(
        0, 32, search,
        (jnp.full((1, Q_BLOCK), INT_MIN, I32), jnp.zeros((1, Q_BLOCK), I32)))
    excess = jnp.where((nge > k_sel) & (thr != INT_MIN), 1, 0)
    any_excess = jnp.max(excess) > 0
    need = lax.cond(any_excess,
                    lambda: (k_sel - count(thr, True)).astype(F32),
                    lambda: jnp.zeros((1, Q_BLOCK), F32))
    thr_lo = jnp.maximum(thr, jnp.int32(INT_MIN + 1))

    qf = q_ref[0]
    qh = []
    for h in range(ATT_HEADS):
        blk = qf[:, (h // 2) * LANES:(h // 2 + 1) * LANES]
        if h % 2:
            blk = pltpu.roll(blk, ATT_HEAD_DIM, 1)
        qh.append((jnp.where(lane < ATT_HEAD_DIM, blk, 0.0) * (LOG2E * ATT_HEAD_DIM ** -0.5)).astype(BF16))
    qpair = pairs(qh)
    kv_of_pair = [(2 * p) // (ATT_HEADS // ATT_KV_HEADS) for p in range(ATT_HEADS // 2)]

    acc_ref[...] = jnp.zeros(acc_ref.shape, F32)
    m_ref[...] = jnp.full(m_ref.shape, NEG_INF, F32)
    p_ref[...] = jnp.zeros(p_ref.shape, BF16)
    npair = ATT_HEADS // 2

    def to_mask(c, eq_seen):
        kc = keys_ref[c]

        def bias_simple():
            return jnp.where(kc >= thr_lo, 0.0, NEG_INF), eq_seen

        def bias_ties():
            eq = (kc == thr) & causal_mask(c)
            eqf = jnp.where(eq, 1.0, 0.0)
            rank = eq_seen + _dot(ltri_ref[...], eqf.astype(BF16))
            take = (kc > thr) | (eq & (rank < need))
            return jnp.where(take, 0.0, NEG_INF), eq_seen + jnp.sum(eqf, axis=0, keepdims=True)

        bias, eq_next = lax.cond(any_excess, bias_ties, bias_simple)
        keys_ref[c] = pltpu.bitcast(bias, I32)
        return eq_next

    lax.fori_loop(0, nck, to_mask, jnp.zeros((1, Q_BLOCK), F32))

    def logits(c_any, slot):
        c = jnp.minimum(c_any, nck - 1)
        bias = pltpu.bitcast(keys_ref[c], F32) + jnp.where(c_any < nck, 0.0, NEG_INF)
        bias_ref[slot] = bias
        start = pl.multiple_of(c * C, C)
        for p in range(npair):
            lg2 = _dot_nt(kx_ref[0, kv_of_pair[p], pl.ds(start, C), :], qpair[p])
            lg_ref[slot * ATT_HEADS + 2 * p] = lg2[:, :LANES] + bias_ref[slot]
            lg_ref[slot * ATT_HEADS + 2 * p + 1] = lg2[:, LANES:] + bias_ref[slot]

    def weighted_values(c_any, slot, alpha):
        c = jnp.clip(c_any, 0, nck - 1)
        for p in range(npair):
            o2 = _dot(vt_ref[0, kv_of_pair[p], c], p_ref[slot * npair + p])
            for s in range(2):
                h = 2 * p + s
                acc_ref[h] = alpha[h:h + 1] * acc_ref[h] + o2[:, s * LANES:(s + 1) * LANES]

    def softmax(slot):
        shifts, alphas = [], []
        for h in range(ATT_HEADS):
            m_old = m_ref[h]
            mc = jnp.max(lg_ref[slot * ATT_HEADS + h].reshape(nsub, 8, LANES), axis=0)
            m_new = jnp.maximum(m_old, jnp.max(mc, axis=0, keepdims=True))
            m_ref[h] = m_new
            m_safe = jnp.where(m_new == NEG_INF, 0.0, m_new)
            shifts.append(m_safe[:1])
            alphas.append(jnp.exp2(m_old - m_safe)[:1])
        for h in range(ATT_HEADS):
            p_ref[slot * npair + h // 2, :, (h % 2) * LANES:(h % 2 + 1) * LANES] = (
                jnp.exp2(lg_ref[slot * ATT_HEADS + h] - shifts[h]).astype(BF16))
        return jnp.concatenate(alphas, axis=0)

    def step(c, cur, alpha_prev):
        nxt = 1 - cur
        weighted_values(c - 1, nxt, alpha_prev)
        logits(c + 1, nxt)
        return softmax(cur)

    def phase_c(i, alpha_prev):
        return step(2 * i + 1, 1, step(2 * i, 0, alpha_prev))

    logits(0, 0)
    alpha_last = lax.fori_loop(0, (nck + 1) // 2, phase_c, jnp.ones((ATT_HEADS, Q_BLOCK), F32))
    weighted_values(2 * ((nck + 1) // 2) - 1, 1, alpha_last)

    for p in range(ATT_HEADS // 2):
        a = acc_ref[2 * p]
        b = acc_ref[2 * p + 1]
        ot = jnp.concatenate([a[:ATT_HEAD_DIM] / a[ATT_HEAD_DIM:ATT_HEAD_DIM + 1],
                              b[:ATT_HEAD_DIM] / b[ATT_HEAD_DIM:ATT_HEAD_DIM + 1]], axis=0)
        o_ref[0, :, p * LANES:(p + 1) * LANES] = ot.T


def _attn(q, iq, ikf, ikb, kx, vx, k_sel):
    B, Lp, _ = q.shape
    C = _pick_tile(Lp, (640, 512, 256, 128))
    nch = Lp // C
    ltri = jnp.asarray(np.tril(np.ones((C, C), np.float32), -1), BF16)
    vt = vx.reshape(B, ATT_KV_HEADS, nch, C, LANES).transpose(0, 1, 2, 4, 3)
    qspec = lambda w: pl.BlockSpec((1, Q_BLOCK, w), lambda b, j: (b, j, 0))
    return pl.pallas_call(
        functools.partial(_attn_kernel, C=C, k_sel=k_sel),
        grid=(B, Lp // Q_BLOCK),
        in_specs=[qspec(512), qspec(256), qspec(128),
                  pl.BlockSpec((1, Lp, LANES), lambda b, j: (b, 0, 0)),
                  pl.BlockSpec((1, ATT_KV_HEADS, Lp, LANES), lambda b, j: (b, 0, 0, 0)),
                  pl.BlockSpec((1, ATT_KV_HEADS, nch, LANES, C), lambda b, j: (b, 0, 0, 0, 0)),
                  pl.BlockSpec((C, C), lambda b, j: (0, 0))],
        out_specs=qspec(512),
        out_shape=jax.ShapeDtypeStruct((B, Lp, ATT_WIDTH), F32),
        scratch_shapes=[pltpu.VMEM((nch, C, Q_BLOCK), I32),
                        pltpu.VMEM((ATT_HEADS, LANES, Q_BLOCK), F32),
                        pltpu.VMEM((ATT_HEADS, 8, Q_BLOCK), F32),
                        pltpu.VMEM((2, C, Q_BLOCK), F32),
                        pltpu.VMEM((2 * ATT_HEADS, C, Q_BLOCK), F32),
                        pltpu.VMEM((2 * (ATT_HEADS // 2), C, 2 * Q_BLOCK), BF16)],
        compiler_params=pltpu.CompilerParams(
            dimension_semantics=("arbitrary", "arbitrary"), vmem_limit_bytes=VMEM_LIMIT),
        name="attn",
    )(q, iq, ikf, ikb, kx, vt, ltri)


def _hgrn_sum_matrix(C):
    nlev = int(np.log2(C))
    t = np.arange(C)[:, None]
    jj = np.arange(C)[None, :]
    mats = [(jj <= t), (jj > t)]
    for l in range(nlev):
        s = 1 << l
        mid = ((t >> (l + 1)) << (l + 1)) + s - 1
        odd = ((t >> l) & 1) == 1
        mats.append(np.where(odd, (jj > mid) & (jj <= t), (jj > t) & (jj <= mid)))
    return np.concatenate(mats, axis=0).astype(np.float32), nlev


def _hgrn_kernel(hq_ref, hf_ref, hi_ref, hg_ref, lb_ref, gn_ref, e_ref, o_ref, st_ref, *, C, nlev):
    @pl.when(pl.program_id(1) == 0)
    def _():
        st_ref[...] = jnp.zeros(st_ref.shape, F32)

    for h in range(HG_HEADS):
        _hgrn_head(hq_ref, hf_ref, hi_ref, hg_ref, lb_ref, gn_ref, e_ref, o_ref, st_ref, h, C, nlev)


def _hgrn_head(hq_ref, hf_ref, hi_ref, hg_ref, lb_ref, gn_ref, e_ref, o_ref, st_ref, h, C, nlev):
    cols = slice(h * HG_DK, (h + 1) * HG_DK)
    lb = lb_ref[:, cols]
    f = lb + (1.0 - lb) * jax.nn.sigmoid(hf_ref[0, :, cols])
    logf = jnp.log(f)
    kk = 1.0 - f
    q = hq_ref[0, :, cols] * (HG_DK ** -0.5)
    v = hi_ref[0, :, cols].astype(BF16)

    p1 = logf.astype(BF16)
    r1 = logf - p1.astype(F32)
    p2 = r1.astype(BF16)
    p3 = (r1 - p2.astype(F32)).astype(BF16)
    x3 = _dot(e_ref[...], jnp.concatenate([p1, p2, p3], axis=1))
    xs = x3[:, :LANES] + x3[:, LANES:2 * LANES] + x3[:, 2 * LANES:]

    b = xs[0:C]
    suffix = xs[C:2 * C]
    rt = lax.broadcasted_iota(I32, (C, C), 0)
    ct = lax.broadcasted_iota(I32, (C, C), 1)
    row = lax.broadcasted_iota(I32, (C, 1), 0)

    scores = jnp.where(rt == ct, _dot_nt(q.astype(BF16), kk.astype(BF16)), 0.0)
    for l in range(nlev):
        w = jnp.exp(xs[(2 + l) * C:(3 + l) * C])
        odd = ((row >> l) & 1) == 1
        a = (jnp.where(odd, q, kk) * w).astype(BF16)
        pair = (((rt >> l) & 1) == 1) & ((ct >> l) == (rt >> l) - 1)
        scores = scores + jnp.where(pair, _dot_nt(a, a), 0.0)

    st = st_ref[h]
    qb = (q * jnp.exp(b)).astype(BF16)
    o = _dot(scores.astype(BF16), v) + _dot_nt(qb, st.astype(BF16))
    ks = (kk * jnp.exp(suffix)).astype(BF16)
    st_ref[h] = jnp.exp(b[C - 1:C, :]) * st + _dot_tn(v, ks)

    o = o * lax.rsqrt(jnp.mean(o * o, axis=-1, keepdims=True) + EPS) * gn_ref[:, cols]
    hg = hg_ref[0, :, cols]
    o_ref[0, :, cols] = o * (hg * jax.nn.sigmoid(hg))


def _hgrn(hq, hf, hi, hg, lb, gn):
    B, Lp, _ = hq.shape
    C = HG_CHUNK
    e_np, nlev = _hgrn_sum_matrix(C)
    e = jnp.asarray(e_np, BF16)
    blk = pl.BlockSpec((1, C, HG_WIDTH), lambda b, c: (b, c, 0))
    vec = pl.BlockSpec((1, HG_WIDTH), lambda b, c: (0, 0))
    return pl.pallas_call(
        functools.partial(_hgrn_kernel, C=C, nlev=nlev),
        grid=(B, Lp // C),
        in_specs=[blk, blk, blk, blk, vec, vec,
                  pl.BlockSpec(e.shape, lambda b, c: (0, 0))],
        out_specs=blk,
        out_shape=jax.ShapeDtypeStruct((B, Lp, HG_WIDTH), F32),
        scratch_shapes=[pltpu.VMEM((HG_HEADS, HG_DV, HG_DK), F32)],
        compiler_params=pltpu.CompilerParams(
            dimension_semantics=("arbitrary", "arbitrary"), vmem_limit_bytes=VMEM_LIMIT),
        name="hgrn",
    )(hq, hf, hi, hg, lb.reshape(1, -1), gn.reshape(1, -1), e)


def _merge_kernel(att_ref, hgr_ref, ga_ref, gh_ref, h_ref, wa_ref, wh_ref, wo_ref, g_ref,
                  wr_ref, br_ref, lt_ref,
                  h2_ref, xrow_ref, topi_ref, gate_ref, rank_ref, cnt_ref, *, tm):
    mix = (jax.nn.sigmoid(ga_ref[0]) * _dot(att_ref[0].astype(BF16), wa_ref[...])
           + jax.nn.sigmoid(gh_ref[0]) * _dot(hgr_ref[0].astype(BF16), wh_ref[...]))
    h2 = h_ref[0] + _dot(mix.astype(BF16), wo_ref[...])
    h2_ref[0] = h2
    xn = h2 * lax.rsqrt(jnp.mean(h2 * h2, axis=-1, keepdims=True) + EPS) * g_ref[...]
    for k in range(D_MODEL // LANES):
        xrow_ref[pl.ds(k, tm, stride=8), :] = xn[:, k * LANES:(k + 1) * LANES]

    xh = xn.astype(BF16)
    xl = (xn - xh.astype(F32)).astype(BF16)
    l1 = _dot(xh, wr_ref[...])
    logits = l1[:, :LANES] + l1[:, LANES:] + _dot(xl, wr_ref[:, :LANES]) + br_ref[...]

    lane = _lane_iota((tm, LANES))
    lanef = lane.astype(F32)
    work = logits
    vals, sels, idxs = [], [], []
    for _ in range(TOP_K):
        m = jnp.max(work, axis=1, keepdims=True)
        idx = jnp.min(jnp.where(work == m, lanef, float(LANES)), axis=1, keepdims=True)
        sel = lanef == idx
        vals.append(m)
        idxs.append(idx)
        sels.append(sel)
        work = jnp.where(sel, NEG_INF, work)
    es = [jnp.exp(v - vals[0]) for v in vals]
    den = es[0] + es[1] + es[2] + es[3]

    member = jnp.zeros((tm, LANES), F32)
    for sel in sels:
        member = member + jnp.where(sel, 1.0, 0.0)
    before = _dot(lt_ref[...], member.astype(BF16))
    topi = jnp.zeros((tm, LANES), F32)
    gate = jnp.zeros((tm, LANES), F32)
    rank = jnp.zeros((tm, LANES), F32)
    for jx in range(TOP_K):
        here = lane == jx
        topi = jnp.where(here, idxs[jx], topi)
        gate = jnp.where(here, es[jx] / den, gate)
        rank = jnp.where(here, jnp.sum(jnp.where(sels[jx], before, 0.0), axis=1, keepdims=True), rank)
    topi_ref[0] = topi.astype(I32)
    gate_ref[0] = gate
    rank_ref[0] = rank.astype(I32)
    cnt_ref[0] = jnp.sum(member, axis=0, keepdims=True).astype(I32)


def _merge(att, hgr, ga, gh, h, wa, wh, wo, g, wr2, br):
    B, Lp, D = h.shape
    tm = _pick_tile(Lp, (320, 256, 128))
    nt = Lp // tm
    lt = jnp.asarray(np.tril(np.ones((tm, tm), np.float32), -1), BF16)
    row = lambda w: pl.BlockSpec((1, tm, w), lambda b, i: (b, i, 0))
    full = lambda a: pl.BlockSpec(a.shape, lambda b, i: (0,) * a.ndim)
    g2 = g.reshape(1, D)
    out_shape = [jax.ShapeDtypeStruct((B, Lp, D), F32),
                 jax.ShapeDtypeStruct((B * Lp * 8, LANES), F32),
                 jax.ShapeDtypeStruct((B, Lp, LANES), I32),
                 jax.ShapeDtypeStruct((B, Lp, LANES), F32),
                 jax.ShapeDtypeStruct((B, Lp, LANES), I32),
                 jax.ShapeDtypeStruct((B * nt, 1, LANES), I32)]
    out_specs = [row(D),
                 pl.BlockSpec((tm * 8, LANES), lambda b, i: (b * nt + i, 0)),
                 row(LANES), row(LANES), row(LANES),
                 pl.BlockSpec((1, 1, LANES), lambda b, i: (b * nt + i, 0, 0))]
    return pl.pallas_call(
        functools.partial(_merge_kernel, tm=tm),
        grid=(B, nt),
        in_specs=[row(512), row(512), row(D), row(D), row(D),
                  full(wa), full(wh), full(wo), full(g2), full(wr2), full(br), full(lt)],
        out_specs=out_specs,
        out_shape=out_shape,
        compiler_params=pltpu.CompilerParams(
            dimension_semantics=("arbitrary", "arbitrary"), vmem_limit_bytes=VMEM_LIMIT),
        name="merge",
    )(att, hgr, ga, gh, h, wa, wh, wo, g2, wr2, br, lt), tm


def _slots_kernel(topi_ref, rank_ref, base_ref, slot_ref):
    topi = topi_ref[0]
    rank = rank_ref[0]
    base = base_ref[0]
    lane = _lane_iota(topi.shape)
    slot = jnp.zeros(topi.shape, I32)
    for jx in range(TOP_K):
        mine = lane == topi[:, jx:jx + 1]
        first = jnp.sum(jnp.where(mine, base, 0), axis=1, keepdims=True)
        slot = jnp.where(lane == jx, first + rank[:, jx:jx + 1], slot)
    slot_ref[0] = slot


def _slots(topi, rank, base, tm):
    B, Lp, _ = topi.shape
    nt = Lp // tm
    row = pl.BlockSpec((1, tm, LANES), lambda b, i: (b, i, 0))
    return pl.pallas_call(
        _slots_kernel,
        grid=(B, nt),
        in_specs=[row, row, pl.BlockSpec((1, 1, LANES), lambda b, i: (b * nt + i, 0, 0))],
        out_specs=row,
        out_shape=jax.ShapeDtypeStruct((B, Lp, LANES), I32),
        compiler_params=pltpu.CompilerParams(dimension_semantics=("arbitrary", "arbitrary")),
        name="slots",
    )(topi, rank, base)


def _dispatch_kernel(pad_lo_ref, pad_n_ref, nu_ref, slot_ref, x_ref, xs_hbm, zblk_ref, sem, *, tm, blk, nblk):
    @pl.when(pl.program_id(0) == 0)
    def _():
        zblk_ref[...] = jnp.zeros(zblk_ref.shape, F32)

        def zblock(b):
            return pltpu.make_async_copy(zblk_ref, xs_hbm.at[pl.ds(b * blk, blk)], sem)

        def bissue(b, c):
            zblock(b).start()
            return c

        def bdrain(b, c):
            zblock(b).wait()
            return c

        lax.fori_loop(nu_ref[0], nblk, bissue, 0)
        lax.fori_loop(nu_ref[0], nblk, bdrain, 0)

        def per_expert(e, c):
            def zcopy(r):
                return pltpu.make_async_copy(zblk_ref.at[0], xs_hbm.at[pad_lo_ref[e] + r], sem)

            def zissue(r, c2):
                zcopy(r).start()
                return c2

            def zdrain(r, c2):
                zcopy(r).wait()
                return c2

            lax.fori_loop(0, pad_n_ref[e], zissue, 0)
            lax.fori_loop(0, pad_n_ref[e], zdrain, 0)
            return c

        lax.fori_loop(0, N_EXPERTS, per_expert, 0)

    def copy(t, jx):
        return pltpu.make_async_copy(x_ref.at[t], xs_hbm.at[slot_ref[0, 0, t * TOP_K + jx]], sem)

    def issue(t, c):
        for jx in range(TOP_K):
            copy(t, jx).start()
        return c

    def drain(t, c):
        for jx in range(TOP_K):
            copy(t, jx).wait()
        return c

    lax.fori_loop(0, tm, issue, 0)
    lax.fori_loop(0, tm, drain, 0)


def _dispatch(pad_lo, pad_n, n_used, slot_s, xrow, cap, tm, blk):
    nt = slot_s.shape[0]
    return pl.pallas_call(
        functools.partial(_dispatch_kernel, tm=tm, blk=blk, nblk=cap // blk),
        grid_spec=pltpu.PrefetchScalarGridSpec(
            num_scalar_prefetch=3, grid=(nt,),
            in_specs=[pl.BlockSpec((1, 1, tm * TOP_K), lambda i, lo, n, nu: (i, 0, 0), memory_space=pltpu.SMEM),
                      pl.BlockSpec((tm, 8, LANES), lambda i, lo, n, nu: (i, 0, 0))],
            out_specs=pl.BlockSpec(memory_space=pl.ANY),
            scratch_shapes=[pltpu.VMEM((blk, 8, LANES), F32), pltpu.SemaphoreType.DMA(())]),
        out_shape=jax.ShapeDtypeStruct((cap, 8, LANES), F32),
        compiler_params=pltpu.CompilerParams(dimension_semantics=("arbitrary",)),
        name="dispatch",
    )(pad_lo, pad_n, n_used, slot_s, xrow)


def _inverse_kernel(pad_lo_ref, pad_n_ref, nu_ref, slot_ref, tok_ref, *, tm, blk, nblk):
    i = pl.program_id(0)

    @pl.when(i == 0)
    def _():
        def per_expert(e, c):
            def clear(r, c2):
                tok_ref[pad_lo_ref[e] + r] = 0
                return c2

            lax.fori_loop(0, pad_n_ref[e], clear, 0)
            return c

        lax.fori_loop(0, N_EXPERTS, per_expert, 0)

        def clear_tail(s, c):
            tok_ref[s] = 0
            return c

        lax.fori_loop(nu_ref[0] * blk, nblk * blk, clear_tail, 0)

    def put(t, c):
        for jx in range(TOP_K):
            tok_ref[slot_ref[0, 0, t * TOP_K + jx]] = i * tm + t
        return c

    lax.fori_loop(0, tm, put, 0)


def _inverse(pad_lo, pad_n, n_used, slot_s, cap, tm, blk):
    nt = slot_s.shape[0]
    return pl.pallas_call(
        functools.partial(_inverse_kernel, tm=tm, blk=blk, nblk=cap // blk),
        grid_spec=pltpu.PrefetchScalarGridSpec(
            num_scalar_prefetch=3, grid=(nt,),
            in_specs=[pl.BlockSpec((1, 1, tm * TOP_K), lambda i, lo, n, nu: (i, 0, 0), memory_space=pltpu.SMEM)],
            out_specs=pl.BlockSpec(memory_space=pltpu.SMEM)),
        out_shape=jax.ShapeDtypeStruct((cap,), I32),
        compiler_params=pltpu.CompilerParams(dimension_semantics=("arbitrary",)),
        name="inverse",
    )(pad_lo, pad_n, n_used, slot_s)


def _ffn_kernel(be_ref, nu_ref, tok_ref, tok_next_ref, x_hbm, wg_ref, bg_ref, wu_ref, bu_ref, wd_ref, bd_ref,
                o_ref, wgb_ref, wub_ref, wdb_ref, xbuf_ref, sem, *, blk):
    i = pl.program_id(0)
    nk = D_MODEL // LANES
    used = i < nu_ref[0]
    new_expert = (i == 0) | (be_ref[i] != be_ref[jnp.maximum(i - 1, 0)])
    cur = lax.rem(i, 2)

    def gather(tokens_ref, half, r):
        return pltpu.make_async_copy(x_hbm.at[tokens_ref[0, 0, r]], xbuf_ref.at[half, pl.ds(r * 8, 8), :], sem)

    @pl.when(i == 0)
    def _():
        for r in range(blk):
            gather(tok_ref, 0, r).start()

    @pl.when(used & new_expert)
    def _():
        rows = 128

        def cast(r, c):
            sl = pl.ds(pl.multiple_of(r * rows, rows), rows)
            wgb_ref[sl, :] = wg_ref[0, sl, :].astype(BF16)
            wub_ref[sl, :] = wu_ref[0, sl, :].astype(BF16)
            wdb_ref[sl, :] = wd_ref[0, sl, :].astype(BF16)
            return c

        lax.fori_loop(0, D_MODEL // rows, cast, 0)

    @pl.when(used)
    def _():
        for r in range(blk):
            gather(tok_ref, cur, r).wait()
        for r in range(blk):
            gather(tok_next_ref, 1 - cur, r).start()
        xb = jnp.concatenate([xbuf_ref[cur, pl.ds(k, blk, stride=8), :] for k in range(nk)], axis=1).astype(BF16)
        g = _dot(xb, wgb_ref[...]) + bg_ref[0]
        u = _dot(xb, wub_ref[...]) + bu_ref[0]
        g = jnp.minimum(g, SWIGLU_LIMIT)
        u = jnp.clip(u, -SWIGLU_LIMIT, SWIGLU_LIMIT)
        act = (u + 1.0) * (g * jax.nn.sigmoid(SWIGLU_ALPHA * g))
        out = _dot(act.astype(BF16), wdb_ref[...]) + bd_ref[0]
        for k in range(nk):
            o_ref[pl.ds(k, blk, stride=8), :] = out[:, k * LANES:(k + 1) * LANES]

    @pl.when(i == nu_ref[0])
    def _():
        for r in range(blk):
            gather(tok_ref, cur, r).wait()

    @pl.when(i >= nu_ref[0])
    def _():
        o_ref[...] = jnp.zeros(o_ref.shape, F32)


def _ffn(block_expert, n_used, tok, xrow, wg, bg, wu, bu, wd, bd, blk):
    cap = tok.shape[0]
    nblk = cap // blk
    tok3 = tok.reshape(nblk, 1, blk)
    wspec = pl.BlockSpec((1, D_MODEL, D_FF), lambda i, be, nu: (be[i], 0, 0))
    wdspec = pl.BlockSpec((1, D_FF, D_MODEL), lambda i, be, nu: (be[i], 0, 0))
    bspec = lambda w: pl.BlockSpec((1, 1, w), lambda i, be, nu: (be[i], 0, 0))
    rows = pl.BlockSpec((blk * 8, LANES), lambda i, be, nu: (i, 0))
    tok_cur = pl.BlockSpec((1, 1, blk), lambda i, be, nu: (i, 0, 0), memory_space=pltpu.SMEM)
    tok_next = pl.BlockSpec((1, 1, blk), lambda i, be, nu: (jnp.minimum(i + 1, nblk - 1), 0, 0),
                            memory_space=pltpu.SMEM)
    assert D_FF == D_MODEL
    wscratch = pltpu.VMEM((D_MODEL, D_FF), BF16)
    out = pl.pallas_call(
        functools.partial(_ffn_kernel, blk=blk),
        grid_spec=pltpu.PrefetchScalarGridSpec(
            num_scalar_prefetch=2, grid=(nblk,),
            in_specs=[tok_cur, tok_next, pl.BlockSpec(memory_space=pl.ANY),
                      wspec, bspec(D_FF), wspec, bspec(D_FF), wdspec, bspec(D_MODEL)],
            out_specs=rows,
            scratch_shapes=[wscratch, wscratch, wscratch,
                            pltpu.VMEM((2, blk * 8, LANES), F32), pltpu.SemaphoreType.DMA(())]),
        out_shape=jax.ShapeDtypeStruct((cap * 8, LANES), F32),
        compiler_params=pltpu.CompilerParams(
            dimension_semantics=("arbitrary",), vmem_limit_bytes=VMEM_LIMIT),
        name="ffn",
    )(block_expert, n_used, tok3, tok3, xrow, wg, bg, wu, bu, wd, bd)
    return out.reshape(cap, 8, LANES)


def _combine_kernel(slot_ref, gate_ref, h2_ref, g_ref, ys_hbm, o_ref, buf_ref, sem, *, tm):
    def copy(t, jx):
        dst = buf_ref.at[pl.ds(pl.multiple_of((jx * tm + t) * 8, 8), 8), :]
        return pltpu.make_async_copy(ys_hbm.at[slot_ref[0, 0, t * TOP_K + jx]], dst, sem)

    def issue(t, c):
        for jx in range(TOP_K):
            copy(t, jx).start()
        return c

    def drain(t, c):
        for jx in range(TOP_K):
            copy(t, jx).wait()
        return c

    lax.fori_loop(0, tm, issue, 0)
    lax.fori_loop(0, tm, drain, 0)

    gate = gate_ref[0]
    h2 = h2_ref[0]
    cols = []
    for k in range(D_MODEL // LANES):
        y = h2[:, k * LANES:(k + 1) * LANES]
        for jx in range(TOP_K):
            y = y + gate[:, jx:jx + 1] * buf_ref[pl.ds(jx * tm * 8 + k, tm, stride=8), :]
        cols.append(y)
    ho = jnp.concatenate(cols, axis=1)
    o_ref[0] = ho * lax.rsqrt(jnp.mean(ho * ho, axis=-1, keepdims=True) + EPS) * g_ref[...]


def _combine(slot_s, gate, h2, g, ys, tm):
    B, Lp, D = h2.shape
    nt = Lp // tm
    row = lambda w: pl.BlockSpec((1, tm, w), lambda b, i: (b, i, 0))
    return pl.pallas_call(
        functools.partial(_combine_kernel, tm=tm),
        grid=(B, nt),
        in_specs=[pl.BlockSpec((1, 1, tm * TOP_K), lambda b, i: (b * nt + i, 0, 0), memory_space=pltpu.SMEM),
                  row(LANES), row(D),
                  pl.BlockSpec((1, D), lambda b, i: (0, 0)),
                  pl.BlockSpec(memory_space=pl.ANY)],
        out_specs=row(D),
        out_shape=jax.ShapeDtypeStruct((B, Lp, D), F32),
        scratch_shapes=[pltpu.VMEM((TOP_K * tm * 8, LANES), F32), pltpu.SemaphoreType.DMA(())],
        compiler_params=pltpu.CompilerParams(
            dimension_semantics=("arbitrary", "arbitrary"), vmem_limit_bytes=VMEM_LIMIT),
        name="combine",
    )(slot_s, gate, h2, g.reshape(1, D), ys)


def _moe(h2, xrow, topi, gate, rank, cnt, tm, ffn_w, final_g):
    B, Lp, D = h2.shape
    T = B * Lp
    nt_total = cnt.shape[0]
    blk = FFN_BLOCK
    nblk = -(-(T * TOP_K) // blk) + N_EXPERTS + 1
    cap = nblk * blk
    cnt2 = cnt.reshape(nt_total, LANES)
    totals = jnp.sum(cnt2, axis=0)
    padded = (totals + blk - 1) // blk * blk
    pad_ends = jnp.cumsum(padded)
    pad_starts = pad_ends - padded
    base = (pad_starts[None, :] + jnp.cumsum(cnt2, axis=0) - cnt2).astype(I32).reshape(nt_total, 1, LANES)
    block_start = (jnp.arange(nblk) * blk)[:, None]
    block_expert = jnp.minimum(
        jnp.sum(pad_ends[None, :N_EXPERTS] <= block_start, axis=1), N_EXPERTS - 1).astype(I32)
    n_used = (pad_ends[N_EXPERTS - 1] // blk).astype(I32).reshape(1)
    slot = _slots(topi, rank, base, tm)
    slot_s = slot[:, :, :TOP_K].reshape(nt_total, 1, tm * TOP_K)

    pad_lo = (pad_starts + totals)[:N_EXPERTS].astype(I32)
    pad_n = (padded - totals)[:N_EXPERTS].astype(I32)
    tok = _inverse(pad_lo, pad_n, n_used, slot_s, cap, tm, blk)
    ys = _ffn(block_expert, n_used, tok, xrow.reshape(T, 8, LANES), *ffn_w, blk)
    return _combine(slot_s, gate, h2, final_g, ys, tm)


def kernel(x, meta_tokens, attn_norm_g, w_in, hgrn_norm_g, w_up_attn, w_up_hgrn, w_out,
           hgrn_lb_logits, ffn_norm_g, router_w, router_b, w_gate, b_gate, w_up, b_up,
           w_down, b_down, final_norm_g):
    B, S, D = x.shape
    L = S + N_META
    Lp = -(-L // Q_BLOCK) * Q_BLOCK
    k_sel = min(TOPK_MAX, L // 4)
    meta = jnp.broadcast_to(meta_tokens[None].astype(x.dtype), (B, N_META, D))
    h = jnp.concatenate([meta, x, jnp.zeros((B, Lp - L, D), x.dtype)], axis=1)
    lb = jnp.cumsum(jax.nn.softmax(hgrn_lb_logits.astype(F32), axis=0), axis=0)[0]

    (q, kx, vx, iq, ikf, ikb, hq, hf, hi, hg, ga, gh) = _inproj(h, attn_norm_g[0], _arrange_w_in(w_in[0]))
    att = _attn(q, iq, ikf, ikb, kx, vx, k_sel)
    hgr = _hgrn(hq, hf, hi, hg, lb, hgrn_norm_g[0])

    wr = jnp.pad(router_w[0], ((0, 0), (0, LANES - N_EXPERTS)))
    wr_hi = wr.astype(BF16)
    wr_lo = (wr - wr_hi.astype(F32)).astype(BF16)
    br = jnp.pad(router_b[0], (0, LANES - N_EXPERTS), constant_values=-1e30).reshape(1, LANES)
    (h2, xrow, topi, gate, rank, cnt), tm = _merge(
        att, hgr, ga, gh, h, w_up_attn[0].astype(BF16), w_up_hgrn[0].astype(BF16), w_out[0].astype(BF16),
        ffn_norm_g[0], jnp.concatenate([wr_hi, wr_lo], axis=1), br)

    ffn_w = (w_gate[0], b_gate[0].reshape(N_EXPERTS, 1, D_FF),
             w_up[0], b_up[0].reshape(N_EXPERTS, 1, D_FF),
             w_down[0], b_down[0].reshape(N_EXPERTS, 1, D_MODEL))
    out = _moe(h2, xrow, topi, gate, rank, cnt, tm, ffn_w, final_norm_g)
    return out[:, N_META:L]
```

```python
import functools

import numpy as np
import jax
import jax.numpy as jnp
from jax import lax
from jax.experimental import pallas as pl
from jax.experimental.pallas import tpu as pltpu

F32 = jnp.float32
BF16 = jnp.bfloat16
I32 = jnp.int32

D_MODEL = 1024
N_META = 16
EPS = 1e-5
ROPE_THETA = 500000.0
ATT_HEADS = 8
ATT_KV_HEADS = 2
ATT_HEAD_DIM = 64
ATT_WIDTH = ATT_HEADS * ATT_HEAD_DIM
IDX_HEADS = 8
IDX_DIM = 32
TOPK_MAX = 256
Q_BLOCK = 128
HG_HEADS = 4
HG_DK = 128
HG_DV = 128
HG_WIDTH = HG_HEADS * HG_DV
N_EXPERTS = 32
TOP_K = 4
D_FF = 1024
SWIGLU_LIMIT = 7.0
SWIGLU_ALPHA = 1.702

LANES = 128
VMEM_LIMIT = 56 * 1024 * 1024
INT_MIN = -2 ** 31
NEG_INF = float("-inf")
LOG2E = 1.4426950408889634

SEARCH_CHECK_BIT = 28
HG_CHUNK = 128
FFN_BLOCK = 256


def _pick_tile(n, candidates):
    for c in candidates:
        if n % c == 0:
            return c
    raise ValueError(f"no tile for {n}")


def _dot(a, b):
    return jnp.dot(a, b, preferred_element_type=F32)


def _dot_nt(a, b):
    return lax.dot_general(a, b, (((1,), (1,)), ((), ())), preferred_element_type=F32)


def _dot_tn(a, b):
    return lax.dot_general(a, b, (((0,), (0,)), ((), ())), preferred_element_type=F32)


def _lane_iota(shape):
    return lax.broadcasted_iota(I32, shape, len(shape) - 1)


def _rope_tables(n_pos, head_dim, lane_valid):
    rot = head_dim // 4
    half = rot // 2
    inv = ROPE_THETA ** (-jnp.arange(half, dtype=F32) * 2.0 / rot)
    ang = jnp.arange(n_pos).astype(F32)[:, None] * inv[None, :]
    c = jnp.cos(ang)
    s = jnp.sin(ang)
    lane = np.arange(LANES)
    within = lane % head_dim
    valid = lane < lane_valid
    first = (within < half) & valid
    second = (within >= half) & (within < rot) & valid
    idx = np.where(within < half, within, within - half) % half
    cc = c[:, idx]
    ss = s[:, idx]
    tc = jnp.where((first | second)[None, :], cc, 1.0)
    t1 = jnp.where(first[None, :], -ss, 0.0)
    t2 = jnp.where(second[None, :], ss, 0.0)
    return jnp.stack([tc, t1, t2]).astype(F32), half


def _rope_apply(x, tab_ref, half):
    c = tab_ref[0]
    s1 = tab_ref[1]
    s2 = tab_ref[2]
    cols = []
    for j in range(x.shape[1] // LANES):
        xc = x[:, j * LANES:(j + 1) * LANES]
        cols.append(xc * c + pltpu.roll(xc, LANES - half, 1) * s1 + pltpu.roll(xc, half, 1) * s2)
    return cols[0] if len(cols) == 1 else jnp.concatenate(cols, axis=1)


_W_Q = (0, 512)
_W_KX = (512, 768)
_W_VX = (768, 1024)
_W_IQ = (1024, 1280)
_W_IK = (1280, 1408)
_W_HQ = (1408, 1920)
_W_HF = (1920, 2432)
_W_HI = (2432, 2944)
_W_HG = (2944, 3456)
_W_GA = (3456, 4480)
_W_GH = (4480, 5504)
_W_COLS = 5504


def _arrange_w_in(w_in):
    o = np.cumsum([0, 512, 128, 128, 256, 32, 8, 512, 512, 512, 512, 1024, 1024])
    seg = [w_in[:, o[i]:o[i + 1]] for i in range(12)]
    q, k, v, iq, ik, iw, hq, hf, hi, hg, ga, gh = seg
    z64 = jnp.zeros((D_MODEL, 64), w_in.dtype)
    z88 = jnp.zeros((D_MODEL, 88), w_in.dtype)
    kx = jnp.concatenate([k[:, :64], z64, k[:, 64:], z64], axis=1)
    vx = jnp.concatenate([v[:, :64], z64, v[:, 64:], z64], axis=1)
    ikx = jnp.concatenate([ik, iw, z88], axis=1)
    w = jnp.concatenate([q, kx, vx, iq, ikx, hq, hf, hi, hg, ga, gh], axis=1)
    assert w.shape[1] == _W_COLS
    return w.astype(BF16)


def _inproj_kernel(h_ref, g_ref, w_ref, tatt_ref, tkx_ref, tiq_ref, tik_ref,
                   q_ref, kx_ref, vx_ref, iq_ref, ikf_ref, ikb_ref,
                   hq_ref, hf_ref, hi_ref, hg_ref, ga_ref, gh_ref, *, half_att, half_idx):
    x = h_ref[0]
    xn = x * lax.rsqrt(jnp.mean(x * x, axis=-1, keepdims=True) + EPS) * g_ref[...]
    xb = xn.astype(BF16)

    def proj(rng):
        return _dot(xb, w_ref[:, rng[0]:rng[1]])

    q_ref[0] = _rope_apply(proj(_W_Q), tatt_ref, half_att)
    kx = _rope_apply(proj(_W_KX), tkx_ref, half_att).astype(BF16)
    kx_ref[0, 0] = kx[:, :LANES]
    kx_ref[0, 1] = kx[:, LANES:]
    vx = proj(_W_VX)
    ones_hi = jnp.where(_lane_iota(vx.shape) % LANES >= ATT_HEAD_DIM, 1.0, 0.0)
    vx = (vx + ones_hi).astype(BF16)
    vx_ref[0, 0] = vx[:, :LANES]
    vx_ref[0, 1] = vx[:, LANES:]
    iq_ref[0] = _rope_apply(proj(_W_IQ), tiq_ref, half_idx)
    ik = _rope_apply(proj(_W_IK), tik_ref, half_idx)
    ikf_ref[0] = ik
    ikb_ref[0] = ik.astype(BF16)
    hq_ref[0] = proj(_W_HQ)
    hf_ref[0] = proj(_W_HF)
    hi_ref[0] = proj(_W_HI)
    hg_ref[0] = proj(_W_HG)
    ga_ref[0] = proj(_W_GA)
    gh_ref[0] = proj(_W_GH)


def _inproj(h, g, w_arr):
    B, Lp, D = h.shape
    tm = _pick_tile(Lp, (320, 256, 128))
    tatt, half_att = _rope_tables(Lp, ATT_HEAD_DIM, LANES)
    tkx, _ = _rope_tables(Lp, ATT_HEAD_DIM, ATT_HEAD_DIM)
    tiq, half_idx = _rope_tables(Lp, IDX_DIM, LANES)
    tik, _ = _rope_tables(Lp, IDX_DIM, IDX_DIM)

    def rows(width, dtype=F32):
        return (jax.ShapeDtypeStruct((B, Lp, width), dtype),
                pl.BlockSpec((1, tm, width), lambda b, i: (b, i, 0)))

    def kv(dtype=BF16):
        return (jax.ShapeDtypeStruct((B, ATT_KV_HEADS, Lp, LANES), dtype),
                pl.BlockSpec((1, ATT_KV_HEADS, tm, LANES), lambda b, i: (b, 0, i, 0)))

    outs = [rows(512), kv(), kv(), rows(256), rows(128), rows(128, BF16),
            rows(512), rows(512), rows(512), rows(512), rows(1024), rows(1024)]
    tab_spec = pl.BlockSpec((3, tm, LANES), lambda b, i: (0, i, 0))
    return pl.pallas_call(
        functools.partial(_inproj_kernel, half_att=half_att, half_idx=half_idx),
        grid=(B, Lp // tm),
        in_specs=[pl.BlockSpec((1, tm, D), lambda b, i: (b, i, 0)),
                  pl.BlockSpec((1, D), lambda b, i: (0, 0)),
                  pl.BlockSpec((D, _W_COLS), lambda b, i: (0, 0)),
                  tab_spec, tab_spec, tab_spec, tab_spec],
        out_specs=[o[1] for o in outs],
        out_shape=[o[0] for o in outs],
        compiler_params=pltpu.CompilerParams(
            dimension_semantics=("arbitrary", "arbitrary"), vmem_limit_bytes=VMEM_LIMIT),
        name="inproj",
    )(h, g.reshape(1, D), w_arr, tatt, tkx, tiq, tik)


def _attn_kernel(q_ref, iq_ref, ikf_ref, ikb_ref, kx_ref, vt_ref, ltri_ref, o_ref,
                 keys_ref, acc_ref, m_ref, bias_ref, lg_ref, p_ref, *, C, k_sel):
    j = pl.program_id(1)
    nck = ((j + 1) * Q_BLOCK + C - 1) // C
    lane = _lane_iota((Q_BLOCK, LANES))
    qpos = j * Q_BLOCK + lax.broadcasted_iota(I32, (1, Q_BLOCK), 1)
    nsub = C // 8
    nacc = 8

    def pairs(heads):
        return [jnp.concatenate([heads[2 * p], heads[2 * p + 1]], axis=0) for p in range(len(heads) // 2)]

    iqf = iq_ref[0]
    per_blk = LANES // IDX_DIM
    iqh = []
    for h in range(IDX_HEADS):
        blk = iqf[:, (h // per_blk) * LANES:(h // per_blk + 1) * LANES]
        off = (h % per_blk) * IDX_DIM
        if off:
            blk = pltpu.roll(blk, LANES - off, 1)
        iqh.append(jnp.where(lane < IDX_DIM, blk, 0.0).astype(BF16))
    iqpair = pairs(iqh)
    wt = ikf_ref[0].T
    wrow = [wt[IDX_DIM + h:IDX_DIM + h + 1, :] for h in range(IDX_HEADS)]

    def causal_mask(c):
        kpos = c * C + lax.broadcasted_iota(I32, (C, 1), 0)
        return kpos <= qpos

    def phase_a(c, carry):
        ks = ikb_ref[0, pl.ds(pl.multiple_of(c * C, C), C), :]
        s = jnp.zeros((C, Q_BLOCK), F32)
        for p in range(IDX_HEADS // 2):
            d2 = _dot_nt(ks, iqpair[p])
            s = (s + jnp.maximum(d2[:, :LANES], 0.0) * wrow[2 * p]
                 + jnp.maximum(d2[:, LANES:], 0.0) * wrow[2 * p + 1])
        bits = pltpu.bitcast(s, I32)
        key = jnp.where(bits < 0, bits ^ jnp.int32(0x7FFFFFFF), bits)
        keys_ref[c] = jnp.where(causal_mask(c), key, jnp.int32(INT_MIN))
        return carry

    lax.fori_loop(0, nck, phase_a, 0)

    def count(cand, strict):
        def body(c, acc):
            kc = keys_ref[c]
            hit = (kc > cand) if strict else (kc >= cand)
            return acc + jnp.sum(jnp.where(hit, 1, 0).reshape(nacc, nsub // nacc, 8, LANES), axis=1)

        acc = lax.fori_loop(0, nck, body, jnp.zeros((nacc, 8, LANES), I32))
        return jnp.sum(jnp.sum(acc, axis=0), axis=0, keepdims=True)

    def search(i, carry):
        prefix, nge = carry
        bit = lax.shift_left(jnp.int32(1), jnp.int32(31) - i)
        cand = prefix ^ bit
        cnt = count(cand, False)
        ok = cnt >= k_sel
        return jnp.where(ok, cand, prefix), jnp.where(ok, cnt, nge)

    thr, nge = lax.fori_loop(
        0, SEARCH_CHECK_BIT, search,
        (jnp.full((1, Q_BLOCK), INT_MIN, I32), jnp.zeros((1, Q_BLOCK), I32)))
    unsettled = jnp.max(jnp.where((nge == k_sel) | (qpos < k_sel), 0, 1)) > 0
    thr, nge = lax.cond(unsettled,
                        lambda: lax.fori_loop(SEARCH_CHECK_BIT, 32, search, (thr, nge)),
                        lambda: (thr, nge))
    excess = jnp.where((nge > k_sel) & (thr != INT_MIN), 1, 0)
    any_excess = jnp.max(excess) > 0
    need = lax.cond(any_excess,
                    lambda: (k_sel - count(thr, True)).astype(F32),
                    lambda: jnp.zeros((1, Q_BLOCK), F32))
    thr_lo = jnp.maximum(thr, jnp.int32(INT_MIN + 1))

    qf = q_ref[0]
    qh = []
    for h in range(ATT_HEADS):
        blk = qf[:, (h // 2) * LANES:(h // 2 + 1) * LANES]
        if h % 2:
            blk = pltpu.roll(blk, ATT_HEAD_DIM, 1)
        qh.append((jnp.where(lane < ATT_HEAD_DIM, blk, 0.0) * (LOG2E * ATT_HEAD_DIM ** -0.5)).astype(BF16))
    qpair = pairs(qh)
    kv_of_pair = [(2 * p) // (ATT_HEADS // ATT_KV_HEADS) for p in range(ATT_HEADS // 2)]

    acc_ref[...] = jnp.zeros(acc_ref.shape, F32)
    m_ref[...] = jnp.full(m_ref.shape, NEG_INF, F32)
    npair = ATT_HEADS // 2
    p_ref[npair:] = jnp.zeros((npair,) + p_ref.shape[1:], BF16)

    def to_mask(c, eq_seen):
        kc = keys_ref[c]

        def bias_simple():
            return jnp.where(kc >= thr_lo, 0.0, NEG_INF), eq_seen

        def bias_ties():
            eq = (kc == thr) & causal_mask(c)
            eqf = jnp.where(eq, 1.0, 0.0)
            rank = eq_seen + _dot(ltri_ref[...], eqf.astype(BF16))
            take = (kc > thr) | (eq & (rank < need))
            return jnp.where(take, 0.0, NEG_INF), eq_seen + jnp.sum(eqf, axis=0, keepdims=True)

        bias, eq_next = lax.cond(any_excess, bias_ties, bias_simple)
        keys_ref[c] = pltpu.bitcast(bias, I32)
        return eq_next

    lax.fori_loop(0, nck, to_mask, jnp.zeros((1, Q_BLOCK), F32))

    def logits(c_any, slot):
        c = jnp.minimum(c_any, nck - 1)
        bias = pltpu.bitcast(keys_ref[c], F32) + jnp.where(c_any < nck, 0.0, NEG_INF)
        bias_ref[slot] = bias
        start = pl.multiple_of(c * C, C)
        for p in range(npair):
            lg2 = _dot_nt(kx_ref[0, kv_of_pair[p], pl.ds(start, C), :], qpair[p])
            lg_ref[slot * ATT_HEADS + 2 * p] = lg2[:, :LANES] + bias_ref[slot]
            lg_ref[slot * ATT_HEADS + 2 * p + 1] = lg2[:, LANES:] + bias_ref[slot]

    def weighted_values(c_any, slot, alpha):
        c = jnp.clip(c_any, 0, nck - 1)
        for p in range(npair):
            o2 = _dot(vt_ref[0, kv_of_pair[p], c], p_ref[slot * npair + p])
            for s in range(2):
                h = 2 * p + s
                acc_ref[h] = alpha[h:h + 1] * acc_ref[h] + o2[:, s * LANES:(s + 1) * LANES]

    def softmax(slot):
        shifts, alphas = [], []
        for h in range(ATT_HEADS):
            m_old = m_ref[h]
            mc = jnp.max(lg_ref[slot * ATT_HEADS + h].reshape(nsub, 8, LANES), axis=0)
            m_new = jnp.maximum(m_old, jnp.max(mc, axis=0, keepdims=True))
            m_ref[h] = m_new
            m_safe = jnp.where(m_new == NEG_INF, 0.0, m_new)
            shifts.append(m_safe[:1])
            alphas.append(jnp.exp2(m_old - m_safe)[:1])
        for h in range(ATT_HEADS):
            p_ref[slot * npair + h // 2, :, (h % 2) * LANES:(h % 2 + 1) * LANES] = (
                jnp.exp2(lg_ref[slot * ATT_HEADS + h] - shifts[h]).astype(BF16))
        return jnp.concatenate(alphas, axis=0)

    def step(c, cur, alpha_prev):
        nxt = 1 - cur
        weighted_values(c - 1, nxt, alpha_prev)
        logits(c + 1, nxt)
        return softmax(cur)

    def phase_c(i, alpha_prev):
        return step(2 * i + 1, 1, step(2 * i, 0, alpha_prev))

    logits(0, 0)
    alpha_last = lax.fori_loop(0, (nck + 1) // 2, phase_c, jnp.ones((ATT_HEADS, Q_BLOCK), F32))
    weighted_values(2 * ((nck + 1) // 2) - 1, 1, alpha_last)

    for p in range(ATT_HEADS // 2):
        a = acc_ref[2 * p]
        b = acc_ref[2 * p + 1]
        ot = jnp.concatenate([a[:ATT_HEAD_DIM] / a[ATT_HEAD_DIM:ATT_HEAD_DIM + 1],
                              b[:ATT_HEAD_DIM] / b[ATT_HEAD_DIM:ATT_HEAD_DIM + 1]], axis=0)
        o_ref[0, :, p * LANES:(p + 1) * LANES] = ot.T


def _attn(q, iq, ikf, ikb, kx, vx, k_sel):
    B, Lp, _ = q.shape
    C = _pick_tile(Lp, (640, 512, 256, 128))
    nch = Lp // C
    ltri = jnp.asarray(np.tril(np.ones((C, C), np.float32), -1), BF16)
    vt = vx.reshape(B, ATT_KV_HEADS, nch, C, LANES).transpose(0, 1, 2, 4, 3)
    qspec = lambda w: pl.BlockSpec((1, Q_BLOCK, w), lambda b, j: (b, j, 0))
    return pl.pallas_call(
        functools.partial(_attn_kernel, C=C, k_sel=k_sel),
        grid=(B, Lp // Q_BLOCK),
        in_specs=[qspec(512), qspec(256), qspec(128),
                  pl.BlockSpec((1, Lp, LANES), lambda b, j: (b, 0, 0)),
                  pl.BlockSpec((1, ATT_KV_HEADS, Lp, LANES), lambda b, j: (b, 0, 0, 0)),
                  pl.BlockSpec((1, ATT_KV_HEADS, nch, LANES, C), lambda b, j: (b, 0, 0, 0, 0)),
                  pl.BlockSpec((C, C), lambda b, j: (0, 0))],
        out_specs=qspec(512),
        out_shape=jax.ShapeDtypeStruct((B, Lp, ATT_WIDTH), F32),
        scratch_shapes=[pltpu.VMEM((nch, C, Q_BLOCK), I32),
                        pltpu.VMEM((ATT_HEADS, LANES, Q_BLOCK), F32),
                        pltpu.VMEM((ATT_HEADS, 8, Q_BLOCK), F32),
                        pltpu.VMEM((2, C, Q_BLOCK), F32),
                        pltpu.VMEM((2 * ATT_HEADS, C, Q_BLOCK), F32),
                        pltpu.VMEM((2 * (ATT_HEADS // 2), C, 2 * Q_BLOCK), BF16)],
        compiler_params=pltpu.CompilerParams(
            dimension_semantics=("arbitrary", "arbitrary"), vmem_limit_bytes=VMEM_LIMIT),
        name="attn",
    )(q, iq, ikf, ikb, kx, vt, ltri)


def _hgrn_sum_matrix(C):
    nlev = int(np.log2(C))
    t = np.arange(C)[:, None]
    jj = np.arange(C)[None, :]
    mats = [(jj <= t), (jj > t)]
    for l in range(nlev):
        s = 1 << l
        mid = ((t >> (l + 1)) << (l + 1)) + s - 1
        odd = ((t >> l) & 1) == 1
        mats.append(np.where(odd, (jj > mid) & (jj <= t), (jj > t) & (jj <= mid)))
    return np.concatenate(mats, axis=0).astype(np.float32), nlev


def _hgrn_kernel(hq_ref, hf_ref, hi_ref, hg_ref, lb_ref, gn_ref, e_ref, o_ref, st_ref, *, C, nlev):
    @pl.when(pl.program_id(1) == 0)
    def _():
        st_ref[...] = jnp.zeros(st_ref.shape, F32)

    for bb in range(hq_ref.shape[0]):
        for h in range(HG_HEADS):
            _hgrn_head(hq_ref, hf_ref, hi_ref, hg_ref, lb_ref, gn_ref, e_ref, o_ref, st_ref, bb, h, C, nlev)


def _hgrn_head(hq_ref, hf_ref, hi_ref, hg_ref, lb_ref, gn_ref, e_ref, o_ref, st_ref, bb, h, C, nlev):
    cols = slice(h * HG_DK, (h + 1) * HG_DK)
    lb = lb_ref[:, cols]
    f = lb + (1.0 - lb) * jax.nn.sigmoid(hf_ref[bb, :, cols])
    logf = jnp.log(f)
    kk = 1.0 - f
    q = hq_ref[bb, :, cols] * (HG_DK ** -0.5)
    v = hi_ref[bb, :, cols].astype(BF16)

    p1 = logf.astype(BF16)
    r1 = logf - p1.astype(F32)
    p2 = r1.astype(BF16)
    p3 = (r1 - p2.astype(F32)).astype(BF16)
    x3 = _dot(e_ref[...], jnp.concatenate([p1, p2, p3], axis=1))
    xs = x3[:, :LANES] + x3[:, LANES:2 * LANES] + x3[:, 2 * LANES:]

    b = xs[0:C]
    suffix = xs[C:2 * C]
    rt = lax.broadcasted_iota(I32, (C, C), 0)
    ct = lax.broadcasted_iota(I32, (C, C), 1)
    row = lax.broadcasted_iota(I32, (C, 1), 0)

    scores = jnp.where(rt == ct, _dot_nt(q.astype(BF16), kk.astype(BF16)), 0.0)
    for l in range(nlev):
        w = jnp.exp(xs[(2 + l) * C:(3 + l) * C])
        odd = ((row >> l) & 1) == 1
        a = (jnp.where(odd, q, kk) * w).astype(BF16)
        pair = (((rt >> l) & 1) == 1) & ((ct >> l) == (rt >> l) - 1)
        scores = scores + jnp.where(pair, _dot_nt(a, a), 0.0)

    st = st_ref[bb * HG_HEADS + h]
    qb = (q * jnp.exp(b)).astype(BF16)
    o = _dot(scores.astype(BF16), v) + _dot_nt(qb, st.astype(BF16))
    ks = (kk * jnp.exp(suffix)).astype(BF16)
    st_ref[bb * HG_HEADS + h] = jnp.exp(b[C - 1:C, :]) * st + _dot_tn(v, ks)

    o = o * lax.rsqrt(jnp.mean(o * o, axis=-1, keepdims=True) + EPS) * gn_ref[:, cols]
    hg = hg_ref[bb, :, cols]
    o_ref[bb, :, cols] = o * (hg * jax.nn.sigmoid(hg))


def _hgrn(hq, hf, hi, hg, lb, gn):
    B, Lp, _ = hq.shape
    C = HG_CHUNK
    e_np, nlev = _hgrn_sum_matrix(C)
    e = jnp.asarray(e_np, BF16)
    nb = 2 if B % 2 == 0 else 1
    blk = pl.BlockSpec((nb, C, HG_WIDTH), lambda b, c: (b, c, 0))
    vec = pl.BlockSpec((1, HG_WIDTH), lambda b, c: (0, 0))
    return pl.pallas_call(
        functools.partial(_hgrn_kernel, C=C, nlev=nlev),
        grid=(B // nb, Lp // C),
        in_specs=[blk, blk, blk, blk, vec, vec,
                  pl.BlockSpec(e.shape, lambda b, c: (0, 0))],
        out_specs=blk,
        out_shape=jax.ShapeDtypeStruct((B, Lp, HG_WIDTH), F32),
        scratch_shapes=[pltpu.VMEM((nb * HG_HEADS, HG_DV, HG_DK), F32)],
        compiler_params=pltpu.CompilerParams(
            dimension_semantics=("arbitrary", "arbitrary"), vmem_limit_bytes=VMEM_LIMIT),
        name="hgrn",
    )(hq, hf, hi, hg, lb.reshape(1, -1), gn.reshape(1, -1), e)


def _merge_kernel(att_ref, hgr_ref, ga_ref, gh_ref, h_ref, wa_ref, wh_ref, wo_ref, g_ref,
                  wr_ref, br_ref, lt_ref,
                  h2_ref, xrow_ref, topi_ref, gate_ref, rank_ref, cnt_ref, *, tm):
    mix = (jax.nn.sigmoid(ga_ref[0]) * _dot(att_ref[0].astype(BF16), wa_ref[...])
           + jax.nn.sigmoid(gh_ref[0]) * _dot(hgr_ref[0].astype(BF16), wh_ref[...]))
    h2 = h_ref[0] + _dot(mix.astype(BF16), wo_ref[...])
    h2_ref[0] = h2
    xn = h2 * lax.rsqrt(jnp.mean(h2 * h2, axis=-1, keepdims=True) + EPS) * g_ref[...]
    for k in range(D_MODEL // LANES):
        xrow_ref[pl.ds(k, tm, stride=8), :] = xn[:, k * LANES:(k + 1) * LANES]

    xh = xn.astype(BF16)
    xl = (xn - xh.astype(F32)).astype(BF16)
    l1 = _dot(xh, wr_ref[...])
    logits = l1[:, :LANES] + l1[:, LANES:] + _dot(xl, wr_ref[:, :LANES]) + br_ref[...]

    lane = _lane_iota((tm, LANES))
    lanef = lane.astype(F32)
    work = logits
    vals, sels, idxs = [], [], []
    for _ in range(TOP_K):
        m = jnp.max(work, axis=1, keepdims=True)
        idx = jnp.min(jnp.where(work == m, lanef, float(LANES)), axis=1, keepdims=True)
        sel = lanef == idx
        vals.append(m)
        idxs.append(idx)
        sels.append(sel)
        work = jnp.where(sel, NEG_INF, work)
    es = [jnp.exp(v - vals[0]) for v in vals]
    den = es[0] + es[1] + es[2] + es[3]

    member = jnp.zeros((tm, LANES), F32)
    for sel in sels:
        member = member + jnp.where(sel, 1.0, 0.0)
    before = _dot(lt_ref[...], member.astype(BF16))
    topi = jnp.zeros((tm, LANES), F32)
    gate = jnp.zeros((tm, LANES), F32)
    rank = jnp.zeros((tm, LANES), F32)
    for jx in range(TOP_K):
        here = lane == jx
        topi = jnp.where(here, idxs[jx], topi)
        gate = jnp.where(here, es[jx] / den, gate)
        rank = jnp.where(here, jnp.sum(jnp.where(sels[jx], before, 0.0), axis=1, keepdims=True), rank)
    topi_ref[0] = topi.astype(I32)
    gate_ref[0] = gate
    rank_ref[0] = rank.astype(I32)
    cnt_ref[0] = jnp.sum(member, axis=0, keepdims=True).astype(I32)


def _merge(att, hgr, ga, gh, h, wa, wh, wo, g, wr2, br):
    B, Lp, D = h.shape
    tm = _pick_tile(Lp, (320, 256, 128))
    nt = Lp // tm
    lt = jnp.asarray(np.tril(np.ones((tm, tm), np.float32), -1), BF16)
    row = lambda w: pl.BlockSpec((1, tm, w), lambda b, i: (b, i, 0))
    full = lambda a: pl.BlockSpec(a.shape, lambda b, i: (0,) * a.ndim)
    g2 = g.reshape(1, D)
    out_shape = [jax.ShapeDtypeStruct((B, Lp, D), F32),
                 jax.ShapeDtypeStruct((B * Lp * 8, LANES), F32),
                 jax.ShapeDtypeStruct((B, Lp, LANES), I32),
                 jax.ShapeDtypeStruct((B, Lp, LANES), F32),
                 jax.ShapeDtypeStruct((B, Lp, LANES), I32),
                 jax.ShapeDtypeStruct((B * nt, 1, LANES), I32)]
    out_specs = [row(D),
                 pl.BlockSpec((tm * 8, LANES), lambda b, i: (b * nt + i, 0)),
                 row(LANES), row(LANES), row(LANES),
                 pl.BlockSpec((1, 1, LANES), lambda b, i: (b * nt + i, 0, 0))]
    return pl.pallas_call(
        functools.partial(_merge_kernel, tm=tm),
        grid=(B, nt),
        in_specs=[row(512), row(512), row(D), row(D), row(D),
                  full(wa), full(wh), full(wo), full(g2), full(wr2), full(br), full(lt)],
        out_specs=out_specs,
        out_shape=out_shape,
        compiler_params=pltpu.CompilerParams(
            dimension_semantics=("arbitrary", "arbitrary"), vmem_limit_bytes=VMEM_LIMIT),
        name="merge",
    )(att, hgr, ga, gh, h, wa, wh, wo, g2, wr2, br, lt), tm


def _slots_kernel(topi_ref, rank_ref, base_ref, slot_ref):
    topi = topi_ref[0]
    rank = rank_ref[0]
    base = base_ref[0]
    lane = _lane_iota(topi.shape)
    slot = jnp.zeros(topi.shape, I32)
    for jx in range(TOP_K):
        mine = lane == topi[:, jx:jx + 1]
        first = jnp.sum(jnp.where(mine, base, 0), axis=1, keepdims=True)
        slot = jnp.where(lane == jx, first + rank[:, jx:jx + 1], slot)
    slot_ref[0] = slot


def _slots(topi, rank, base, tm):
    B, Lp, _ = topi.shape
    nt = Lp // tm
    row = pl.BlockSpec((1, tm, LANES), lambda b, i: (b, i, 0))
    return pl.pallas_call(
        _slots_kernel,
        grid=(B, nt),
        in_specs=[row, row, pl.BlockSpec((1, 1, LANES), lambda b, i: (b * nt + i, 0, 0))],
        out_specs=row,
        out_shape=jax.ShapeDtypeStruct((B, Lp, LANES), I32),
        compiler_params=pltpu.CompilerParams(dimension_semantics=("arbitrary", "arbitrary")),
        name="slots",
    )(topi, rank, base)


def _dispatch_kernel(pad_lo_ref, pad_n_ref, nu_ref, slot_ref, x_ref, xs_hbm, zblk_ref, sem, *, tm, blk, nblk):
    @pl.when(pl.program_id(0) == 0)
    def _():
        zblk_ref[...] = jnp.zeros(zblk_ref.shape, F32)

        def zblock(b):
            return pltpu.make_async_copy(zblk_ref, xs_hbm.at[pl.ds(b * blk, blk)], sem)

        def bissue(b, c):
            zblock(b).start()
            return c

        def bdrain(b, c):
            zblock(b).wait()
            return c

        lax.fori_loop(nu_ref[0], nblk, bissue, 0)
        lax.fori_loop(nu_ref[0], nblk, bdrain, 0)

        def per_expert(e, c):
            def zcopy(r):
                return pltpu.make_async_copy(zblk_ref.at[0], xs_hbm.at[pad_lo_ref[e] + r], sem)

            def zissue(r, c2):
                zcopy(r).start()
                return c2

            def zdrain(r, c2):
                zcopy(r).wait()
                return c2

            lax.fori_loop(0, pad_n_ref[e], zissue, 0)
            lax.fori_loop(0, pad_n_ref[e], zdrain, 0)
            return c

        lax.fori_loop(0, N_EXPERTS, per_expert, 0)

    def copy(t, jx):
        return pltpu.make_async_copy(x_ref.at[t], xs_hbm.at[slot_ref[0, 0, t * TOP_K + jx]], sem)

    def issue(t, c):
        for jx in range(TOP_K):
            copy(t, jx).start()
        return c

    def drain(t, c):
        for jx in range(TOP_K):
            copy(t, jx).wait()
        return c

    lax.fori_loop(0, tm, issue, 0)
    lax.fori_loop(0, tm, drain, 0)


def _dispatch(pad_lo, pad_n, n_used, slot_s, xrow, cap, tm, blk):
    nt = slot_s.shape[0]
    return pl.pallas_call(
        functools.partial(_dispatch_kernel, tm=tm, blk=blk, nblk=cap // blk),
        grid_spec=pltpu.PrefetchScalarGridSpec(
            num_scalar_prefetch=3, grid=(nt,),
            in_specs=[pl.BlockSpec((1, 1, tm * TOP_K), lambda i, lo, n, nu: (i, 0, 0), memory_space=pltpu.SMEM),
                      pl.BlockSpec((tm, 8, LANES), lambda i, lo, n, nu: (i, 0, 0))],
            out_specs=pl.BlockSpec(memory_space=pl.ANY),
            scratch_shapes=[pltpu.VMEM((blk, 8, LANES), F32), pltpu.SemaphoreType.DMA(())]),
        out_shape=jax.ShapeDtypeStruct((cap, 8, LANES), F32),
        compiler_params=pltpu.CompilerParams(dimension_semantics=("arbitrary",)),
        name="dispatch",
    )(pad_lo, pad_n, n_used, slot_s, xrow)


def _ffn_kernel(be_ref, nu_ref, x_ref, wg_ref, bg_ref, wu_ref, bu_ref, wd_ref, bd_ref, o_ref,
                wgb_ref, wub_ref, wdb_ref, *, blk):
    i = pl.program_id(0)
    nk = D_MODEL // LANES
    used = i < nu_ref[0]
    new_expert = (i == 0) | (be_ref[i] != be_ref[jnp.maximum(i - 1, 0)])

    @pl.when(used & new_expert)
    def _():
        rows = 128

        def cast(r, c):
            sl = pl.ds(pl.multiple_of(r * rows, rows), rows)
            wgb_ref[sl, :] = wg_ref[0, sl, :].astype(BF16)
            wub_ref[sl, :] = wu_ref[0, sl, :].astype(BF16)
            wdb_ref[sl, :] = wd_ref[0, sl, :].astype(BF16)
            return c

        lax.fori_loop(0, D_MODEL // rows, cast, 0)

    @pl.when(used)
    def _():
        xb = jnp.concatenate([x_ref[pl.ds(k, blk, stride=8), :] for k in range(nk)], axis=1).astype(BF16)
        g = _dot(xb, wgb_ref[...]) + bg_ref[0]
        u = _dot(xb, wub_ref[...]) + bu_ref[0]
        g = jnp.minimum(g, SWIGLU_LIMIT)
        u = jnp.clip(u, -SWIGLU_LIMIT, SWIGLU_LIMIT)
        act = (u + 1.0) * (g * jax.nn.sigmoid(SWIGLU_ALPHA * g))
        out = _dot(act.astype(BF16), wdb_ref[...]) + bd_ref[0]
        for k in range(nk):
            o_ref[pl.ds(k, blk, stride=8), :] = out[:, k * LANES:(k + 1) * LANES]

    @pl.when(i >= nu_ref[0])
    def _():
        o_ref[...] = jnp.zeros(o_ref.shape, F32)


def _ffn(block_expert, n_used, xs, wg, bg, wu, bu, wd, bd, blk):
    cap = xs.shape[0]
    nblk = cap // blk
    x2 = xs.reshape(cap * 8, LANES)
    wspec = pl.BlockSpec((1, D_MODEL, D_FF), lambda i, be, nu: (be[i], 0, 0))
    wdspec = pl.BlockSpec((1, D_FF, D_MODEL), lambda i, be, nu: (be[i], 0, 0))
    bspec = lambda w: pl.BlockSpec((1, 1, w), lambda i, be, nu: (be[i], 0, 0))
    rows = pl.BlockSpec((blk * 8, LANES), lambda i, be, nu: (i, 0))
    rows_in = pl.BlockSpec((blk * 8, LANES), lambda i, be, nu: (jnp.minimum(i, nu[0] - 1), 0))
    assert D_FF == D_MODEL
    wscratch = pltpu.VMEM((D_MODEL, D_FF), BF16)
    out = pl.pallas_call(
        functools.partial(_ffn_kernel, blk=blk),
        grid_spec=pltpu.PrefetchScalarGridSpec(
            num_scalar_prefetch=2, grid=(nblk,),
            in_specs=[rows_in, wspec, bspec(D_FF), wspec, bspec(D_FF), wdspec, bspec(D_MODEL)],
            out_specs=rows,
            scratch_shapes=[wscratch, wscratch, wscratch]),
        out_shape=jax.ShapeDtypeStruct((cap * 8, LANES), F32),
        compiler_params=pltpu.CompilerParams(
            dimension_semantics=("arbitrary",), vmem_limit_bytes=VMEM_LIMIT),
        name="ffn",
    )(block_expert, n_used, x2, wg, bg, wu, bu, wd, bd)
    return out.reshape(cap, 8, LANES)


def _combine_kernel(slot_ref, gate_ref, h2_ref, g_ref, ys_hbm, o_ref, buf_ref, sem, *, tm):
    def copy(t, jx):
        dst = buf_ref.at[pl.ds(pl.multiple_of((jx * tm + t) * 8, 8), 8), :]
        return pltpu.make_async_copy(ys_hbm.at[slot_ref[0, 0, t * TOP_K + jx]], dst, sem)

    def issue(t, c):
        for jx in range(TOP_K):
            copy(t, jx).start()
        return c

    def drain(t, c):
        for jx in range(TOP_K):
            copy(t, jx).wait()
        return c

    lax.fori_loop(0, tm, issue, 0)
    lax.fori_loop(0, tm, drain, 0)

    gate = gate_ref[0]
    h2 = h2_ref[0]
    cols = []
    for k in range(D_MODEL // LANES):
        y = h2[:, k * LANES:(k + 1) * LANES]
        for jx in range(TOP_K):
            y = y + gate[:, jx:jx + 1] * buf_ref[pl.ds(jx * tm * 8 + k, tm, stride=8), :]
        cols.append(y)
    ho = jnp.concatenate(cols, axis=1)
    o_ref[0] = ho * lax.rsqrt(jnp.mean(ho * ho, axis=-1, keepdims=True) + EPS) * g_ref[...]


def _combine(slot_s, gate, h2, g, ys, tm):
    B, Lp, D = h2.shape
    nt = Lp // tm
    row = lambda w: pl.BlockSpec((1, tm, w), lambda b, i: (b, i, 0))
    return pl.pallas_call(
        functools.partial(_combine_kernel, tm=tm),
        grid=(B, nt),
        in_specs=[pl.BlockSpec((1, 1, tm * TOP_K), lambda b, i: (b * nt + i, 0, 0), memory_space=pltpu.SMEM),
                  row(LANES), row(D),
                  pl.BlockSpec((1, D), lambda b, i: (0, 0)),
                  pl.BlockSpec(memory_space=pl.ANY)],
        out_specs=row(D),
        out_shape=jax.ShapeDtypeStruct((B, Lp, D), F32),
        scratch_shapes=[pltpu.VMEM((TOP_K * tm * 8, LANES), F32), pltpu.SemaphoreType.DMA(())],
        compiler_params=pltpu.CompilerParams(
            dimension_semantics=("arbitrary", "arbitrary"), vmem_limit_bytes=VMEM_LIMIT),
        name="combine",
    )(slot_s, gate, h2, g.reshape(1, D), ys)


def _moe(h2, xrow, topi, gate, rank, cnt, tm, ffn_w, final_g):
    B, Lp, D = h2.shape
    T = B * Lp
    nt_total = cnt.shape[0]
    blk = FFN_BLOCK
    nblk = -(-(T * TOP_K) // blk) + N_EXPERTS
    cap = nblk * blk
    cnt2 = cnt.reshape(nt_total, LANES)
    totals = jnp.sum(cnt2, axis=0)
    padded = (totals + blk - 1) // blk * blk
    pad_ends = jnp.cumsum(padded)
    pad_starts = pad_ends - padded
    base = (pad_starts[None, :] + jnp.cumsum(cnt2, axis=0) - cnt2).astype(I32).reshape(nt_total, 1, LANES)
    block_start = (jnp.arange(nblk) * blk)[:, None]
    block_expert = jnp.minimum(
        jnp.sum(pad_ends[None, :N_EXPERTS] <= block_start, axis=1), N_EXPERTS - 1).astype(I32)
    n_used = (pad_ends[N_EXPERTS - 1] // blk).astype(I32).reshape(1)
    slot = _slots(topi, rank, base, tm)
    slot_s = slot[:, :, :TOP_K].reshape(nt_total, 1, tm * TOP_K)

    pad_lo = (pad_starts + totals)[:N_EXPERTS].astype(I32)
    pad_n = (padded - totals)[:N_EXPERTS].astype(I32)
    xs = _dispatch(pad_lo, pad_n, n_used, slot_s, xrow.reshape(T, 8, LANES), cap, tm, blk)
    ys = _ffn(block_expert, n_used, xs, *ffn_w, blk)
    return _combine(slot_s, gate, h2, final_g, ys, tm)


def kernel(x, meta_tokens, attn_norm_g, w_in, hgrn_norm_g, w_up_attn, w_up_hgrn, w_out,
           hgrn_lb_logits, ffn_norm_g, router_w, router_b, w_gate, b_gate, w_up, b_up,
           w_down, b_down, final_norm_g):
    B, S, D = x.shape
    L = S + N_META
    Lp = -(-L // Q_BLOCK) * Q_BLOCK
    k_sel = min(TOPK_MAX, L // 4)
    meta = jnp.broadcast_to(meta_tokens[None].astype(x.dtype), (B, N_META, D))
    h = jnp.concatenate([meta, x, jnp.zeros((B, Lp - L, D), x.dtype)], axis=1)
    lb = jnp.cumsum(jax.nn.softmax(hgrn_lb_logits.astype(F32), axis=0), axis=0)[0]

    (q, kx, vx, iq, ikf, ikb, hq, hf, hi, hg, ga, gh) = _inproj(h, attn_norm_g[0], _arrange_w_in(w_in[0]))
    att = _attn(q, iq, ikf, ikb, kx, vx, k_sel)
    hgr = _hgrn(hq, hf, hi, hg, lb, hgrn_norm_g[0])

    wr = jnp.pad(router_w[0], ((0, 0), (0, LANES - N_EXPERTS)))
    wr_hi = wr.astype(BF16)
    wr_lo = (wr - wr_hi.astype(F32)).astype(BF16)
    br = jnp.pad(router_b[0], (0, LANES - N_EXPERTS), constant_values=-1e30).reshape(1, LANES)
    (h2, xrow, topi, gate, rank, cnt), tm = _merge(
        att, hgr, ga, gh, h, w_up_attn[0].astype(BF16), w_up_hgrn[0].astype(BF16), w_out[0].astype(BF16),
        ffn_norm_g[0], jnp.concatenate([wr_hi, wr_lo], axis=1), br)

    ffn_w = (w_gate[0], b_gate[0].reshape(N_EXPERTS, 1, D_FF),
             w_up[0], b_up[0].reshape(N_EXPERTS, 1, D_FF),
             w_down[0], b_down[0].reshape(N_EXPERTS, 1, D_MODEL))
    out = _moe(h2, xrow, topi, gate, rank, cnt, tm, ffn_w, final_norm_g)
    return out[:, N_META:L]
```

```python
import functools

import numpy as np
import jax
import jax.numpy as jnp
from jax import lax
from jax.experimental import pallas as pl
from jax.experimental.pallas import tpu as pltpu

F32 = jnp.float32
BF16 = jnp.bfloat16
I32 = jnp.int32

D_MODEL = 1024
N_META = 16
EPS = 1e-5
ROPE_THETA = 500000.0
ATT_HEADS = 8
ATT_KV_HEADS = 2
ATT_HEAD_DIM = 64
ATT_WIDTH = ATT_HEADS * ATT_HEAD_DIM
IDX_HEADS = 8
IDX_DIM = 32
TOPK_MAX = 256
Q_BLOCK = 128
HG_HEADS = 4
HG_DK = 128
HG_DV = 128
HG_WIDTH = HG_HEADS * HG_DV
N_EXPERTS = 32
TOP_K = 4
D_FF = 1024
SWIGLU_LIMIT = 7.0
SWIGLU_ALPHA = 1.702

LANES = 128
VMEM_LIMIT = 56 * 1024 * 1024
INT_MIN = -2 ** 31
NEG_INF = float("-inf")
LOG2E = 1.4426950408889634

SEARCH_CHECK_BIT = 25
HG_CHUNK = 128
FFN_BLOCK = 256


def _pick_tile(n, candidates):
    for c in candidates:
        if n % c == 0:
            return c
    raise ValueError(f"no tile for {n}")


def _dot(a, b):
    return jnp.dot(a, b, preferred_element_type=F32)


def _dot_nt(a, b):
    return lax.dot_general(a, b, (((1,), (1,)), ((), ())), preferred_element_type=F32)


def _dot_tn(a, b):
    return lax.dot_general(a, b, (((0,), (0,)), ((), ())), preferred_element_type=F32)


def _lane_iota(shape):
    return lax.broadcasted_iota(I32, shape, len(shape) - 1)


def _rope_tables(n_pos, head_dim, lane_valid):
    rot = head_dim // 4
    half = rot // 2
    inv = ROPE_THETA ** (-jnp.arange(half, dtype=F32) * 2.0 / rot)
    ang = jnp.arange(n_pos).astype(F32)[:, None] * inv[None, :]
    c = jnp.cos(ang)
    s = jnp.sin(ang)
    lane = np.arange(LANES)
    within = lane % head_dim
    valid = lane < lane_valid
    first = (within < half) & valid
    second = (within >= half) & (within < rot) & valid
    idx = np.where(within < half, within, within - half) % half
    cc = c[:, idx]
    ss = s[:, idx]
    tc = jnp.where((first | second)[None, :], cc, 1.0)
    t1 = jnp.where(first[None, :], -ss, 0.0)
    t2 = jnp.where(second[None, :], ss, 0.0)
    return jnp.stack([tc, t1, t2]).astype(F32), half


def _rope_apply(x, tab_ref, half):
    c = tab_ref[0]
    s1 = tab_ref[1]
    s2 = tab_ref[2]
    cols = []
    for j in range(x.shape[1] // LANES):
        xc = x[:, j * LANES:(j + 1) * LANES]
        cols.append(xc * c + pltpu.roll(xc, LANES - half, 1) * s1 + pltpu.roll(xc, half, 1) * s2)
    return cols[0] if len(cols) == 1 else jnp.concatenate(cols, axis=1)


_W_Q = (0, 512)
_W_KX = (512, 768)
_W_VX = (768, 1024)
_W_IQ = (1024, 1280)
_W_IK = (1280, 1408)
_W_HQ = (1408, 1920)
_W_HF = (1920, 2432)
_W_HI = (2432, 2944)
_W_HG = (2944, 3456)
_W_GA = (3456, 4480)
_W_GH = (4480, 5504)
_W_COLS = 5504


def _arrange_w_in(w_in):
    o = np.cumsum([0, 512, 128, 128, 256, 32, 8, 512, 512, 512, 512, 1024, 1024])
    seg = [w_in[:, o[i]:o[i + 1]] for i in range(12)]
    q, k, v, iq, ik, iw, hq, hf, hi, hg, ga, gh = seg
    z64 = jnp.zeros((D_MODEL, 64), w_in.dtype)
    z88 = jnp.zeros((D_MODEL, 88), w_in.dtype)
    kx = jnp.concatenate([k[:, :64], z64, k[:, 64:], z64], axis=1)
    vx = jnp.concatenate([v[:, :64], z64, v[:, 64:], z64], axis=1)
    ikx = jnp.concatenate([ik, iw, z88], axis=1)
    w = jnp.concatenate([q, kx, vx, iq, ikx, hq, hf, hi, hg, ga, gh], axis=1)
    assert w.shape[1] == _W_COLS
    return w.astype(BF16)


def _inproj_kernel(h_ref, g_ref, w_ref, tatt_ref, tkx_ref, tiq_ref, tik_ref,
                   q_ref, kx_ref, vx_ref, iq_ref, ikf_ref, ikb_ref,
                   hq_ref, hf_ref, hi_ref, hg_ref, ga_ref, gh_ref, *, half_att, half_idx):
    x = h_ref[0]
    xn = x * lax.rsqrt(jnp.mean(x * x, axis=-1, keepdims=True) + EPS) * g_ref[...]
    xb = xn.astype(BF16)

    def proj(rng):
        return _dot(xb, w_ref[:, rng[0]:rng[1]])

    q_ref[0] = _rope_apply(proj(_W_Q), tatt_ref, half_att)
    kx = _rope_apply(proj(_W_KX), tkx_ref, half_att).astype(BF16)
    kx_ref[0, 0] = kx[:, :LANES]
    kx_ref[0, 1] = kx[:, LANES:]
    vx = proj(_W_VX)
    ones_hi = jnp.where(_lane_iota(vx.shape) % LANES >= ATT_HEAD_DIM, 1.0, 0.0)
    vx = (vx + ones_hi).astype(BF16)
    vx_ref[0, 0] = vx[:, :LANES]
    vx_ref[0, 1] = vx[:, LANES:]
    iq_ref[0] = _rope_apply(proj(_W_IQ), tiq_ref, half_idx)
    ik = _rope_apply(proj(_W_IK), tik_ref, half_idx)
    ikf_ref[0] = ik
    ikb_ref[0] = ik.astype(BF16)
    hq_ref[0] = proj(_W_HQ)
    hf_ref[0] = proj(_W_HF)
    hi_ref[0] = proj(_W_HI)
    hg_ref[0] = proj(_W_HG)
    ga_ref[0] = proj(_W_GA)
    gh_ref[0] = proj(_W_GH)


def _inproj(h, g, w_arr):
    B, Lp, D = h.shape
    tm = _pick_tile(Lp, (320, 256, 128))
    tatt, half_att = _rope_tables(Lp, ATT_HEAD_DIM, LANES)
    tkx, _ = _rope_tables(Lp, ATT_HEAD_DIM, ATT_HEAD_DIM)
    tiq, half_idx = _rope_tables(Lp, IDX_DIM, LANES)
    tik, _ = _rope_tables(Lp, IDX_DIM, IDX_DIM)

    def rows(width, dtype=F32):
        return (jax.ShapeDtypeStruct((B, Lp, width), dtype),
                pl.BlockSpec((1, tm, width), lambda b, i: (b, i, 0)))

    def kv(dtype=BF16):
        return (jax.ShapeDtypeStruct((B, ATT_KV_HEADS, Lp, LANES), dtype),
                pl.BlockSpec((1, ATT_KV_HEADS, tm, LANES), lambda b, i: (b, 0, i, 0)))

    outs = [rows(512), kv(), kv(), rows(256), rows(128), rows(128, BF16),
            rows(512), rows(512), rows(512), rows(512), rows(1024), rows(1024)]
    tab_spec = pl.BlockSpec((3, tm, LANES), lambda b, i: (0, i, 0))
    return pl.pallas_call(
        functools.partial(_inproj_kernel, half_att=half_att, half_idx=half_idx),
        grid=(B, Lp // tm),
        in_specs=[pl.BlockSpec((1, tm, D), lambda b, i: (b, i, 0)),
                  pl.BlockSpec((1, D), lambda b, i: (0, 0)),
                  pl.BlockSpec((D, _W_COLS), lambda b, i: (0, 0)),
                  tab_spec, tab_spec, tab_spec, tab_spec],
        out_specs=[o[1] for o in outs],
        out_shape=[o[0] for o in outs],
        compiler_params=pltpu.CompilerParams(
            dimension_semantics=("arbitrary", "arbitrary"), vmem_limit_bytes=VMEM_LIMIT),
        name="inproj",
    )(h, g.reshape(1, D), w_arr, tatt, tkx, tiq, tik)


def _attn_kernel(q_ref, iq_ref, ikf_ref, ikb_ref, kx_ref, vt_ref, ltri_ref, o_ref,
                 keys_ref, acc_ref, m_ref, bias_ref, lg_ref, p_ref, *, C, k_sel):
    j = pl.program_id(1)
    nck = ((j + 1) * Q_BLOCK + C - 1) // C
    lane = _lane_iota((Q_BLOCK, LANES))
    qpos = j * Q_BLOCK + lax.broadcasted_iota(I32, (1, Q_BLOCK), 1)
    nsub = C // 8
    nacc = 8

    def pairs(heads):
        return [jnp.concatenate([heads[2 * p], heads[2 * p + 1]], axis=0) for p in range(len(heads) // 2)]

    iqf = iq_ref[0]
    per_blk = LANES // IDX_DIM
    iqh = []
    for h in range(IDX_HEADS):
        blk = iqf[:, (h // per_blk) * LANES:(h // per_blk + 1) * LANES]
        off = (h % per_blk) * IDX_DIM
        if off:
            blk = pltpu.roll(blk, LANES - off, 1)
        iqh.append(jnp.where(lane < IDX_DIM, blk, 0.0).astype(BF16))
    iqpair = pairs(iqh)
    wt = ikf_ref[0].T
    wrow = [wt[IDX_DIM + h:IDX_DIM + h + 1, :] for h in range(IDX_HEADS)]

    def causal_mask(c):
        kpos = c * C + lax.broadcasted_iota(I32, (C, 1), 0)
        return kpos <= qpos

    def phase_a(c, carry):
        ks = ikb_ref[0, pl.ds(pl.multiple_of(c * C, C), C), :]
        s = jnp.zeros((C, Q_BLOCK), F32)
        for p in range(IDX_HEADS // 2):
            d2 = _dot_nt(ks, iqpair[p])
            s = (s + jnp.maximum(d2[:, :LANES], 0.0) * wrow[2 * p]
                 + jnp.maximum(d2[:, LANES:], 0.0) * wrow[2 * p + 1])
        bits = pltpu.bitcast(s, I32)
        key = jnp.where(bits < 0, bits ^ jnp.int32(0x7FFFFFFF), bits)
        keys_ref[c] = jnp.where(causal_mask(c), key, jnp.int32(INT_MIN))
        return carry

    lax.fori_loop(0, nck, phase_a, 0)

    def count(cand, strict):
        def body(c, acc):
            kc = keys_ref[c]
            hit = (kc > cand) if strict else (kc >= cand)
            return acc + jnp.sum(jnp.where(hit, 1, 0).reshape(nacc, nsub // nacc, 8, LANES), axis=1)

        acc = lax.fori_loop(0, nck, body, jnp.zeros((nacc, 8, LANES), I32))
        return jnp.sum(jnp.sum(acc, axis=0), axis=0, keepdims=True)

    def search(i, carry):
        prefix, nge = carry
        bit = lax.shift_left(jnp.int32(1), jnp.int32(31) - i)
        cand = prefix ^ bit
        cnt = count(cand, False)
        ok = cnt >= k_sel
        return jnp.where(ok, cand, prefix), jnp.where(ok, cnt, nge)

    thr, nge = lax.fori_loop(
        0, SEARCH_CHECK_BIT, search,
        (jnp.full((1, Q_BLOCK), INT_MIN, I32), jnp.zeros((1, Q_BLOCK), I32)))
    unsettled = jnp.max(jnp.where((nge == k_sel) | (qpos < k_sel), 0, 1)) > 0
    thr, nge = lax.cond(unsettled,
                        lambda: lax.fori_loop(SEARCH_CHECK_BIT, 32, search, (thr, nge)),
                        lambda: (thr, nge))
    excess = jnp.where((nge > k_sel) & (thr != INT_MIN), 1, 0)
    any_excess = jnp.max(excess) > 0
    need = lax.cond(any_excess,
                    lambda: (k_sel - count(thr, True)).astype(F32),
                    lambda: jnp.zeros((1, Q_BLOCK), F32))
    thr_lo = jnp.maximum(thr, jnp.int32(INT_MIN + 1))

    qf = q_ref[0]
    qh = []
    for h in range(ATT_HEADS):
        blk = qf[:, (h // 2) * LANES:(h // 2 + 1) * LANES]
        if h % 2:
            blk = pltpu.roll(blk, ATT_HEAD_DIM, 1)
        qh.append((jnp.where(lane < ATT_HEAD_DIM, blk, 0.0) * (LOG2E * ATT_HEAD_DIM ** -0.5)).astype(BF16))
    qpair = pairs(qh)
    kv_of_pair = [(2 * p) // (ATT_HEADS // ATT_KV_HEADS) for p in range(ATT_HEADS // 2)]

    acc_ref[...] = jnp.zeros(acc_ref.shape, F32)
    m_ref[...] = jnp.full(m_ref.shape, NEG_INF, F32)
    npair = ATT_HEADS // 2
    p_ref[npair:] = jnp.zeros((npair,) + p_ref.shape[1:], BF16)

    def to_mask(c, eq_seen):
        kc = keys_ref[c]

        def bias_simple():
            return jnp.where(kc >= thr_lo, 0.0, NEG_INF), eq_seen

        def bias_ties():
            eq = (kc == thr) & causal_mask(c)
            eqf = jnp.where(eq, 1.0, 0.0)
            seen = eq_seen
            ranks = []
            for t in range(C // LANES):
                part = eqf[t * LANES:(t + 1) * LANES]
                ranks.append(seen + _dot(ltri_ref[...], part.astype(BF16)))
                seen = seen + jnp.sum(part, axis=0, keepdims=True)
            rank = jnp.concatenate(ranks, axis=0)
            take = (kc > thr) | (eq & (rank < need))
            return jnp.where(take, 0.0, NEG_INF), seen

        bias, eq_next = lax.cond(any_excess, bias_ties, bias_simple)
        keys_ref[c] = pltpu.bitcast(bias, I32)
        return eq_next

    lax.fori_loop(0, nck, to_mask, jnp.zeros((1, Q_BLOCK), F32))

    def logits(c_any, slot):
        c = jnp.minimum(c_any, nck - 1)
        bias = pltpu.bitcast(keys_ref[c], F32) + jnp.where(c_any < nck, 0.0, NEG_INF)
        bias_ref[slot] = bias
        start = pl.multiple_of(c * C, C)
        for p in range(npair):
            lg2 = _dot_nt(kx_ref[0, kv_of_pair[p], pl.ds(start, C), :], qpair[p])
            lg_ref[slot * ATT_HEADS + 2 * p] = lg2[:, :LANES] + bias_ref[slot]
            lg_ref[slot * ATT_HEADS + 2 * p + 1] = lg2[:, LANES:] + bias_ref[slot]

    def weighted_values(c_any, slot, alpha):
        c = jnp.clip(c_any, 0, nck - 1)
        for p in range(npair):
            o2 = _dot(vt_ref[0, kv_of_pair[p], c], p_ref[slot * npair + p])
            for s in range(2):
                h = 2 * p + s
                acc_ref[h] = alpha[h:h + 1] * acc_ref[h] + o2[:, s * LANES:(s + 1) * LANES]

    def softmax(slot):
        shifts, alphas = [], []
        for h in range(ATT_HEADS):
            m_old = m_ref[h]
            mc = jnp.max(lg_ref[slot * ATT_HEADS + h].reshape(nsub, 8, LANES), axis=0)
            m_new = jnp.maximum(m_old, jnp.max(mc, axis=0, keepdims=True))
            m_ref[h] = m_new
            m_safe = jnp.where(m_new == NEG_INF, 0.0, m_new)
            shifts.append(m_safe[:1])
            alphas.append(jnp.exp2(m_old - m_safe)[:1])
        for h in range(ATT_HEADS):
            p_ref[slot * npair + h // 2, :, (h % 2) * LANES:(h % 2 + 1) * LANES] = (
                jnp.exp2(lg_ref[slot * ATT_HEADS + h] - shifts[h]).astype(BF16))
        return jnp.concatenate(alphas, axis=0)

    def step(c, cur, alpha_prev):
        nxt = 1 - cur
        weighted_values(c - 1, nxt, alpha_prev)
        logits(c + 1, nxt)
        return softmax(cur)

    def phase_c(i, alpha_prev):
        return step(2 * i + 1, 1, step(2 * i, 0, alpha_prev))

    logits(0, 0)
    alpha_last = lax.fori_loop(0, (nck + 1) // 2, phase_c, jnp.ones((ATT_HEADS, Q_BLOCK), F32))
    weighted_values(2 * ((nck + 1) // 2) - 1, 1, alpha_last)

    for p in range(ATT_HEADS // 2):
        a = acc_ref[2 * p]
        b = acc_ref[2 * p + 1]
        ot = jnp.concatenate([a[:ATT_HEAD_DIM] / a[ATT_HEAD_DIM:ATT_HEAD_DIM + 1],
                              b[:ATT_HEAD_DIM] / b[ATT_HEAD_DIM:ATT_HEAD_DIM + 1]], axis=0)
        o_ref[0, :, p * LANES:(p + 1) * LANES] = ot.T


def _attn(q, iq, ikf, ikb, kx, vx, k_sel):
    B, Lp, _ = q.shape
    C = _pick_tile(Lp, (640, 512, 256, 128))
    nch = Lp // C
    ltri = jnp.asarray(np.tril(np.ones((LANES, LANES), np.float32), -1), BF16)
    vt = vx.reshape(B, ATT_KV_HEADS, nch, C, LANES).transpose(0, 1, 2, 4, 3)
    qspec = lambda w: pl.BlockSpec((1, Q_BLOCK, w), lambda b, j: (b, j, 0))
    return pl.pallas_call(
        functools.partial(_attn_kernel, C=C, k_sel=k_sel),
        grid=(B, Lp // Q_BLOCK),
        in_specs=[qspec(512), qspec(256), qspec(128),
                  pl.BlockSpec((1, Lp, LANES), lambda b, j: (b, 0, 0)),
                  pl.BlockSpec((1, ATT_KV_HEADS, Lp, LANES), lambda b, j: (b, 0, 0, 0)),
                  pl.BlockSpec((1, ATT_KV_HEADS, nch, LANES, C), lambda b, j: (b, 0, 0, 0, 0)),
                  pl.BlockSpec((LANES, LANES), lambda b, j: (0, 0))],
        out_specs=qspec(512),
        out_shape=jax.ShapeDtypeStruct((B, Lp, ATT_WIDTH), F32),
        scratch_shapes=[pltpu.VMEM((nch, C, Q_BLOCK), I32),
                        pltpu.VMEM((ATT_HEADS, LANES, Q_BLOCK), F32),
                        pltpu.VMEM((ATT_HEADS, 8, Q_BLOCK), F32),
                        pltpu.VMEM((2, C, Q_BLOCK), F32),
                        pltpu.VMEM((2 * ATT_HEADS, C, Q_BLOCK), F32),
                        pltpu.VMEM((2 * (ATT_HEADS // 2), C, 2 * Q_BLOCK), BF16)],
        compiler_params=pltpu.CompilerParams(
            dimension_semantics=("arbitrary", "arbitrary"), vmem_limit_bytes=VMEM_LIMIT),
        name="attn",
    )(q, iq, ikf, ikb, kx, vt, ltri)


def _hgrn_sum_matrix(C):
    nlev = int(np.log2(C))
    t = np.arange(C)[:, None]
    jj = np.arange(C)[None, :]
    mats = [(jj <= t), (jj > t)]
    for l in range(nlev):
        s = 1 << l
        mid = ((t >> (l + 1)) << (l + 1)) + s - 1
        odd = ((t >> l) & 1) == 1
        mats.append(np.where(odd, (jj > mid) & (jj <= t), (jj > t) & (jj <= mid)))
    return np.concatenate(mats, axis=0).astype(np.float32), nlev


def _hgrn_kernel(hq_ref, hf_ref, hi_ref, hg_ref, lb_ref, gn_ref, e_ref, o_ref, st_ref, *, C, nlev):
    @pl.when(pl.program_id(1) == 0)
    def _():
        st_ref[...] = jnp.zeros(st_ref.shape, F32)

    for bb in range(hq_ref.shape[0]):
        for h in range(HG_HEADS):
            _hgrn_head(hq_ref, hf_ref, hi_ref, hg_ref, lb_ref, gn_ref, e_ref, o_ref, st_ref, bb, h, C, nlev)


def _hgrn_head(hq_ref, hf_ref, hi_ref, hg_ref, lb_ref, gn_ref, e_ref, o_ref, st_ref, bb, h, C, nlev):
    cols = slice(h * HG_DK, (h + 1) * HG_DK)
    lb = lb_ref[:, cols]
    f = lb + (1.0 - lb) * jax.nn.sigmoid(hf_ref[bb, :, cols])
    logf = jnp.log(f)
    kk = 1.0 - f
    q = hq_ref[bb, :, cols] * (HG_DK ** -0.5)
    v = hi_ref[bb, :, cols].astype(BF16)

    p1 = logf.astype(BF16)
    r1 = logf - p1.astype(F32)
    p2 = r1.astype(BF16)
    p3 = (r1 - p2.astype(F32)).astype(BF16)
    x3 = _dot(e_ref[...], jnp.concatenate([p1, p2, p3], axis=1))
    xs = x3[:, :LANES] + x3[:, LANES:2 * LANES] + x3[:, 2 * LANES:]

    b = xs[0:C]
    suffix = xs[C:2 * C]
    rt = lax.broadcasted_iota(I32, (C, C), 0)
    ct = lax.broadcasted_iota(I32, (C, C), 1)
    row = lax.broadcasted_iota(I32, (C, 1), 0)

    scores = jnp.where(rt == ct, _dot_nt(q.astype(BF16), kk.astype(BF16)), 0.0)
    for l in range(nlev):
        w = jnp.exp(xs[(2 + l) * C:(3 + l) * C])
        odd = ((row >> l) & 1) == 1
        a = (jnp.where(odd, q, kk) * w).astype(BF16)
        pair = (((rt >> l) & 1) == 1) & ((ct >> l) == (rt >> l) - 1)
        scores = scores + jnp.where(pair, _dot_nt(a, a), 0.0)

    st = st_ref[bb * HG_HEADS + h]
    qb = (q * jnp.exp(b)).astype(BF16)
    o = _dot(scores.astype(BF16), v) + _dot_nt(qb, st.astype(BF16))
    ks = (kk * jnp.exp(suffix)).astype(BF16)
    st_ref[bb * HG_HEADS + h] = jnp.exp(b[C - 1:C, :]) * st + _dot_tn(v, ks)

    o = o * lax.rsqrt(jnp.mean(o * o, axis=-1, keepdims=True) + EPS) * gn_ref[:, cols]
    hg = hg_ref[bb, :, cols]
    o_ref[bb, :, cols] = o * (hg * jax.nn.sigmoid(hg))


def _hgrn(hq, hf, hi, hg, lb, gn):
    B, Lp, _ = hq.shape
    C = HG_CHUNK
    e_np, nlev = _hgrn_sum_matrix(C)
    e = jnp.asarray(e_np, BF16)
    nb = 2 if B % 2 == 0 else 1
    blk = pl.BlockSpec((nb, C, HG_WIDTH), lambda b, c: (b, c, 0))
    vec = pl.BlockSpec((1, HG_WIDTH), lambda b, c: (0, 0))
    return pl.pallas_call(
        functools.partial(_hgrn_kernel, C=C, nlev=nlev),
        grid=(B // nb, Lp // C),
        in_specs=[blk, blk, blk, blk, vec, vec,
                  pl.BlockSpec(e.shape, lambda b, c: (0, 0))],
        out_specs=blk,
        out_shape=jax.ShapeDtypeStruct((B, Lp, HG_WIDTH), F32),
        scratch_shapes=[pltpu.VMEM((nb * HG_HEADS, HG_DV, HG_DK), F32)],
        compiler_params=pltpu.CompilerParams(
            dimension_semantics=("arbitrary", "arbitrary"), vmem_limit_bytes=VMEM_LIMIT),
        name="hgrn",
    )(hq, hf, hi, hg, lb.reshape(1, -1), gn.reshape(1, -1), e)


def _merge_kernel(att_ref, hgr_ref, ga_ref, gh_ref, h_ref, wa_ref, wh_ref, wo_ref, g_ref,
                  wr_ref, br_ref, lt_ref,
                  h2_ref, xrow_ref, topi_ref, gate_ref, rank_ref, cnt_ref, *, tm):
    mix = (jax.nn.sigmoid(ga_ref[0]) * _dot(att_ref[0].astype(BF16), wa_ref[...])
           + jax.nn.sigmoid(gh_ref[0]) * _dot(hgr_ref[0].astype(BF16), wh_ref[...]))
    h2 = h_ref[0] + _dot(mix.astype(BF16), wo_ref[...])
    h2_ref[0] = h2
    xn = h2 * lax.rsqrt(jnp.mean(h2 * h2, axis=-1, keepdims=True) + EPS) * g_ref[...]
    for k in range(D_MODEL // LANES):
        xrow_ref[pl.ds(k, tm, stride=8), :] = xn[:, k * LANES:(k + 1) * LANES]

    xh = xn.astype(BF16)
    xl = (xn - xh.astype(F32)).astype(BF16)
    l1 = _dot(xh, wr_ref[...])
    logits = l1[:, :LANES] + l1[:, LANES:] + _dot(xl, wr_ref[:, :LANES]) + br_ref[...]

    lane = _lane_iota((tm, LANES))
    lanef = lane.astype(F32)
    work = logits
    vals, sels, idxs = [], [], []
    for _ in range(TOP_K):
        m = jnp.max(work, axis=1, keepdims=True)
        idx = jnp.min(jnp.where(work == m, lanef, float(LANES)), axis=1, keepdims=True)
        sel = lanef == idx
        vals.append(m)
        idxs.append(idx)
        sels.append(sel)
        work = jnp.where(sel, NEG_INF, work)
    es = [jnp.exp(v - vals[0]) for v in vals]
    den = es[0] + es[1] + es[2] + es[3]

    member = jnp.zeros((tm, LANES), F32)
    for sel in sels:
        member = member + jnp.where(sel, 1.0, 0.0)
    before = _dot(lt_ref[...], member.astype(BF16))
    topi = jnp.zeros((tm, LANES), F32)
    gate = jnp.zeros((tm, LANES), F32)
    rank = jnp.zeros((tm, LANES), F32)
    for jx in range(TOP_K):
        here = lane == jx
        topi = jnp.where(here, idxs[jx], topi)
        gate = jnp.where(here, es[jx] / den, gate)
        rank = jnp.where(here, jnp.sum(jnp.where(sels[jx], before, 0.0), axis=1, keepdims=True), rank)
    topi_ref[0] = topi.astype(I32)
    gate_ref[0] = gate
    rank_ref[0] = rank.astype(I32)
    cnt_ref[0] = jnp.sum(member, axis=0, keepdims=True).astype(I32)


def _merge(att, hgr, ga, gh, h, wa, wh, wo, g, wr2, br):
    B, Lp, D = h.shape
    tm = _pick_tile(Lp, (320, 256, 128))
    nt = Lp // tm
    lt = jnp.asarray(np.tril(np.ones((tm, tm), np.float32), -1), BF16)
    row = lambda w: pl.BlockSpec((1, tm, w), lambda b, i: (b, i, 0))
    full = lambda a: pl.BlockSpec(a.shape, lambda b, i: (0,) * a.ndim)
    g2 = g.reshape(1, D)
    out_shape = [jax.ShapeDtypeStruct((B, Lp, D), F32),
                 jax.ShapeDtypeStruct((B * Lp * 8, LANES), F32),
                 jax.ShapeDtypeStruct((B, Lp, LANES), I32),
                 jax.ShapeDtypeStruct((B, Lp, LANES), F32),
                 jax.ShapeDtypeStruct((B, Lp, LANES), I32),
                 jax.ShapeDtypeStruct((B * nt, 1, LANES), I32)]
    out_specs = [row(D),
                 pl.BlockSpec((tm * 8, LANES), lambda b, i: (b * nt + i, 0)),
                 row(LANES), row(LANES), row(LANES),
                 pl.BlockSpec((1, 1, LANES), lambda b, i: (b * nt + i, 0, 0))]
    return pl.pallas_call(
        functools.partial(_merge_kernel, tm=tm),
        grid=(B, nt),
        in_specs=[row(512), row(512), row(D), row(D), row(D),
                  full(wa), full(wh), full(wo), full(g2), full(wr2), full(br), full(lt)],
        out_specs=out_specs,
        out_shape=out_shape,
        compiler_params=pltpu.CompilerParams(
            dimension_semantics=("arbitrary", "arbitrary"), vmem_limit_bytes=VMEM_LIMIT),
        name="merge",
    )(att, hgr, ga, gh, h, wa, wh, wo, g2, wr2, br, lt), tm


def _slots_kernel(topi_ref, rank_ref, base_ref, slot_ref):
    topi = topi_ref[0]
    rank = rank_ref[0]
    base = base_ref[0]
    lane = _lane_iota(topi.shape)
    slot = jnp.zeros(topi.shape, I32)
    for jx in range(TOP_K):
        mine = lane == topi[:, jx:jx + 1]
        first = jnp.sum(jnp.where(mine, base, 0), axis=1, keepdims=True)
        slot = jnp.where(lane == jx, first + rank[:, jx:jx + 1], slot)
    slot_ref[0] = slot


def _slots(topi, rank, base, tm):
    B, Lp, _ = topi.shape
    nt = Lp // tm
    row = pl.BlockSpec((1, tm, LANES), lambda b, i: (b, i, 0))
    return pl.pallas_call(
        _slots_kernel,
        grid=(B, nt),
        in_specs=[row, row, pl.BlockSpec((1, 1, LANES), lambda b, i: (b * nt + i, 0, 0))],
        out_specs=row,
        out_shape=jax.ShapeDtypeStruct((B, Lp, LANES), I32),
        compiler_params=pltpu.CompilerParams(dimension_semantics=("arbitrary", "arbitrary")),
        name="slots",
    )(topi, rank, base)


def _dispatch_kernel(pad_lo_ref, pad_n_ref, nu_ref, slot_ref, x_ref, xs_hbm, zblk_ref, sem, *, tm, blk, nblk):
    @pl.when(pl.program_id(0) == 0)
    def _():
        zblk_ref[...] = jnp.zeros(zblk_ref.shape, F32)

        def zblock(b):
            return pltpu.make_async_copy(zblk_ref, xs_hbm.at[pl.ds(b * blk, blk)], sem)

        def bissue(b, c):
            zblock(b).start()
            return c

        def bdrain(b, c):
            zblock(b).wait()
            return c

        lax.fori_loop(nu_ref[0], nblk, bissue, 0)
        lax.fori_loop(nu_ref[0], nblk, bdrain, 0)

        def per_expert(e, c):
            def zcopy(r):
                return pltpu.make_async_copy(zblk_ref.at[0], xs_hbm.at[pad_lo_ref[e] + r], sem)

            def zissue(r, c2):
                zcopy(r).start()
                return c2

            def zdrain(r, c2):
                zcopy(r).wait()
                return c2

            lax.fori_loop(0, pad_n_ref[e], zissue, 0)
            lax.fori_loop(0, pad_n_ref[e], zdrain, 0)
            return c

        lax.fori_loop(0, N_EXPERTS, per_expert, 0)

    def copy(t, jx):
        return pltpu.make_async_copy(x_ref.at[t], xs_hbm.at[slot_ref[0, 0, t * TOP_K + jx]], sem)

    def issue(t, c):
        for jx in range(TOP_K):
            copy(t, jx).start()
        return c

    def drain(t, c):
        for jx in range(TOP_K):
            copy(t, jx).wait()
        return c

    lax.fori_loop(0, tm, issue, 0)
    lax.fori_loop(0, tm, drain, 0)


def _dispatch(pad_lo, pad_n, n_used, slot_s, xrow, cap, tm, blk):
    nt = slot_s.shape[0]
    return pl.pallas_call(
        functools.partial(_dispatch_kernel, tm=tm, blk=blk, nblk=cap // blk),
        grid_spec=pltpu.PrefetchScalarGridSpec(
            num_scalar_prefetch=3, grid=(nt,),
            in_specs=[pl.BlockSpec((1, 1, tm * TOP_K), lambda i, lo, n, nu: (i, 0, 0), memory_space=pltpu.SMEM),
                      pl.BlockSpec((tm, 8, LANES), lambda i, lo, n, nu: (i, 0, 0))],
            out_specs=pl.BlockSpec(memory_space=pl.ANY),
            scratch_shapes=[pltpu.VMEM((blk, 8, LANES), F32), pltpu.SemaphoreType.DMA(())]),
        out_shape=jax.ShapeDtypeStruct((cap, 8, LANES), F32),
        compiler_params=pltpu.CompilerParams(dimension_semantics=("arbitrary",)),
        name="dispatch",
    )(pad_lo, pad_n, n_used, slot_s, xrow)


def _ffn_kernel(be_ref, nu_ref, x_ref, wg_ref, bg_ref, wu_ref, bu_ref, wd_ref, bd_ref, o_ref,
                wgb_ref, wub_ref, wdb_ref, *, blk):
    i = pl.program_id(0)
    nk = D_MODEL // LANES
    used = i < nu_ref[0]
    new_expert = (i == 0) | (be_ref[i] != be_ref[jnp.maximum(i - 1, 0)])

    @pl.when(used & new_expert)
    def _():
        rows = 128

        def cast(r, c):
            sl = pl.ds(pl.multiple_of(r * rows, rows), rows)
            wgb_ref[sl, :] = wg_ref[0, sl, :].astype(BF16)
            wub_ref[sl, :] = wu_ref[0, sl, :].astype(BF16)
            wdb_ref[sl, :] = wd_ref[0, sl, :].astype(BF16)
            return c

        lax.fori_loop(0, D_MODEL // rows, cast, 0)

    @pl.when(used)
    def _():
        xb = jnp.concatenate([x_ref[pl.ds(k, blk, stride=8), :] for k in range(nk)], axis=1).astype(BF16)
        g = _dot(xb, wgb_ref[...]) + bg_ref[0]
        u = _dot(xb, wub_ref[...]) + bu_ref[0]
        g = jnp.minimum(g, SWIGLU_LIMIT)
        u = jnp.clip(u, -SWIGLU_LIMIT, SWIGLU_LIMIT)
        act = (u + 1.0) * (g * jax.nn.sigmoid(SWIGLU_ALPHA * g))
        out = _dot(act.astype(BF16), wdb_ref[...]) + bd_ref[0]
        for k in range(nk):
            o_ref[pl.ds(k, blk, stride=8), :] = out[:, k * LANES:(k + 1) * LANES]

    @pl.when(i >= nu_ref[0])
    def _():
        o_ref[...] = jnp.zeros(o_ref.shape, F32)


def _ffn(block_expert, n_used, xs, wg, bg, wu, bu, wd, bd, blk):
    cap = xs.shape[0]
    nblk = cap // blk
    x2 = xs.reshape(cap * 8, LANES)
    wspec = pl.BlockSpec((1, D_MODEL, D_FF), lambda i, be, nu: (be[i], 0, 0))
    wdspec = pl.BlockSpec((1, D_FF, D_MODEL), lambda i, be, nu: (be[i], 0, 0))
    bspec = lambda w: pl.BlockSpec((1, 1, w), lambda i, be, nu: (be[i], 0, 0))
    rows = pl.BlockSpec((blk * 8, LANES), lambda i, be, nu: (i, 0))
    rows_in = pl.BlockSpec((blk * 8, LANES), lambda i, be, nu: (jnp.minimum(i, nu[0] - 1), 0))
    assert D_FF == D_MODEL
    wscratch = pltpu.VMEM((D_MODEL, D_FF), BF16)
    out = pl.pallas_call(
        functools.partial(_ffn_kernel, blk=blk),
        grid_spec=pltpu.PrefetchScalarGridSpec(
            num_scalar_prefetch=2, grid=(nblk,),
            in_specs=[rows_in, wspec, bspec(D_FF), wspec, bspec(D_FF), wdspec, bspec(D_MODEL)],
            out_specs=rows,
            scratch_shapes=[wscratch, wscratch, wscratch]),
        out_shape=jax.ShapeDtypeStruct((cap * 8, LANES), F32),
        compiler_params=pltpu.CompilerParams(
            dimension_semantics=("arbitrary",), vmem_limit_bytes=VMEM_LIMIT),
        name="ffn",
    )(block_expert, n_used, x2, wg, bg, wu, bu, wd, bd)
    return out.reshape(cap, 8, LANES)


def _combine_kernel(slot_ref, gate_ref, h2_ref, g_ref, ys_hbm, o_ref, buf_ref, sem, *, tm):
    def copy(t, jx):
        dst = buf_ref.at[pl.ds(pl.multiple_of((jx * tm + t) * 8, 8), 8), :]
        return pltpu.make_async_copy(ys_hbm.at[slot_ref[0, 0, t * TOP_K + jx]], dst, sem)

    def issue(t, c):
        for jx in range(TOP_K):
            copy(t, jx).start()
        return c

    def drain(t, c):
        for jx in range(TOP_K):
            copy(t, jx).wait()
        return c

    lax.fori_loop(0, tm, issue, 0)
    lax.fori_loop(0, tm, drain, 0)

    gate = gate_ref[0]
    h2 = h2_ref[0]
    cols = []
    for k in range(D_MODEL // LANES):
        y = h2[:, k * LANES:(k + 1) * LANES]
        for jx in range(TOP_K):
            y = y + gate[:, jx:jx + 1] * buf_ref[pl.ds(jx * tm * 8 + k, tm, stride=8), :]
        cols.append(y)
    ho = jnp.concatenate(cols, axis=1)
    o_ref[0] = ho * lax.rsqrt(jnp.mean(ho * ho, axis=-1, keepdims=True) + EPS) * g_ref[...]


def _combine(slot_s, gate, h2, g, ys, tm):
    B, Lp, D = h2.shape
    nt = Lp // tm
    row = lambda w: pl.BlockSpec((1, tm, w), lambda b, i: (b, i, 0))
    return pl.pallas_call(
        functools.partial(_combine_kernel, tm=tm),
        grid=(B, nt),
        in_specs=[pl.BlockSpec((1, 1, tm * TOP_K), lambda b, i: (b * nt + i, 0, 0), memory_space=pltpu.SMEM),
                  row(LANES), row(D),
                  pl.BlockSpec((1, D), lambda b, i: (0, 0)),
                  pl.BlockSpec(memory_space=pl.ANY)],
        out_specs=row(D),
        out_shape=jax.ShapeDtypeStruct((B, Lp, D), F32),
        scratch_shapes=[pltpu.VMEM((TOP_K * tm * 8, LANES), F32), pltpu.SemaphoreType.DMA(())],
        compiler_params=pltpu.CompilerParams(
            dimension_semantics=("arbitrary", "arbitrary"), vmem_limit_bytes=VMEM_LIMIT),
        name="combine",
    )(slot_s, gate, h2, g.reshape(1, D), ys)


def _moe(h2, xrow, topi, gate, rank, cnt, tm, ffn_w, final_g):
    B, Lp, D = h2.shape
    T = B * Lp
    nt_total = cnt.shape[0]
    blk = FFN_BLOCK
    nblk = -(-(T * TOP_K) // blk) + N_EXPERTS
    cap = nblk * blk
    cnt2 = cnt.reshape(nt_total, LANES)
    totals = jnp.sum(cnt2, axis=0)
    padded = (totals + blk - 1) // blk * blk
    pad_ends = jnp.cumsum(padded)
    pad_starts = pad_ends - padded
    base = (pad_starts[None, :] + jnp.cumsum(cnt2, axis=0) - cnt2).astype(I32).reshape(nt_total, 1, LANES)
    block_start = (jnp.arange(nblk) * blk)[:, None]
    block_expert = jnp.minimum(
        jnp.sum(pad_ends[None, :N_EXPERTS] <= block_start, axis=1), N_EXPERTS - 1).astype(I32)
    n_used = (pad_ends[N_EXPERTS - 1] // blk).astype(I32).reshape(1)
    slot = _slots(topi, rank, base, tm)
    slot_s = slot[:, :, :TOP_K].reshape(nt_total, 1, tm * TOP_K)

    pad_lo = (pad_starts + totals)[:N_EXPERTS].astype(I32)
    pad_n = (padded - totals)[:N_EXPERTS].astype(I32)
    xs = _dispatch(pad_lo, pad_n, n_used, slot_s, xrow.reshape(T, 8, LANES), cap, tm, blk)
    ys = _ffn(block_expert, n_used, xs, *ffn_w, blk)
    return _combine(slot_s, gate, h2, final_g, ys, tm)


def kernel(x, meta_tokens, attn_norm_g, w_in, hgrn_norm_g, w_up_attn, w_up_hgrn, w_out,
           hgrn_lb_logits, ffn_norm_g, router_w, router_b, w_gate, b_gate, w_up, b_up,
           w_down, b_down, final_norm_g):
    B, S, D = x.shape
    L = S + N_META
    Lp = -(-L // Q_BLOCK) * Q_BLOCK
    k_sel = min(TOPK_MAX, L // 4)
    meta = jnp.broadcast_to(meta_tokens[None].astype(x.dtype), (B, N_META, D))
    h = jnp.concatenate([meta, x, jnp.zeros((B, Lp - L, D), x.dtype)], axis=1)
    lb = jnp.cumsum(jax.nn.softmax(hgrn_lb_logits.astype(F32), axis=0), axis=0)[0]

    (q, kx, vx, iq, ikf, ikb, hq, hf, hi, hg, ga, gh) = _inproj(h, attn_norm_g[0], _arrange_w_in(w_in[0]))
    att = _attn(q, iq, ikf, ikb, kx, vx, k_sel)
    hgr = _hgrn(hq, hf, hi, hg, lb, hgrn_norm_g[0])

    wr = jnp.pad(router_w[0], ((0, 0), (0, LANES - N_EXPERTS)))
    wr_hi = wr.astype(BF16)
    wr_lo = (wr - wr_hi.astype(F32)).astype(BF16)
    br = jnp.pad(router_b[0], (0, LANES - N_EXPERTS), constant_values=-1e30).reshape(1, LANES)
    (h2, xrow, topi, gate, rank, cnt), tm = _merge(
        att, hgr, ga, gh, h, w_up_attn[0].astype(BF16), w_up_hgrn[0].astype(BF16), w_out[0].astype(BF16),
        ffn_norm_g[0], jnp.concatenate([wr_hi, wr_lo], axis=1), br)

    ffn_w = (w_gate[0], b_gate[0].reshape(N_EXPERTS, 1, D_FF),
             w_up[0], b_up[0].reshape(N_EXPERTS, 1, D_FF),
             w_down[0], b_down[0].reshape(N_EXPERTS, 1, D_MODEL))
    out = _moe(h2, xrow, topi, gate, rank, cnt, tm, ffn_w, final_norm_g)
    return out[:, N_META:L]
```

```python
import functools

import numpy as np
import jax
import jax.numpy as jnp
from jax import lax
from jax.experimental import pallas as pl
from jax.experimental.pallas import tpu as pltpu

F32 = jnp.float32
BF16 = jnp.bfloat16
I32 = jnp.int32

D_MODEL = 1024
N_META = 16
EPS = 1e-5
ROPE_THETA = 500000.0
ATT_HEADS = 8
ATT_KV_HEADS = 2
ATT_HEAD_DIM = 64
ATT_WIDTH = ATT_HEADS * ATT_HEAD_DIM
IDX_HEADS = 8
IDX_DIM = 32
TOPK_MAX = 256
Q_BLOCK = 128
HG_HEADS = 4
HG_DK = 128
HG_DV = 128
HG_WIDTH = HG_HEADS * HG_DV
N_EXPERTS = 32
TOP_K = 4
D_FF = 1024
SWIGLU_LIMIT = 7.0
SWIGLU_ALPHA = 1.702

LANES = 128
VMEM_LIMIT = 56 * 1024 * 1024
INT_MIN = -2 ** 31
NEG_INF = float("-inf")
LOG2E = 1.4426950408889634

SEARCH_CHECK_BIT = 25
HG_CHUNK = 128
FFN_BLOCK = 256


def _pick_tile(n, candidates):
    for c in candidates:
        if n % c == 0:
            return c
    raise ValueError(f"no tile for {n}")


def _dot(a, b):
    return jnp.dot(a, b, preferred_element_type=F32)


def _dot_nt(a, b):
    return lax.dot_general(a, b, (((1,), (1,)), ((), ())), preferred_element_type=F32)


def _dot_tn(a, b):
    return lax.dot_general(a, b, (((0,), (0,)), ((), ())), preferred_element_type=F32)


def _lane_iota(shape):
    return lax.broadcasted_iota(I32, shape, len(shape) - 1)


def _rope_tables(n_pos, head_dim, lane_valid):
    rot = head_dim // 4
    half = rot // 2
    inv = ROPE_THETA ** (-jnp.arange(half, dtype=F32) * 2.0 / rot)
    ang = jnp.arange(n_pos).astype(F32)[:, None] * inv[None, :]
    c = jnp.cos(ang)
    s = jnp.sin(ang)
    lane = np.arange(LANES)
    within = lane % head_dim
    valid = lane < lane_valid
    first = (within < half) & valid
    second = (within >= half) & (within < rot) & valid
    idx = np.where(within < half, within, within - half) % half
    cc = c[:, idx]
    ss = s[:, idx]
    tc = jnp.where((first | second)[None, :], cc, 1.0)
    t1 = jnp.where(first[None, :], -ss, 0.0)
    t2 = jnp.where(second[None, :], ss, 0.0)
    return jnp.stack([tc, t1, t2]).astype(F32), half


def _rope_apply(x, tab_ref, half):
    c = tab_ref[0]
    s1 = tab_ref[1]
    s2 = tab_ref[2]
    cols = []
    for j in range(x.shape[1] // LANES):
        xc = x[:, j * LANES:(j + 1) * LANES]
        cols.append(xc * c + pltpu.roll(xc, LANES - half, 1) * s1 + pltpu.roll(xc, half, 1) * s2)
    return cols[0] if len(cols) == 1 else jnp.concatenate(cols, axis=1)


_W_Q = (0, 512)
_W_KX = (512, 768)
_W_VX = (768, 1024)
_W_IQ = (1024, 1280)
_W_IK = (1280, 1408)
_W_HQ = (1408, 1920)
_W_HF = (1920, 2432)
_W_HI = (2432, 2944)
_W_HG = (2944, 3456)
_W_GA = (3456, 4480)
_W_GH = (4480, 5504)
_W_COLS = 5504


def _arrange_w_in(w_in):
    o = np.cumsum([0, 512, 128, 128, 256, 32, 8, 512, 512, 512, 512, 1024, 1024])
    seg = [w_in[:, o[i]:o[i + 1]] for i in range(12)]
    q, k, v, iq, ik, iw, hq, hf, hi, hg, ga, gh = seg
    z64 = jnp.zeros((D_MODEL, 64), w_in.dtype)
    z88 = jnp.zeros((D_MODEL, 88), w_in.dtype)
    kx = jnp.concatenate([k[:, :64], z64, k[:, 64:], z64], axis=1)
    vx = jnp.concatenate([v[:, :64], z64, v[:, 64:], z64], axis=1)
    ikx = jnp.concatenate([ik, iw, z88], axis=1)
    w = jnp.concatenate([q, kx, vx, iq, ikx, hq, hf, hi, hg, ga, gh], axis=1)
    assert w.shape[1] == _W_COLS
    return w.astype(BF16)


def _inproj_kernel(h_ref, g_ref, w_ref, tatt_ref, tkx_ref, tiq_ref, tik_ref,
                   q_ref, kx_ref, vx_ref, iq_ref, ikf_ref, ikb_ref,
                   hq_ref, hf_ref, hi_ref, hg_ref, ga_ref, gh_ref, *, half_att, half_idx):
    x = h_ref[0]
    xn = x * lax.rsqrt(jnp.mean(x * x, axis=-1, keepdims=True) + EPS) * g_ref[...]
    xb = xn.astype(BF16)

    def proj(rng):
        return _dot(xb, w_ref[:, rng[0]:rng[1]])

    q_ref[0] = _rope_apply(proj(_W_Q), tatt_ref, half_att)
    kx = _rope_apply(proj(_W_KX), tkx_ref, half_att).astype(BF16)
    kx_ref[0, 0] = kx[:, :LANES]
    kx_ref[0, 1] = kx[:, LANES:]
    vx = proj(_W_VX)
    ones_hi = jnp.where(_lane_iota(vx.shape) % LANES >= ATT_HEAD_DIM, 1.0, 0.0)
    vx = (vx + ones_hi).astype(BF16)
    vx_ref[0, 0] = vx[:, :LANES]
    vx_ref[0, 1] = vx[:, LANES:]
    iq_ref[0] = _rope_apply(proj(_W_IQ), tiq_ref, half_idx)
    ik = _rope_apply(proj(_W_IK), tik_ref, half_idx)
    ikf_ref[0] = ik
    ikb_ref[0] = ik.astype(BF16)
    hq_ref[0] = proj(_W_HQ)
    hf_ref[0] = proj(_W_HF)
    hi_ref[0] = proj(_W_HI)
    hg_ref[0] = proj(_W_HG)
    ga_ref[0] = proj(_W_GA)
    gh_ref[0] = proj(_W_GH)


def _inproj(h, g, w_arr):
    B, Lp, D = h.shape
    tm = _pick_tile(Lp, (320, 256, 128))
    tatt, half_att = _rope_tables(Lp, ATT_HEAD_DIM, LANES)
    tkx, _ = _rope_tables(Lp, ATT_HEAD_DIM, ATT_HEAD_DIM)
    tiq, half_idx = _rope_tables(Lp, IDX_DIM, LANES)
    tik, _ = _rope_tables(Lp, IDX_DIM, IDX_DIM)

    def rows(width, dtype=F32):
        return (jax.ShapeDtypeStruct((B, Lp, width), dtype),
                pl.BlockSpec((1, tm, width), lambda b, i: (b, i, 0)))

    def kv(dtype=BF16):
        return (jax.ShapeDtypeStruct((B, ATT_KV_HEADS, Lp, LANES), dtype),
                pl.BlockSpec((1, ATT_KV_HEADS, tm, LANES), lambda b, i: (b, 0, i, 0)))

    outs = [rows(512), kv(), kv(), rows(256), rows(128), rows(128, BF16),
            rows(512), rows(512), rows(512), rows(512), rows(1024), rows(1024)]
    tab_spec = pl.BlockSpec((3, tm, LANES), lambda b, i: (0, i, 0))
    return pl.pallas_call(
        functools.partial(_inproj_kernel, half_att=half_att, half_idx=half_idx),
        grid=(B, Lp // tm),
        in_specs=[pl.BlockSpec((1, tm, D), lambda b, i: (b, i, 0)),
                  pl.BlockSpec((1, D), lambda b, i: (0, 0)),
                  pl.BlockSpec((D, _W_COLS), lambda b, i: (0, 0)),
                  tab_spec, tab_spec, tab_spec, tab_spec],
        out_specs=[o[1] for o in outs],
        out_shape=[o[0] for o in outs],
        compiler_params=pltpu.CompilerParams(
            dimension_semantics=("arbitrary", "arbitrary"), vmem_limit_bytes=VMEM_LIMIT),
        name="inproj",
    )(h, g.reshape(1, D), w_arr, tatt, tkx, tiq, tik)


def _attn_kernel(q_ref, iq_ref, ikf_ref, ikb_ref, kx_ref, vt_ref, ltri_ref, o_ref,
                 keys_ref, acc_ref, m_ref, bias_ref, lg_ref, p_ref, *, C, k_sel):
    j = pl.program_id(1)
    nck = ((j + 1) * Q_BLOCK + C - 1) // C
    lane = _lane_iota((Q_BLOCK, LANES))
    qpos = j * Q_BLOCK + lax.broadcasted_iota(I32, (1, Q_BLOCK), 1)
    nsub = C // 8
    nacc = 8

    def pairs(heads):
        return [jnp.concatenate([heads[2 * p], heads[2 * p + 1]], axis=0) for p in range(len(heads) // 2)]

    iqf = iq_ref[0]
    per_blk = LANES // IDX_DIM
    iqh = []
    for h in range(IDX_HEADS):
        blk = iqf[:, (h // per_blk) * LANES:(h // per_blk + 1) * LANES]
        off = (h % per_blk) * IDX_DIM
        if off:
            blk = pltpu.roll(blk, LANES - off, 1)
        iqh.append(jnp.where(lane < IDX_DIM, blk, 0.0).astype(BF16))
    iqpair = pairs(iqh)
    wt = ikf_ref[0].T
    wrow = [wt[IDX_DIM + h:IDX_DIM + h + 1, :] for h in range(IDX_HEADS)]

    def causal_mask(c):
        kpos = c * C + lax.broadcasted_iota(I32, (C, 1), 0)
        return kpos <= qpos

    def phase_a(c, carry):
        ks = ikb_ref[0, pl.ds(pl.multiple_of(c * C, C), C), :]
        s = jnp.zeros((C, Q_BLOCK), F32)
        for p in range(IDX_HEADS // 2):
            d2 = _dot_nt(ks, iqpair[p])
            s = (s + jnp.maximum(d2[:, :LANES], 0.0) * wrow[2 * p]
                 + jnp.maximum(d2[:, LANES:], 0.0) * wrow[2 * p + 1])
        bits = pltpu.bitcast(s, I32)
        key = jnp.where(bits < 0, bits ^ jnp.int32(0x7FFFFFFF), bits)
        keys_ref[c] = jnp.where(causal_mask(c), key, jnp.int32(INT_MIN))
        return carry

    lax.fori_loop(0, nck, phase_a, 0)

    def count(cand, strict):
        def body(c, acc):
            kc = keys_ref[c]
            hit = (kc > cand) if strict else (kc >= cand)
            return acc + jnp.sum(jnp.where(hit, 1, 0).reshape(nacc, nsub // nacc, 8, LANES), axis=1)

        acc = lax.fori_loop(0, nck, body, jnp.zeros((nacc, 8, LANES), I32))
        return jnp.sum(jnp.sum(acc, axis=0), axis=0, keepdims=True)

    def search(i, carry):
        prefix, nge = carry
        bit = lax.shift_left(jnp.int32(1), jnp.int32(31) - i)
        cand = prefix ^ bit
        cnt = count(cand, False)
        ok = cnt >= k_sel
        return jnp.where(ok, cand, prefix), jnp.where(ok, cnt, nge)

    thr, nge = lax.fori_loop(
        0, SEARCH_CHECK_BIT, search,
        (jnp.full((1, Q_BLOCK), INT_MIN, I32), jnp.zeros((1, Q_BLOCK), I32)))
    settled = (nge == k_sel) | (qpos < k_sel)

    def resolve_open():
        done = settled | (count(thr, True) < k_sel)
        return lax.cond(jnp.max(jnp.where(done, 0, 1)) > 0,
                        lambda: lax.fori_loop(SEARCH_CHECK_BIT, 32, search, (thr, nge)),
                        lambda: (thr, nge))

    thr, nge = lax.cond(jnp.max(jnp.where(settled, 0, 1)) > 0, resolve_open, lambda: (thr, nge))
    excess = jnp.where((nge > k_sel) & (thr != INT_MIN), 1, 0)
    any_excess = jnp.max(excess) > 0
    need = lax.cond(any_excess,
                    lambda: (k_sel - count(thr, True)).astype(F32),
                    lambda: jnp.zeros((1, Q_BLOCK), F32))
    thr_lo = jnp.maximum(thr, jnp.int32(INT_MIN + 1))

    qf = q_ref[0]
    qh = []
    for h in range(ATT_HEADS):
        blk = qf[:, (h // 2) * LANES:(h // 2 + 1) * LANES]
        if h % 2:
            blk = pltpu.roll(blk, ATT_HEAD_DIM, 1)
        qh.append((jnp.where(lane < ATT_HEAD_DIM, blk, 0.0) * (LOG2E * ATT_HEAD_DIM ** -0.5)).astype(BF16))
    qpair = pairs(qh)
    kv_of_pair = [(2 * p) // (ATT_HEADS // ATT_KV_HEADS) for p in range(ATT_HEADS // 2)]

    acc_ref[...] = jnp.zeros(acc_ref.shape, F32)
    m_ref[...] = jnp.full(m_ref.shape, NEG_INF, F32)
    npair = ATT_HEADS // 2
    p_ref[npair:] = jnp.zeros((npair,) + p_ref.shape[1:], BF16)

    def to_mask(c, eq_seen):
        kc = keys_ref[c]

        def bias_simple():
            return jnp.where(kc >= thr_lo, 0.0, NEG_INF), eq_seen

        def bias_ties():
            eq = (kc == thr) & causal_mask(c)
            eqf = jnp.where(eq, 1.0, 0.0)
            seen = eq_seen
            ranks = []
            for t in range(C // LANES):
                part = eqf[t * LANES:(t + 1) * LANES]
                ranks.append(seen + _dot(ltri_ref[...], part.astype(BF16)))
                seen = seen + jnp.sum(part, axis=0, keepdims=True)
            rank = jnp.concatenate(ranks, axis=0)
            take = (kc > thr) | (eq & (rank < need))
            return jnp.where(take, 0.0, NEG_INF), seen

        bias, eq_next = lax.cond(any_excess, bias_ties, bias_simple)
        keys_ref[c] = pltpu.bitcast(bias, I32)
        return eq_next

    lax.fori_loop(0, nck, to_mask, jnp.zeros((1, Q_BLOCK), F32))

    def logits(c_any, slot):
        c = jnp.minimum(c_any, nck - 1)
        bias = pltpu.bitcast(keys_ref[c], F32) + jnp.where(c_any < nck, 0.0, NEG_INF)
        bias_ref[slot] = bias
        start = pl.multiple_of(c * C, C)
        for p in range(npair):
            lg2 = _dot_nt(kx_ref[0, kv_of_pair[p], pl.ds(start, C), :], qpair[p])
            lg_ref[slot * ATT_HEADS + 2 * p] = lg2[:, :LANES] + bias_ref[slot]
            lg_ref[slot * ATT_HEADS + 2 * p + 1] = lg2[:, LANES:] + bias_ref[slot]

    def weighted_values(c_any, slot, alpha):
        c = jnp.clip(c_any, 0, nck - 1)
        for p in range(npair):
            o2 = _dot(vt_ref[0, kv_of_pair[p], c], p_ref[slot * npair + p])
            for s in range(2):
                h = 2 * p + s
                acc_ref[h] = alpha[h:h + 1] * acc_ref[h] + o2[:, s * LANES:(s + 1) * LANES]

    def softmax(slot):
        shifts, alphas = [], []
        for h in range(ATT_HEADS):
            m_old = m_ref[h]
            mc = jnp.max(lg_ref[slot * ATT_HEADS + h].reshape(nsub, 8, LANES), axis=0)
            m_new = jnp.maximum(m_old, jnp.max(mc, axis=0, keepdims=True))
            m_ref[h] = m_new
            m_safe = jnp.where(m_new == NEG_INF, 0.0, m_new)
            shifts.append(m_safe[:1])
            alphas.append(jnp.exp2(m_old - m_safe)[:1])
        for h in range(ATT_HEADS):
            p_ref[slot * npair + h // 2, :, (h % 2) * LANES:(h % 2 + 1) * LANES] = (
                jnp.exp2(lg_ref[slot * ATT_HEADS + h] - shifts[h]).astype(BF16))
        return jnp.concatenate(alphas, axis=0)

    def step(c, cur, alpha_prev):
        nxt = 1 - cur
        weighted_values(c - 1, nxt, alpha_prev)
        logits(c + 1, nxt)
        return softmax(cur)

    def phase_c(i, alpha_prev):
        return step(2 * i + 1, 1, step(2 * i, 0, alpha_prev))

    logits(0, 0)
    alpha_last = lax.fori_loop(0, (nck + 1) // 2, phase_c, jnp.ones((ATT_HEADS, Q_BLOCK), F32))
    weighted_values(2 * ((nck + 1) // 2) - 1, 1, alpha_last)

    for p in range(ATT_HEADS // 2):
        a = acc_ref[2 * p]
        b = acc_ref[2 * p + 1]
        ot = jnp.concatenate([a[:ATT_HEAD_DIM] / a[ATT_HEAD_DIM:ATT_HEAD_DIM + 1],
                              b[:ATT_HEAD_DIM] / b[ATT_HEAD_DIM:ATT_HEAD_DIM + 1]], axis=0)
        o_ref[0, :, p * LANES:(p + 1) * LANES] = ot.T


def _attn(q, iq, ikf, ikb, kx, vx, k_sel):
    B, Lp, _ = q.shape
    C = _pick_tile(Lp, (640, 512, 256, 128))
    nch = Lp // C
    ltri = jnp.asarray(np.tril(np.ones((LANES, LANES), np.float32), -1), BF16)
    vt = vx.reshape(B, ATT_KV_HEADS, nch, C, LANES).transpose(0, 1, 2, 4, 3)
    qspec = lambda w: pl.BlockSpec((1, Q_BLOCK, w), lambda b, j: (b, j, 0))
    return pl.pallas_call(
        functools.partial(_attn_kernel, C=C, k_sel=k_sel),
        grid=(B, Lp // Q_BLOCK),
        in_specs=[qspec(512), qspec(256), qspec(128),
                  pl.BlockSpec((1, Lp, LANES), lambda b, j: (b, 0, 0)),
                  pl.BlockSpec((1, ATT_KV_HEADS, Lp, LANES), lambda b, j: (b, 0, 0, 0)),
                  pl.BlockSpec((1, ATT_KV_HEADS, nch, LANES, C), lambda b, j: (b, 0, 0, 0, 0)),
                  pl.BlockSpec((LANES, LANES), lambda b, j: (0, 0))],
        out_specs=qspec(512),
        out_shape=jax.ShapeDtypeStruct((B, Lp, ATT_WIDTH), F32),
        scratch_shapes=[pltpu.VMEM((nch, C, Q_BLOCK), I32),
                        pltpu.VMEM((ATT_HEADS, LANES, Q_BLOCK), F32),
                        pltpu.VMEM((ATT_HEADS, 8, Q_BLOCK), F32),
                        pltpu.VMEM((2, C, Q_BLOCK), F32),
                        pltpu.VMEM((2 * ATT_HEADS, C, Q_BLOCK), F32),
                        pltpu.VMEM((2 * (ATT_HEADS // 2), C, 2 * Q_BLOCK), BF16)],
        compiler_params=pltpu.CompilerParams(
            dimension_semantics=("arbitrary", "arbitrary"), vmem_limit_bytes=VMEM_LIMIT),
        name="attn",
    )(q, iq, ikf, ikb, kx, vt, ltri)


def _hgrn_sum_matrix(C):
    nlev = int(np.log2(C))
    t = np.arange(C)[:, None]
    jj = np.arange(C)[None, :]
    mats = [(jj <= t), (jj > t)]
    for l in range(nlev):
        s = 1 << l
        mid = ((t >> (l + 1)) << (l + 1)) + s - 1
        odd = ((t >> l) & 1) == 1
        mats.append(np.where(odd, (jj > mid) & (jj <= t), (jj > t) & (jj <= mid)))
    return np.concatenate(mats, axis=0).astype(np.float32), nlev


def _hgrn_kernel(hq_ref, hf_ref, hi_ref, hg_ref, lb_ref, gn_ref, e_ref, o_ref, st_ref, *, C, nlev):
    @pl.when(pl.program_id(1) == 0)
    def _():
        st_ref[...] = jnp.zeros(st_ref.shape, F32)

    for bb in range(hq_ref.shape[0]):
        for h in range(HG_HEADS):
            _hgrn_head(hq_ref, hf_ref, hi_ref, hg_ref, lb_ref, gn_ref, e_ref, o_ref, st_ref, bb, h, C, nlev)


def _hgrn_head(hq_ref, hf_ref, hi_ref, hg_ref, lb_ref, gn_ref, e_ref, o_ref, st_ref, bb, h, C, nlev):
    cols = slice(h * HG_DK, (h + 1) * HG_DK)
    lb = lb_ref[:, cols]
    f = lb + (1.0 - lb) * jax.nn.sigmoid(hf_ref[bb, :, cols])
    logf = jnp.log(f)
    kk = 1.0 - f
    q = hq_ref[bb, :, cols] * (HG_DK ** -0.5)
    v = hi_ref[bb, :, cols].astype(BF16)

    p1 = logf.astype(BF16)
    r1 = logf - p1.astype(F32)
    p2 = r1.astype(BF16)
    p3 = (r1 - p2.astype(F32)).astype(BF16)
    x3 = _dot(e_ref[...], jnp.concatenate([p1, p2, p3], axis=1))
    xs = x3[:, :LANES] + x3[:, LANES:2 * LANES] + x3[:, 2 * LANES:]

    b = xs[0:C]
    suffix = xs[C:2 * C]
    rt = lax.broadcasted_iota(I32, (C, C), 0)
    ct = lax.broadcasted_iota(I32, (C, C), 1)
    row = lax.broadcasted_iota(I32, (C, 1), 0)

    scores = jnp.where(rt == ct, _dot_nt(q.astype(BF16), kk.astype(BF16)), 0.0)
    for l in range(nlev):
        w = jnp.exp(xs[(2 + l) * C:(3 + l) * C])
        odd = ((row >> l) & 1) == 1
        a = (jnp.where(odd, q, kk) * w).astype(BF16)
        pair = (((rt >> l) & 1) == 1) & ((ct >> l) == (rt >> l) - 1)
        scores = scores + jnp.where(pair, _dot_nt(a, a), 0.0)

    st = st_ref[bb * HG_HEADS + h]
    qb = (q * jnp.exp(b)).astype(BF16)
    o = _dot(scores.astype(BF16), v) + _dot_nt(qb, st.astype(BF16))
    ks = (kk * jnp.exp(suffix)).astype(BF16)
    st_ref[bb * HG_HEADS + h] = jnp.exp(b[C - 1:C, :]) * st + _dot_tn(v, ks)

    o = o * lax.rsqrt(jnp.mean(o * o, axis=-1, keepdims=True) + EPS) * gn_ref[:, cols]
    hg = hg_ref[bb, :, cols]
    o_ref[bb, :, cols] = o * (hg * jax.nn.sigmoid(hg))


def _hgrn(hq, hf, hi, hg, lb, gn):
    B, Lp, _ = hq.shape
    C = HG_CHUNK
    e_np, nlev = _hgrn_sum_matrix(C)
    e = jnp.asarray(e_np, BF16)
    nb = 2 if B % 2 == 0 else 1
    blk = pl.BlockSpec((nb, C, HG_WIDTH), lambda b, c: (b, c, 0))
    vec = pl.BlockSpec((1, HG_WIDTH), lambda b, c: (0, 0))
    return pl.pallas_call(
        functools.partial(_hgrn_kernel, C=C, nlev=nlev),
        grid=(B // nb, Lp // C),
        in_specs=[blk, blk, blk, blk, vec, vec,
                  pl.BlockSpec(e.shape, lambda b, c: (0, 0))],
        out_specs=blk,
        out_shape=jax.ShapeDtypeStruct((B, Lp, HG_WIDTH), F32),
        scratch_shapes=[pltpu.VMEM((nb * HG_HEADS, HG_DV, HG_DK), F32)],
        compiler_params=pltpu.CompilerParams(
            dimension_semantics=("arbitrary", "arbitrary"), vmem_limit_bytes=VMEM_LIMIT),
        name="hgrn",
    )(hq, hf, hi, hg, lb.reshape(1, -1), gn.reshape(1, -1), e)


def _merge_kernel(att_ref, hgr_ref, ga_ref, gh_ref, h_ref, wa_ref, wh_ref, wo_ref, g_ref,
                  wr_ref, br_ref, lt_ref,
                  h2_ref, xrow_ref, topi_ref, gate_ref, rank_ref, cnt_ref, *, tm):
    mix = (jax.nn.sigmoid(ga_ref[0]) * _dot(att_ref[0].astype(BF16), wa_ref[...])
           + jax.nn.sigmoid(gh_ref[0]) * _dot(hgr_ref[0].astype(BF16), wh_ref[...]))
    h2 = h_ref[0] + _dot(mix.astype(BF16), wo_ref[...])
    h2_ref[0] = h2
    xn = h2 * lax.rsqrt(jnp.mean(h2 * h2, axis=-1, keepdims=True) + EPS) * g_ref[...]
    for k in range(D_MODEL // LANES):
        xrow_ref[pl.ds(k, tm, stride=8), :] = xn[:, k * LANES:(k + 1) * LANES]

    xh = xn.astype(BF16)
    xl = (xn - xh.astype(F32)).astype(BF16)
    l1 = _dot(xh, wr_ref[...])
    logits = l1[:, :LANES] + l1[:, LANES:] + _dot(xl, wr_ref[:, :LANES]) + br_ref[...]

    lane = _lane_iota((tm, LANES))
    lanef = lane.astype(F32)
    work = logits
    vals, sels, idxs = [], [], []
    for _ in range(TOP_K):
        m = jnp.max(work, axis=1, keepdims=True)
        idx = jnp.min(jnp.where(work == m, lanef, float(LANES)), axis=1, keepdims=True)
        sel = lanef == idx
        vals.append(m)
        idxs.append(idx)
        sels.append(sel)
        work = jnp.where(sel, NEG_INF, work)
    es = [jnp.exp(v - vals[0]) for v in vals]
    den = es[0] + es[1] + es[2] + es[3]

    member = jnp.zeros((tm, LANES), F32)
    for sel in sels:
        member = member + jnp.where(sel, 1.0, 0.0)
    before = _dot(lt_ref[...], member.astype(BF16))
    topi = jnp.zeros((tm, LANES), F32)
    gate = jnp.zeros((tm, LANES), F32)
    rank = jnp.zeros((tm, LANES), F32)
    for jx in range(TOP_K):
        here = lane == jx
        topi = jnp.where(here, idxs[jx], topi)
        gate = jnp.where(here, es[jx] / den, gate)
        rank = jnp.where(here, jnp.sum(jnp.where(sels[jx], before, 0.0), axis=1, keepdims=True), rank)
    topi_ref[0] = topi.astype(I32)
    gate_ref[0] = gate
    rank_ref[0] = rank.astype(I32)
    cnt_ref[0] = jnp.sum(member, axis=0, keepdims=True).astype(I32)


def _merge(att, hgr, ga, gh, h, wa, wh, wo, g, wr2, br):
    B, Lp, D = h.shape
    tm = _pick_tile(Lp, (320, 256, 128))
    nt = Lp // tm
    lt = jnp.asarray(np.tril(np.ones((tm, tm), np.float32), -1), BF16)
    row = lambda w: pl.BlockSpec((1, tm, w), lambda b, i: (b, i, 0))
    full = lambda a: pl.BlockSpec(a.shape, lambda b, i: (0,) * a.ndim)
    g2 = g.reshape(1, D)
    out_shape = [jax.ShapeDtypeStruct((B, Lp, D), F32),
                 jax.ShapeDtypeStruct((B * Lp * 8, LANES), F32),
                 jax.ShapeDtypeStruct((B, Lp, LANES), I32),
                 jax.ShapeDtypeStruct((B, Lp, LANES), F32),
                 jax.ShapeDtypeStruct((B, Lp, LANES), I32),
                 jax.ShapeDtypeStruct((B * nt, 1, LANES), I32)]
    out_specs = [row(D),
                 pl.BlockSpec((tm * 8, LANES), lambda b, i: (b * nt + i, 0)),
                 row(LANES), row(LANES), row(LANES),
                 pl.BlockSpec((1, 1, LANES), lambda b, i: (b * nt + i, 0, 0))]
    return pl.pallas_call(
        functools.partial(_merge_kernel, tm=tm),
        grid=(B, nt),
        in_specs=[row(512), row(512), row(D), row(D), row(D),
                  full(wa), full(wh), full(wo), full(g2), full(wr2), full(br), full(lt)],
        out_specs=out_specs,
        out_shape=out_shape,
        compiler_params=pltpu.CompilerParams(
            dimension_semantics=("arbitrary", "arbitrary"), vmem_limit_bytes=VMEM_LIMIT),
        name="merge",
    )(att, hgr, ga, gh, h, wa, wh, wo, g2, wr2, br, lt), tm


def _slots_kernel(topi_ref, rank_ref, base_ref, slot_ref):
    topi = topi_ref[0]
    rank = rank_ref[0]
    base = base_ref[0]
    lane = _lane_iota(topi.shape)
    slot = jnp.zeros(topi.shape, I32)
    for jx in range(TOP_K):
        mine = lane == topi[:, jx:jx + 1]
        first = jnp.sum(jnp.where(mine, base, 0), axis=1, keepdims=True)
        slot = jnp.where(lane == jx, first + rank[:, jx:jx + 1], slot)
    slot_ref[0] = slot


def _slots(topi, rank, base, tm):
    B, Lp, _ = topi.shape
    nt = Lp // tm
    row = pl.BlockSpec((1, tm, LANES), lambda b, i: (b, i, 0))
    return pl.pallas_call(
        _slots_kernel,
        grid=(B, nt),
        in_specs=[row, row, pl.BlockSpec((1, 1, LANES), lambda b, i: (b * nt + i, 0, 0))],
        out_specs=row,
        out_shape=jax.ShapeDtypeStruct((B, Lp, LANES), I32),
        compiler_params=pltpu.CompilerParams(dimension_semantics=("arbitrary", "arbitrary")),
        name="slots",
    )(topi, rank, base)


def _dispatch_kernel(pad_lo_ref, pad_n_ref, nu_ref, slot_ref, x_ref, xs_hbm, zblk_ref, sem, *, tm, blk, nblk):
    @pl.when(pl.program_id(0) == 0)
    def _():
        zblk_ref[...] = jnp.zeros(zblk_ref.shape, F32)

        def zblock(b):
            return pltpu.make_async_copy(zblk_ref, xs_hbm.at[pl.ds(b * blk, blk)], sem)

        def bissue(b, c):
            zblock(b).start()
            return c

        def bdrain(b, c):
            zblock(b).wait()
            return c

        lax.fori_loop(nu_ref[0], nblk, bissue, 0)
        lax.fori_loop(nu_ref[0], nblk, bdrain, 0)

        def per_expert(e, c):
            def zcopy(r):
                return pltpu.make_async_copy(zblk_ref.at[0], xs_hbm.at[pad_lo_ref[e] + r], sem)

            def zissue(r, c2):
                zcopy(r).start()
                return c2

            def zdrain(r, c2):
                zcopy(r).wait()
                return c2

            lax.fori_loop(0, pad_n_ref[e], zissue, 0)
            lax.fori_loop(0, pad_n_ref[e], zdrain, 0)
            return c

        lax.fori_loop(0, N_EXPERTS, per_expert, 0)

    def copy(t, jx):
        return pltpu.make_async_copy(x_ref.at[t], xs_hbm.at[slot_ref[0, 0, t * TOP_K + jx]], sem)

    def issue(t, c):
        for jx in range(TOP_K):
            copy(t, jx).start()
        return c

    def drain(t, c):
        for jx in range(TOP_K):
            copy(t, jx).wait()
        return c

    lax.fori_loop(0, tm, issue, 0)
    lax.fori_loop(0, tm, drain, 0)


def _dispatch(pad_lo, pad_n, n_used, slot_s, xrow, cap, tm, blk):
    nt = slot_s.shape[0]
    return pl.pallas_call(
        functools.partial(_dispatch_kernel, tm=tm, blk=blk, nblk=cap // blk),
        grid_spec=pltpu.PrefetchScalarGridSpec(
            num_scalar_prefetch=3, grid=(nt,),
            in_specs=[pl.BlockSpec((1, 1, tm * TOP_K), lambda i, lo, n, nu: (i, 0, 0), memory_space=pltpu.SMEM),
                      pl.BlockSpec((tm, 8, LANES), lambda i, lo, n, nu: (i, 0, 0))],
            out_specs=pl.BlockSpec(memory_space=pl.ANY),
            scratch_shapes=[pltpu.VMEM((blk, 8, LANES), F32), pltpu.SemaphoreType.DMA(())]),
        out_shape=jax.ShapeDtypeStruct((cap, 8, LANES), F32),
        compiler_params=pltpu.CompilerParams(dimension_semantics=("arbitrary",)),
        name="dispatch",
    )(pad_lo, pad_n, n_used, slot_s, xrow)


def _ffn_kernel(be_ref, nu_ref, x_ref, wg_ref, bg_ref, wu_ref, bu_ref, wd_ref, bd_ref, o_ref,
                wgb_ref, wub_ref, wdb_ref, *, blk):
    i = pl.program_id(0)
    nk = D_MODEL // LANES
    used = i < nu_ref[0]
    new_expert = (i == 0) | (be_ref[i] != be_ref[jnp.maximum(i - 1, 0)])

    @pl.when(used & new_expert)
    def _():
        rows = 128

        def cast(r, c):
            sl = pl.ds(pl.multiple_of(r * rows, rows), rows)
            wgb_ref[sl, :] = wg_ref[0, sl, :].astype(BF16)
            wub_ref[sl, :] = wu_ref[0, sl, :].astype(BF16)
            wdb_ref[sl, :] = wd_ref[0, sl, :].astype(BF16)
            return c

        lax.fori_loop(0, D_MODEL // rows, cast, 0)

    @pl.when(used)
    def _():
        xb = jnp.concatenate([x_ref[pl.ds(k, blk, stride=8), :] for k in range(nk)], axis=1).astype(BF16)
        g = _dot(xb, wgb_ref[...]) + bg_ref[0]
        u = _dot(xb, wub_ref[...]) + bu_ref[0]
        g = jnp.minimum(g, SWIGLU_LIMIT)
        u = jnp.clip(u, -SWIGLU_LIMIT, SWIGLU_LIMIT)
        act = (u + 1.0) * (g * jax.nn.sigmoid(SWIGLU_ALPHA * g))
        out = _dot(act.astype(BF16), wdb_ref[...]) + bd_ref[0]
        for k in range(nk):
            o_ref[pl.ds(k, blk, stride=8), :] = out[:, k * LANES:(k + 1) * LANES]

    @pl.when(i >= nu_ref[0])
    def _():
        o_ref[...] = jnp.zeros(o_ref.shape, F32)


def _ffn(block_expert, n_used, xs, wg, bg, wu, bu, wd, bd, blk):
    cap = xs.shape[0]
    nblk = cap // blk
    x2 = xs.reshape(cap * 8, LANES)
    wspec = pl.BlockSpec((1, D_MODEL, D_FF), lambda i, be, nu: (be[i], 0, 0))
    wdspec = pl.BlockSpec((1, D_FF, D_MODEL), lambda i, be, nu: (be[i], 0, 0))
    bspec = lambda w: pl.BlockSpec((1, 1, w), lambda i, be, nu: (be[i], 0, 0))
    rows = pl.BlockSpec((blk * 8, LANES), lambda i, be, nu: (i, 0))
    rows_in = pl.BlockSpec((blk * 8, LANES), lambda i, be, nu: (jnp.minimum(i, nu[0] - 1), 0))
    assert D_FF == D_MODEL
    wscratch = pltpu.VMEM((D_MODEL, D_FF), BF16)
    out = pl.pallas_call(
        functools.partial(_ffn_kernel, blk=blk),
        grid_spec=pltpu.PrefetchScalarGridSpec(
            num_scalar_prefetch=2, grid=(nblk,),
            in_specs=[rows_in, wspec, bspec(D_FF), wspec, bspec(D_FF), wdspec, bspec(D_MODEL)],
            out_specs=rows,
            scratch_shapes=[wscratch, wscratch, wscratch]),
        out_shape=jax.ShapeDtypeStruct((cap * 8, LANES), F32),
        compiler_params=pltpu.CompilerParams(
            dimension_semantics=("arbitrary",), vmem_limit_bytes=VMEM_LIMIT),
        name="ffn",
    )(block_expert, n_used, x2, wg, bg, wu, bu, wd, bd)
    return out.reshape(cap, 8, LANES)


def _combine_kernel(slot_ref, gate_ref, h2_ref, g_ref, ys_hbm, o_ref, buf_ref, sem, *, tm):
    def copy(t, jx):
        dst = buf_ref.at[pl.ds(pl.multiple_of((jx * tm + t) * 8, 8), 8), :]
        return pltpu.make_async_copy(ys_hbm.at[slot_ref[0, 0, t * TOP_K + jx]], dst, sem)

    def issue(t, c):
        for jx in range(TOP_K):
            copy(t, jx).start()
        return c

    def drain(t, c):
        for jx in range(TOP_K):
            copy(t, jx).wait()
        return c

    lax.fori_loop(0, tm, issue, 0)
    lax.fori_loop(0, tm, drain, 0)

    gate = gate_ref[0]
    h2 = h2_ref[0]
    cols = []
    for k in range(D_MODEL // LANES):
        y = h2[:, k * LANES:(k + 1) * LANES]
        for jx in range(TOP_K):
            y = y + gate[:, jx:jx + 1] * buf_ref[pl.ds(jx * tm * 8 + k, tm, stride=8), :]
        cols.append(y)
    ho = jnp.concatenate(cols, axis=1)
    o_ref[0] = ho * lax.rsqrt(jnp.mean(ho * ho, axis=-1, keepdims=True) + EPS) * g_ref[...]


def _combine(slot_s, gate, h2, g, ys, tm):
    B, Lp, D = h2.shape
    nt = Lp // tm
    row = lambda w: pl.BlockSpec((1, tm, w), lambda b, i: (b, i, 0))
    return pl.pallas_call(
        functools.partial(_combine_kernel, tm=tm),
        grid=(B, nt),
        in_specs=[pl.BlockSpec((1, 1, tm * TOP_K), lambda b, i: (b * nt + i, 0, 0), memory_space=pltpu.SMEM),
                  row(LANES), row(D),
                  pl.BlockSpec((1, D), lambda b, i: (0, 0)),
                  pl.BlockSpec(memory_space=pl.ANY)],
        out_specs=row(D),
        out_shape=jax.ShapeDtypeStruct((B, Lp, D), F32),
        scratch_shapes=[pltpu.VMEM((TOP_K * tm * 8, LANES), F32), pltpu.SemaphoreType.DMA(())],
        compiler_params=pltpu.CompilerParams(
            dimension_semantics=("arbitrary", "arbitrary"), vmem_limit_bytes=VMEM_LIMIT),
        name="combine",
    )(slot_s, gate, h2, g.reshape(1, D), ys)


def _moe(h2, xrow, topi, gate, rank, cnt, tm, ffn_w, final_g):
    B, Lp, D = h2.shape
    T = B * Lp
    nt_total = cnt.shape[0]
    blk = FFN_BLOCK
    nblk = -(-(T * TOP_K) // blk) + N_EXPERTS
    cap = nblk * blk
    cnt2 = cnt.reshape(nt_total, LANES)
    totals = jnp.sum(cnt2, axis=0)
    padded = (totals + blk - 1) // blk * blk
    pad_ends = jnp.cumsum(padded)
    pad_starts = pad_ends - padded
    base = (pad_starts[None, :] + jnp.cumsum(cnt2, axis=0) - cnt2).astype(I32).reshape(nt_total, 1, LANES)
    block_start = (jnp.arange(nblk) * blk)[:, None]
    block_expert = jnp.minimum(
        jnp.sum(pad_ends[None, :N_EXPERTS] <= block_start, axis=1), N_EXPERTS - 1).astype(I32)
    n_used = (pad_ends[N_EXPERTS - 1] // blk).astype(I32).reshape(1)
    slot = _slots(topi, rank, base, tm)
    slot_s = slot[:, :, :TOP_K].reshape(nt_total, 1, tm * TOP_K)

    pad_lo = (pad_starts + totals)[:N_EXPERTS].astype(I32)
    pad_n = (padded - totals)[:N_EXPERTS].astype(I32)
    xs = _dispatch(pad_lo, pad_n, n_used, slot_s, xrow.reshape(T, 8, LANES), cap, tm, blk)
    ys = _ffn(block_expert, n_used, xs, *ffn_w, blk)
    return _combine(slot_s, gate, h2, final_g, ys, tm)


def kernel(x, meta_tokens, attn_norm_g, w_in, hgrn_norm_g, w_up_attn, w_up_hgrn, w_out,
           hgrn_lb_logits, ffn_norm_g, router_w, router_b, w_gate, b_gate, w_up, b_up,
           w_down, b_down, final_norm_g):
    B, S, D = x.shape
    L = S + N_META
    Lp = -(-L // Q_BLOCK) * Q_BLOCK
    k_sel = min(TOPK_MAX, L // 4)
    meta = jnp.broadcast_to(meta_tokens[None].astype(x.dtype), (B, N_META, D))
    h = jnp.concatenate([meta, x, jnp.zeros((B, Lp - L, D), x.dtype)], axis=1)
    lb = jnp.cumsum(jax.nn.softmax(hgrn_lb_logits.astype(F32), axis=0), axis=0)[0]

    (q, kx, vx, iq, ikf, ikb, hq, hf, hi, hg, ga, gh) = _inproj(h, attn_norm_g[0], _arrange_w_in(w_in[0]))
    att = _attn(q, iq, ikf, ikb, kx, vx, k_sel)
    hgr = _hgrn(hq, hf, hi, hg, lb, hgrn_norm_g[0])

    wr = jnp.pad(router_w[0], ((0, 0), (0, LANES - N_EXPERTS)))
    wr_hi = wr.astype(BF16)
    wr_lo = (wr - wr_hi.astype(F32)).astype(BF16)
    br = jnp.pad(router_b[0], (0, LANES - N_EXPERTS), constant_values=-1e30).reshape(1, LANES)
    (h2, xrow, topi, gate, rank, cnt), tm = _merge(
        att, hgr, ga, gh, h, w_up_attn[0].astype(BF16), w_up_hgrn[0].astype(BF16), w_out[0].astype(BF16),
        ffn_norm_g[0], jnp.concatenate([wr_hi, wr_lo], axis=1), br)

    ffn_w = (w_gate[0], b_gate[0].reshape(N_EXPERTS, 1, D_FF),
             w_up[0], b_up[0].reshape(N_EXPERTS, 1, D_FF),
             w_down[0], b_down[0].reshape(N_EXPERTS, 1, D_MODEL))
    out = _moe(h2, xrow, topi, gate, rank, cnt, tm, ffn_w, final_norm_g)
    return out[:, N_META:L]
```

```python
import functools

import numpy as np
import jax
import jax.numpy as jnp
from jax import lax
from jax.experimental import pallas as pl
from jax.experimental.pallas import tpu as pltpu

F32 = jnp.float32
BF16 = jnp.bfloat16
I32 = jnp.int32

D_MODEL = 1024
N_META = 16
EPS = 1e-5
ROPE_THETA = 500000.0
ATT_HEADS = 8
ATT_KV_HEADS = 2
ATT_HEAD_DIM = 64
ATT_WIDTH = ATT_HEADS * ATT_HEAD_DIM
IDX_HEADS = 8
IDX_DIM = 32
TOPK_MAX = 256
Q_BLOCK = 128
HG_HEADS = 4
HG_DK = 128
HG_DV = 128
HG_WIDTH = HG_HEADS * HG_DV
N_EXPERTS = 32
TOP_K = 4
D_FF = 1024
SWIGLU_LIMIT = 7.0
SWIGLU_ALPHA = 1.702

LANES = 128
VMEM_LIMIT = 56 * 1024 * 1024
INT_MIN = -2 ** 31
NEG_INF = float("-inf")
LOG2E = 1.4426950408889634

SEARCH_CHECK_BIT = 25
HG_CHUNK = 128
FFN_BLOCK = 256


def _pick_tile(n, candidates):
    for c in candidates:
        if n % c == 0:
            return c
    raise ValueError(f"no tile for {n}")


def _dot(a, b):
    return jnp.dot(a, b, preferred_element_type=F32)


def _dot_nt(a, b):
    return lax.dot_general(a, b, (((1,), (1,)), ((), ())), preferred_element_type=F32)


def _dot_tn(a, b):
    return lax.dot_general(a, b, (((0,), (0,)), ((), ())), preferred_element_type=F32)


def _lane_iota(shape):
    return lax.broadcasted_iota(I32, shape, len(shape) - 1)


def _rope_tables(n_pos, head_dim, lane_valid):
    rot = head_dim // 4
    half = rot // 2
    inv = ROPE_THETA ** (-jnp.arange(half, dtype=F32) * 2.0 / rot)
    ang = jnp.arange(n_pos).astype(F32)[:, None] * inv[None, :]
    c = jnp.cos(ang)
    s = jnp.sin(ang)
    lane = np.arange(LANES)
    within = lane % head_dim
    valid = lane < lane_valid
    first = (within < half) & valid
    second = (within >= half) & (within < rot) & valid
    idx = np.where(within < half, within, within - half) % half
    cc = c[:, idx]
    ss = s[:, idx]
    tc = jnp.where((first | second)[None, :], cc, 1.0)
    t1 = jnp.where(first[None, :], -ss, 0.0)
    t2 = jnp.where(second[None, :], ss, 0.0)
    return jnp.stack([tc, t1, t2]).astype(F32), half


def _rope_apply(x, tab_ref, half):
    c = tab_ref[0]
    s1 = tab_ref[1]
    s2 = tab_ref[2]
    cols = []
    for j in range(x.shape[1] // LANES):
        xc = x[:, j * LANES:(j + 1) * LANES]
        cols.append(xc * c + pltpu.roll(xc, LANES - half, 1) * s1 + pltpu.roll(xc, half, 1) * s2)
    return cols[0] if len(cols) == 1 else jnp.concatenate(cols, axis=1)


_W_Q = (0, 512)
_W_KX = (512, 768)
_W_VX = (768, 1024)
_W_IQ = (1024, 1280)
_W_IK = (1280, 1408)
_W_HQ = (1408, 1920)
_W_HF = (1920, 2432)
_W_HI = (2432, 2944)
_W_HG = (2944, 3456)
_W_GA = (3456, 4480)
_W_GH = (4480, 5504)
_W_COLS = 5504


def _arrange_w_in(w_in):
    o = np.cumsum([0, 512, 128, 128, 256, 32, 8, 512, 512, 512, 512, 1024, 1024])
    seg = [w_in[:, o[i]:o[i + 1]] for i in range(12)]
    q, k, v, iq, ik, iw, hq, hf, hi, hg, ga, gh = seg
    z64 = jnp.zeros((D_MODEL, 64), w_in.dtype)
    z88 = jnp.zeros((D_MODEL, 88), w_in.dtype)
    kx = jnp.concatenate([k[:, :64], z64, k[:, 64:], z64], axis=1)
    vx = jnp.concatenate([v[:, :64], z64, v[:, 64:], z64], axis=1)
    ikx = jnp.concatenate([ik, iw, z88], axis=1)
    w = jnp.concatenate([q, kx, vx, iq, ikx, hq, hf, hi, hg, ga, gh], axis=1)
    assert w.shape[1] == _W_COLS
    return w.astype(BF16)


def _inproj_kernel(h_ref, g_ref, w_ref, tatt_ref, tkx_ref, tiq_ref, tik_ref,
                   q_ref, kx_ref, vx_ref, iq_ref, ikf_ref, ikb_ref,
                   hq_ref, hf_ref, hi_ref, hg_ref, ga_ref, gh_ref, *, half_att, half_idx):
    x = h_ref[0]
    xn = x * lax.rsqrt(jnp.mean(x * x, axis=-1, keepdims=True) + EPS) * g_ref[...]
    xb = xn.astype(BF16)

    def proj(rng):
        return _dot(xb, w_ref[:, rng[0]:rng[1]])

    q_ref[0] = _rope_apply(proj(_W_Q), tatt_ref, half_att)
    kx = _rope_apply(proj(_W_KX), tkx_ref, half_att).astype(BF16)
    kx_ref[0, 0] = kx[:, :LANES]
    kx_ref[0, 1] = kx[:, LANES:]
    vx = proj(_W_VX)
    ones_hi = jnp.where(_lane_iota(vx.shape) % LANES >= ATT_HEAD_DIM, 1.0, 0.0)
    vx = (vx + ones_hi).astype(BF16)
    vx_ref[0, 0] = vx[:, :LANES]
    vx_ref[0, 1] = vx[:, LANES:]
    iq_ref[0] = _rope_apply(proj(_W_IQ), tiq_ref, half_idx)
    ik = _rope_apply(proj(_W_IK), tik_ref, half_idx)
    ikf_ref[0] = ik
    ikb_ref[0] = ik.astype(BF16)
    hq_ref[0] = proj(_W_HQ)
    hf_ref[0] = proj(_W_HF)
    hi_ref[0] = proj(_W_HI)
    hg_ref[0] = proj(_W_HG)
    ga_ref[0] = proj(_W_GA)
    gh_ref[0] = proj(_W_GH)


def _inproj(h, g, w_arr):
    B, Lp, D = h.shape
    tm = _pick_tile(Lp, (320, 256, 128))
    tatt, half_att = _rope_tables(Lp, ATT_HEAD_DIM, LANES)
    tkx, _ = _rope_tables(Lp, ATT_HEAD_DIM, ATT_HEAD_DIM)
    tiq, half_idx = _rope_tables(Lp, IDX_DIM, LANES)
    tik, _ = _rope_tables(Lp, IDX_DIM, IDX_DIM)

    def rows(width, dtype=F32):
        return (jax.ShapeDtypeStruct((B, Lp, width), dtype),
                pl.BlockSpec((1, tm, width), lambda b, i: (b, i, 0)))

    def kv(dtype=BF16):
        return (jax.ShapeDtypeStruct((B, ATT_KV_HEADS, Lp, LANES), dtype),
                pl.BlockSpec((1, ATT_KV_HEADS, tm, LANES), lambda b, i: (b, 0, i, 0)))

    outs = [rows(512), kv(), kv(), rows(256), rows(128), rows(128, BF16),
            rows(512), rows(512), rows(512), rows(512), rows(1024), rows(1024)]
    tab_spec = pl.BlockSpec((3, tm, LANES), lambda b, i: (0, i, 0))
    return pl.pallas_call(
        functools.partial(_inproj_kernel, half_att=half_att, half_idx=half_idx),
        grid=(B, Lp // tm),
        in_specs=[pl.BlockSpec((1, tm, D), lambda b, i: (b, i, 0)),
                  pl.BlockSpec((1, D), lambda b, i: (0, 0)),
                  pl.BlockSpec((D, _W_COLS), lambda b, i: (0, 0)),
                  tab_spec, tab_spec, tab_spec, tab_spec],
        out_specs=[o[1] for o in outs],
        out_shape=[o[0] for o in outs],
        compiler_params=pltpu.CompilerParams(
            dimension_semantics=("arbitrary", "arbitrary"), vmem_limit_bytes=VMEM_LIMIT),
        name="inproj",
    )(h, g.reshape(1, D), w_arr, tatt, tkx, tiq, tik)


def _attn_kernel(q_ref, iq_ref, ikf_ref, ikb_ref, kx_ref, vt_ref, ltri_ref, o_ref,
                 keys_ref, acc_ref, m_ref, bias_ref, lg_ref, p_ref, *, C, k_sel):
    j = pl.program_id(1)
    nck = ((j + 1) * Q_BLOCK + C - 1) // C
    lane = _lane_iota((Q_BLOCK, LANES))
    qpos = j * Q_BLOCK + lax.broadcasted_iota(I32, (1, Q_BLOCK), 1)
    nsub = C // 8
    nacc = 8

    def pairs(heads):
        return [jnp.concatenate([heads[2 * p], heads[2 * p + 1]], axis=0) for p in range(len(heads) // 2)]

    iqf = iq_ref[0]
    per_blk = LANES // IDX_DIM
    iqh = []
    for h in range(IDX_HEADS):
        blk = iqf[:, (h // per_blk) * LANES:(h // per_blk + 1) * LANES]
        off = (h % per_blk) * IDX_DIM
        if off:
            blk = pltpu.roll(blk, LANES - off, 1)
        iqh.append(jnp.where(lane < IDX_DIM, blk, 0.0).astype(BF16))
    iqpair = pairs(iqh)
    wt = ikf_ref[0].T
    wrow = [wt[IDX_DIM + h:IDX_DIM + h + 1, :] for h in range(IDX_HEADS)]

    def causal_mask(c):
        kpos = c * C + lax.broadcasted_iota(I32, (C, 1), 0)
        return kpos <= qpos

    def phase_a(c, carry):
        ks = ikb_ref[0, pl.ds(pl.multiple_of(c * C, C), C), :]
        s = jnp.zeros((C, Q_BLOCK), F32)
        for p in range(IDX_HEADS // 2):
            d2 = _dot_nt(ks, iqpair[p])
            s = (s + jnp.maximum(d2[:, :LANES], 0.0) * wrow[2 * p]
                 + jnp.maximum(d2[:, LANES:], 0.0) * wrow[2 * p + 1])
        bits = pltpu.bitcast(s, I32)
        key = jnp.where(bits < 0, bits ^ jnp.int32(0x7FFFFFFF), bits)
        keys_ref[c] = jnp.where(causal_mask(c), key, jnp.int32(INT_MIN))
        return carry

    lax.fori_loop(0, nck, phase_a, 0)

    def count(cand, strict):
        def body(c, acc):
            kc = keys_ref[c]
            hit = (kc > cand) if strict else (kc >= cand)
            return acc + jnp.sum(jnp.where(hit, 1, 0).reshape(nacc, nsub // nacc, 8, LANES), axis=1)

        acc = lax.fori_loop(0, nck, body, jnp.zeros((nacc, 8, LANES), I32))
        return jnp.sum(jnp.sum(acc, axis=0), axis=0, keepdims=True)

    def search(i, carry):
        prefix, nge = carry
        bit = lax.shift_left(jnp.int32(1), jnp.int32(31) - i)
        cand = prefix ^ bit
        cnt = count(cand, False)
        ok = cnt >= k_sel
        return jnp.where(ok, cand, prefix), jnp.where(ok, cnt, nge)

    thr, nge = lax.fori_loop(
        0, SEARCH_CHECK_BIT, search,
        (jnp.full((1, Q_BLOCK), INT_MIN, I32), jnp.zeros((1, Q_BLOCK), I32)))
    unsettled = jnp.max(jnp.where((nge == k_sel) | (qpos < k_sel), 0, 1)) > 0
    thr, nge = lax.cond(unsettled,
                        lambda: lax.fori_loop(SEARCH_CHECK_BIT, 32, search, (thr, nge)),
                        lambda: (thr, nge))
    excess = jnp.where((nge > k_sel) & (thr != INT_MIN), 1, 0)
    any_excess = jnp.max(excess) > 0
    need = lax.cond(any_excess,
                    lambda: (k_sel - count(thr, True)).astype(F32),
                    lambda: jnp.zeros((1, Q_BLOCK), F32))
    thr_lo = jnp.maximum(thr, jnp.int32(INT_MIN + 1))

    qf = q_ref[0]
    qh = []
    for h in range(ATT_HEADS):
        blk = qf[:, (h // 2) * LANES:(h // 2 + 1) * LANES]
        if h % 2:
            blk = pltpu.roll(blk, ATT_HEAD_DIM, 1)
        qh.append((jnp.where(lane < ATT_HEAD_DIM, blk, 0.0) * (LOG2E * ATT_HEAD_DIM ** -0.5)).astype(BF16))
    qpair = pairs(qh)
    kv_of_pair = [(2 * p) // (ATT_HEADS // ATT_KV_HEADS) for p in range(ATT_HEADS // 2)]

    acc_ref[...] = jnp.zeros(acc_ref.shape, F32)
    m_ref[...] = jnp.full(m_ref.shape, NEG_INF, F32)
    npair = ATT_HEADS // 2
    p_ref[npair:] = jnp.zeros((npair,) + p_ref.shape[1:], BF16)

    def to_mask(c, eq_seen):
        kc = keys_ref[c]

        def bias_simple():
            return jnp.where(kc >= thr_lo, 0.0, NEG_INF), eq_seen

        def bias_ties():
            eq = (kc == thr) & causal_mask(c)
            eqf = jnp.where(eq, 1.0, 0.0)
            seen = eq_seen
            ranks = []
            for t in range(C // LANES):
                part = eqf[t * LANES:(t + 1) * LANES]
                ranks.append(seen + _dot(ltri_ref[...], part.astype(BF16)))
                seen = seen + jnp.sum(part, axis=0, keepdims=True)
            rank = jnp.concatenate(ranks, axis=0)
            take = (kc > thr) | (eq & (rank < need))
            return jnp.where(take, 0.0, NEG_INF), seen

        bias, eq_next = lax.cond(any_excess, bias_ties, bias_simple)
        keys_ref[c] = pltpu.bitcast(bias, I32)
        return eq_next

    lax.fori_loop(0, nck, to_mask, jnp.zeros((1, Q_BLOCK), F32))

    def logits(c_any, slot):
        c = jnp.minimum(c_any, nck - 1)
        bias = pltpu.bitcast(keys_ref[c], F32) + jnp.where(c_any < nck, 0.0, NEG_INF)
        bias_ref[slot] = bias
        start = pl.multiple_of(c * C, C)
        for p in range(npair):
            lg2 = _dot_nt(kx_ref[0, kv_of_pair[p], pl.ds(start, C), :], qpair[p])
            lg_ref[slot * ATT_HEADS + 2 * p] = lg2[:, :LANES] + bias_ref[slot]
            lg_ref[slot * ATT_HEADS + 2 * p + 1] = lg2[:, LANES:] + bias_ref[slot]

    def weighted_values(c_any, slot, alpha):
        c = jnp.clip(c_any, 0, nck - 1)
        for p in range(npair):
            o2 = _dot(vt_ref[0, kv_of_pair[p], c], p_ref[slot * npair + p])
            for s in range(2):
                h = 2 * p + s
                acc_ref[h] = alpha[h:h + 1] * acc_ref[h] + o2[:, s * LANES:(s + 1) * LANES]

    def softmax(slot):
        shifts, alphas = [], []
        for h in range(ATT_HEADS):
            m_old = m_ref[h]
            mc = jnp.max(lg_ref[slot * ATT_HEADS + h].reshape(nsub, 8, LANES), axis=0)
            m_new = jnp.maximum(m_old, jnp.max(mc, axis=0, keepdims=True))
            m_ref[h] = m_new
            m_safe = jnp.where(m_new == NEG_INF, 0.0, m_new)
            shifts.append(m_safe[:1])
            alphas.append(jnp.exp2(m_old - m_safe)[:1])
        for h in range(ATT_HEADS):
            p_ref[slot * npair + h // 2, :, (h % 2) * LANES:(h % 2 + 1) * LANES] = (
                jnp.exp2(lg_ref[slot * ATT_HEADS + h] - shifts[h]).astype(BF16))
        return jnp.concatenate(alphas, axis=0)

    def step(c, cur, alpha_prev):
        nxt = 1 - cur
        weighted_values(c - 1, nxt, alpha_prev)
        logits(c + 1, nxt)
        return softmax(cur)

    def phase_c(i, alpha_prev):
        return step(2 * i + 1, 1, step(2 * i, 0, alpha_prev))

    logits(0, 0)
    alpha_last = lax.fori_loop(0, (nck + 1) // 2, phase_c, jnp.ones((ATT_HEADS, Q_BLOCK), F32))
    weighted_values(2 * ((nck + 1) // 2) - 1, 1, alpha_last)

    for p in range(ATT_HEADS // 2):
        a = acc_ref[2 * p]
        b = acc_ref[2 * p + 1]
        ot = jnp.concatenate([a[:ATT_HEAD_DIM] / a[ATT_HEAD_DIM:ATT_HEAD_DIM + 1],
                              b[:ATT_HEAD_DIM] / b[ATT_HEAD_DIM:ATT_HEAD_DIM + 1]], axis=0)
        o_ref[0, :, p * LANES:(p + 1) * LANES] = ot.T


def _attn(q, iq, ikf, ikb, kx, vx, k_sel):
    B, Lp, _ = q.shape
    C = _pick_tile(Lp, (640, 512, 256, 128))
    nch = Lp // C
    ltri = jnp.asarray(np.tril(np.ones((LANES, LANES), np.float32), -1), BF16)
    vt = vx.reshape(B, ATT_KV_HEADS, nch, C, LANES).transpose(0, 1, 2, 4, 3)
    qspec = lambda w: pl.BlockSpec((1, Q_BLOCK, w), lambda b, j: (b, j, 0))
    return pl.pallas_call(
        functools.partial(_attn_kernel, C=C, k_sel=k_sel),
        grid=(B, Lp // Q_BLOCK),
        in_specs=[qspec(512), qspec(256), qspec(128),
                  pl.BlockSpec((1, Lp, LANES), lambda b, j: (b, 0, 0)),
                  pl.BlockSpec((1, ATT_KV_HEADS, Lp, LANES), lambda b, j: (b, 0, 0, 0)),
                  pl.BlockSpec((1, ATT_KV_HEADS, nch, LANES, C), lambda b, j: (b, 0, 0, 0, 0)),
                  pl.BlockSpec((LANES, LANES), lambda b, j: (0, 0))],
        out_specs=qspec(512),
        out_shape=jax.ShapeDtypeStruct((B, Lp, ATT_WIDTH), F32),
        scratch_shapes=[pltpu.VMEM((nch, C, Q_BLOCK), I32),
                        pltpu.VMEM((ATT_HEADS, LANES, Q_BLOCK), F32),
                        pltpu.VMEM((ATT_HEADS, 8, Q_BLOCK), F32),
                        pltpu.VMEM((2, C, Q_BLOCK), F32),
                        pltpu.VMEM((2 * ATT_HEADS, C, Q_BLOCK), F32),
                        pltpu.VMEM((2 * (ATT_HEADS // 2), C, 2 * Q_BLOCK), BF16)],
        compiler_params=pltpu.CompilerParams(
            dimension_semantics=("arbitrary", "arbitrary"), vmem_limit_bytes=VMEM_LIMIT),
        name="attn",
    )(q, iq, ikf, ikb, kx, vt, ltri)


def _hgrn_sum_matrix(C):
    nlev = int(np.log2(C))
    t = np.arange(C)[:, None]
    jj = np.arange(C)[None, :]
    mats = [(jj <= t), (jj > t)]
    for l in range(nlev):
        s = 1 << l
        mid = ((t >> (l + 1)) << (l + 1)) + s - 1
        odd = ((t >> l) & 1) == 1
        mats.append(np.where(odd, (jj > mid) & (jj <= t), (jj > t) & (jj <= mid)))
    return np.concatenate(mats, axis=0).astype(np.float32), nlev


def _hgrn_kernel(hq_ref, hf_ref, hi_ref, hg_ref, lb_ref, gn_ref, e_ref, o_ref, st_ref, *, C, nlev):
    @pl.when(pl.program_id(1) == 0)
    def _():
        st_ref[...] = jnp.zeros(st_ref.shape, F32)

    for bb in range(hq_ref.shape[0]):
        for h in range(HG_HEADS):
            _hgrn_head(hq_ref, hf_ref, hi_ref, hg_ref, lb_ref, gn_ref, e_ref, o_ref, st_ref, bb, h, C, nlev)


def _hgrn_head(hq_ref, hf_ref, hi_ref, hg_ref, lb_ref, gn_ref, e_ref, o_ref, st_ref, bb, h, C, nlev):
    cols = slice(h * HG_DK, (h + 1) * HG_DK)
    lb = lb_ref[:, cols]
    f = lb + (1.0 - lb) * jax.nn.sigmoid(hf_ref[bb, :, cols])
    logf = jnp.log(f)
    kk = 1.0 - f
    q = hq_ref[bb, :, cols] * (HG_DK ** -0.5)
    v = hi_ref[bb, :, cols].astype(BF16)

    p1 = logf.astype(BF16)
    r1 = logf - p1.astype(F32)
    p2 = r1.astype(BF16)
    p3 = (r1 - p2.astype(F32)).astype(BF16)
    x3 = _dot(e_ref[...], jnp.concatenate([p1, p2, p3], axis=1))
    xs = x3[:, :LANES] + x3[:, LANES:2 * LANES] + x3[:, 2 * LANES:]

    b = xs[0:C]
    suffix = xs[C:2 * C]
    rt = lax.broadcasted_iota(I32, (C, C), 0)
    ct = lax.broadcasted_iota(I32, (C, C), 1)
    row = lax.broadcasted_iota(I32, (C, 1), 0)

    scores = jnp.where(rt == ct, _dot_nt(q.astype(BF16), kk.astype(BF16)), 0.0)
    for l in range(nlev):
        w = jnp.exp(xs[(2 + l) * C:(3 + l) * C])
        odd = ((row >> l) & 1) == 1
        a = (jnp.where(odd, q, kk) * w).astype(BF16)
        pair = (((rt >> l) & 1) == 1) & ((ct >> l) == (rt >> l) - 1)
        scores = scores + jnp.where(pair, _dot_nt(a, a), 0.0)

    st = st_ref[bb * HG_HEADS + h]
    qb = (q * jnp.exp(b)).astype(BF16)
    o = _dot(scores.astype(BF16), v) + _dot_nt(qb, st.astype(BF16))
    ks = (kk * jnp.exp(suffix)).astype(BF16)
    st_ref[bb * HG_HEADS + h] = jnp.exp(b[C - 1:C, :]) * st + _dot_tn(v, ks)

    o = o * lax.rsqrt(jnp.mean(o * o, axis=-1, keepdims=True) + EPS) * gn_ref[:, cols]
    hg = hg_ref[bb, :, cols]
    o_ref[bb, :, cols] = o * (hg * jax.nn.sigmoid(hg))


def _hgrn(hq, hf, hi, hg, lb, gn):
    B, Lp, _ = hq.shape
    C = HG_CHUNK
    e_np, nlev = _hgrn_sum_matrix(C)
    e = jnp.asarray(e_np, BF16)
    nb = 2 if B % 2 == 0 else 1
    blk = pl.BlockSpec((nb, C, HG_WIDTH), lambda b, c: (b, c, 0))
    vec = pl.BlockSpec((1, HG_WIDTH), lambda b, c: (0, 0))
    return pl.pallas_call(
        functools.partial(_hgrn_kernel, C=C, nlev=nlev),
        grid=(B // nb, Lp // C),
        in_specs=[blk, blk, blk, blk, vec, vec,
                  pl.BlockSpec(e.shape, lambda b, c: (0, 0))],
        out_specs=blk,
        out_shape=jax.ShapeDtypeStruct((B, Lp, HG_WIDTH), F32),
        scratch_shapes=[pltpu.VMEM((nb * HG_HEADS, HG_DV, HG_DK), F32)],
        compiler_params=pltpu.CompilerParams(
            dimension_semantics=("arbitrary", "arbitrary"), vmem_limit_bytes=VMEM_LIMIT),
        name="hgrn",
    )(hq, hf, hi, hg, lb.reshape(1, -1), gn.reshape(1, -1), e)


def _merge_kernel(att_ref, hgr_ref, ga_ref, gh_ref, h_ref, wa_ref, wh_ref, wo_ref, g_ref,
                  wr_ref, br_ref, lt_ref,
                  h2_ref, xrow_ref, topi_ref, gate_ref, rank_ref, cnt_ref, *, tm):
    mix = (jax.nn.sigmoid(ga_ref[0]) * _dot(att_ref[0].astype(BF16), wa_ref[...])
           + jax.nn.sigmoid(gh_ref[0]) * _dot(hgr_ref[0].astype(BF16), wh_ref[...]))
    h2 = h_ref[0] + _dot(mix.astype(BF16), wo_ref[...])
    h2_ref[0] = h2
    xn = h2 * lax.rsqrt(jnp.mean(h2 * h2, axis=-1, keepdims=True) + EPS) * g_ref[...]
    for k in range(D_MODEL // LANES):
        xrow_ref[pl.ds(k, tm, stride=8), :] = xn[:, k * LANES:(k + 1) * LANES]

    xh = xn.astype(BF16)
    xl = (xn - xh.astype(F32)).astype(BF16)
    l1 = _dot(xh, wr_ref[...])
    logits = l1[:, :LANES] + l1[:, LANES:] + _dot(xl, wr_ref[:, :LANES]) + br_ref[...]

    lane = _lane_iota((tm, LANES))
    lanef = lane.astype(F32)
    work = logits
    vals, sels, idxs = [], [], []
    for _ in range(TOP_K):
        m = jnp.max(work, axis=1, keepdims=True)
        idx = jnp.min(jnp.where(work == m, lanef, float(LANES)), axis=1, keepdims=True)
        sel = lanef == idx
        vals.append(m)
        idxs.append(idx)
        sels.append(sel)
        work = jnp.where(sel, NEG_INF, work)
    es = [jnp.exp(v - vals[0]) for v in vals]
    den = es[0] + es[1] + es[2] + es[3]

    member = jnp.zeros((tm, LANES), F32)
    for sel in sels:
        member = member + jnp.where(sel, 1.0, 0.0)
    before = _dot(lt_ref[...], member.astype(BF16))
    topi = jnp.zeros((tm, LANES), F32)
    gate = jnp.zeros((tm, LANES), F32)
    rank = jnp.zeros((tm, LANES), F32)
    for jx in range(TOP_K):
        here = lane == jx
        topi = jnp.where(here, idxs[jx], topi)
        gate = jnp.where(here, es[jx] / den, gate)
        rank = jnp.where(here, jnp.sum(jnp.where(sels[jx], before, 0.0), axis=1, keepdims=True), rank)
    topi_ref[0] = topi.astype(I32)
    gate_ref[0] = gate
    rank_ref[0] = rank.astype(I32)
    cnt_ref[0] = jnp.sum(member, axis=0, keepdims=True).astype(I32)


def _merge(att, hgr, ga, gh, h, wa, wh, wo, g, wr2, br):
    B, Lp, D = h.shape
    tm = _pick_tile(Lp, (320, 256, 128))
    nt = Lp // tm
    lt = jnp.asarray(np.tril(np.ones((tm, tm), np.float32), -1), BF16)
    row = lambda w: pl.BlockSpec((1, tm, w), lambda b, i: (b, i, 0))
    full = lambda a: pl.BlockSpec(a.shape, lambda b, i: (0,) * a.ndim)
    g2 = g.reshape(1, D)
    out_shape = [jax.ShapeDtypeStruct((B, Lp, D), F32),
                 jax.ShapeDtypeStruct((B * Lp * 8, LANES), F32),
                 jax.ShapeDtypeStruct((B, Lp, LANES), I32),
                 jax.ShapeDtypeStruct((B, Lp, LANES), F32),
                 jax.ShapeDtypeStruct((B, Lp, LANES), I32),
                 jax.ShapeDtypeStruct((B * nt, 1, LANES), I32)]
    out_specs = [row(D),
                 pl.BlockSpec((tm * 8, LANES), lambda b, i: (b * nt + i, 0)),
                 row(LANES), row(LANES), row(LANES),
                 pl.BlockSpec((1, 1, LANES), lambda b, i: (b * nt + i, 0, 0))]
    return pl.pallas_call(
        functools.partial(_merge_kernel, tm=tm),
        grid=(B, nt),
        in_specs=[row(512), row(512), row(D), row(D), row(D),
                  full(wa), full(wh), full(wo), full(g2), full(wr2), full(br), full(lt)],
        out_specs=out_specs,
        out_shape=out_shape,
        compiler_params=pltpu.CompilerParams(
            dimension_semantics=("arbitrary", "arbitrary"), vmem_limit_bytes=VMEM_LIMIT),
        name="merge",
    )(att, hgr, ga, gh, h, wa, wh, wo, g2, wr2, br, lt), tm


def _slots_kernel(topi_ref, rank_ref, base_ref, slot_ref):
    topi = topi_ref[0]
    rank = rank_ref[0]
    base = base_ref[0]
    lane = _lane_iota(topi.shape)
    slot = jnp.zeros(topi.shape, I32)
    for jx in range(TOP_K):
        mine = lane == topi[:, jx:jx + 1]
        first = jnp.sum(jnp.where(mine, base, 0), axis=1, keepdims=True)
        slot = jnp.where(lane == jx, first + rank[:, jx:jx + 1], slot)
    slot_ref[0] = slot


def _slots(topi, rank, base, tm):
    B, Lp, _ = topi.shape
    nt = Lp // tm
    row = pl.BlockSpec((1, tm, LANES), lambda b, i: (b, i, 0))
    return pl.pallas_call(
        _slots_kernel,
        grid=(B, nt),
        in_specs=[row, row, pl.BlockSpec((1, 1, LANES), lambda b, i: (b * nt + i, 0, 0))],
        out_specs=row,
        out_shape=jax.ShapeDtypeStruct((B, Lp, LANES), I32),
        compiler_params=pltpu.CompilerParams(dimension_semantics=("arbitrary", "arbitrary")),
        name="slots",
    )(topi, rank, base)


def _dispatch_kernel(pad_lo_ref, pad_n_ref, nu_ref, slot_ref, x_ref, xs_hbm, zblk_ref, sem, *, tm, blk, nblk):
    @pl.when(pl.program_id(0) == 0)
    def _():
        zblk_ref[...] = jnp.zeros(zblk_ref.shape, F32)

        def zblock(b):
            return pltpu.make_async_copy(zblk_ref, xs_hbm.at[pl.ds(b * blk, blk)], sem)

        def bissue(b, c):
            zblock(b).start()
            return c

        def bdrain(b, c):
            zblock(b).wait()
            return c

        lax.fori_loop(nu_ref[0], nblk, bissue, 0)
        lax.fori_loop(nu_ref[0], nblk, bdrain, 0)

        def per_expert(e, c):
            def zcopy(r):
                return pltpu.make_async_copy(zblk_ref.at[0], xs_hbm.at[pad_lo_ref[e] + r], sem)

            def zissue(r, c2):
                zcopy(r).start()
                return c2

            def zdrain(r, c2):
                zcopy(r).wait()
                return c2

            lax.fori_loop(0, pad_n_ref[e], zissue, 0)
            lax.fori_loop(0, pad_n_ref[e], zdrain, 0)
            return c

        lax.fori_loop(0, N_EXPERTS, per_expert, 0)

    def copy(t, jx):
        return pltpu.make_async_copy(x_ref.at[t], xs_hbm.at[slot_ref[0, 0, t * TOP_K + jx]], sem)

    def issue(t, c):
        for jx in range(TOP_K):
            copy(t, jx).start(priority=jx % 2)
        return c

    def drain(t, c):
        for jx in range(TOP_K):
            copy(t, jx).wait()
        return c

    lax.fori_loop(0, tm, issue, 0)
    lax.fori_loop(0, tm, drain, 0)


def _dispatch(pad_lo, pad_n, n_used, slot_s, xrow, cap, tm, blk):
    nt = slot_s.shape[0]
    return pl.pallas_call(
        functools.partial(_dispatch_kernel, tm=tm, blk=blk, nblk=cap // blk),
        grid_spec=pltpu.PrefetchScalarGridSpec(
            num_scalar_prefetch=3, grid=(nt,),
            in_specs=[pl.BlockSpec((1, 1, tm * TOP_K), lambda i, lo, n, nu: (i, 0, 0), memory_space=pltpu.SMEM),
                      pl.BlockSpec((tm, 8, LANES), lambda i, lo, n, nu: (i, 0, 0))],
            out_specs=pl.BlockSpec(memory_space=pl.ANY),
            scratch_shapes=[pltpu.VMEM((blk, 8, LANES), F32), pltpu.SemaphoreType.DMA(())]),
        out_shape=jax.ShapeDtypeStruct((cap, 8, LANES), F32),
        compiler_params=pltpu.CompilerParams(dimension_semantics=("arbitrary",)),
        name="dispatch",
    )(pad_lo, pad_n, n_used, slot_s, xrow)


def _ffn_kernel(be_ref, nu_ref, x_ref, wg_ref, bg_ref, wu_ref, bu_ref, wd_ref, bd_ref, o_ref,
                wgb_ref, wub_ref, wdb_ref, *, blk):
    i = pl.program_id(0)
    nk = D_MODEL // LANES
    used = i < nu_ref[0]
    new_expert = (i == 0) | (be_ref[i] != be_ref[jnp.maximum(i - 1, 0)])

    @pl.when(used & new_expert)
    def _():
        rows = 128

        def cast(r, c):
            sl = pl.ds(pl.multiple_of(r * rows, rows), rows)
            wgb_ref[sl, :] = wg_ref[0, sl, :].astype(BF16)
            wub_ref[sl, :] = wu_ref[0, sl, :].astype(BF16)
            wdb_ref[sl, :] = wd_ref[0, sl, :].astype(BF16)
            return c

        lax.fori_loop(0, D_MODEL // rows, cast, 0)

    @pl.when(used)
    def _():
        xb = jnp.concatenate([x_ref[pl.ds(k, blk, stride=8), :] for k in range(nk)], axis=1).astype(BF16)
        g = _dot(xb, wgb_ref[...]) + bg_ref[0]
        u = _dot(xb, wub_ref[...]) + bu_ref[0]
        g = jnp.minimum(g, SWIGLU_LIMIT)
        u = jnp.clip(u, -SWIGLU_LIMIT, SWIGLU_LIMIT)
        act = (u + 1.0) * (g * jax.nn.sigmoid(SWIGLU_ALPHA * g))
        out = _dot(act.astype(BF16), wdb_ref[...]) + bd_ref[0]
        for k in range(nk):
            o_ref[pl.ds(k, blk, stride=8), :] = out[:, k * LANES:(k + 1) * LANES]

    @pl.when(i >= nu_ref[0])
    def _():
        o_ref[...] = jnp.zeros(o_ref.shape, F32)


def _ffn(block_expert, n_used, xs, wg, bg, wu, bu, wd, bd, blk):
    cap = xs.shape[0]
    nblk = cap // blk
    x2 = xs.reshape(cap * 8, LANES)
    wspec = pl.BlockSpec((1, D_MODEL, D_FF), lambda i, be, nu: (be[i], 0, 0))
    wdspec = pl.BlockSpec((1, D_FF, D_MODEL), lambda i, be, nu: (be[i], 0, 0))
    bspec = lambda w: pl.BlockSpec((1, 1, w), lambda i, be, nu: (be[i], 0, 0))
    rows = pl.BlockSpec((blk * 8, LANES), lambda i, be, nu: (i, 0))
    rows_in = pl.BlockSpec((blk * 8, LANES), lambda i, be, nu: (jnp.minimum(i, nu[0] - 1), 0))
    assert D_FF == D_MODEL
    wscratch = pltpu.VMEM((D_MODEL, D_FF), BF16)
    out = pl.pallas_call(
        functools.partial(_ffn_kernel, blk=blk),
        grid_spec=pltpu.PrefetchScalarGridSpec(
            num_scalar_prefetch=2, grid=(nblk,),
            in_specs=[rows_in, wspec, bspec(D_FF), wspec, bspec(D_FF), wdspec, bspec(D_MODEL)],
            out_specs=rows,
            scratch_shapes=[wscratch, wscratch, wscratch]),
        out_shape=jax.ShapeDtypeStruct((cap * 8, LANES), F32),
        compiler_params=pltpu.CompilerParams(
            dimension_semantics=("arbitrary",), vmem_limit_bytes=VMEM_LIMIT),
        name="ffn",
    )(block_expert, n_used, x2, wg, bg, wu, bu, wd, bd)
    return out.reshape(cap, 8, LANES)


def _combine_kernel(slot_ref, gate_ref, h2_ref, g_ref, ys_hbm, o_ref, buf_ref, sem, *, tm):
    def copy(t, jx):
        dst = buf_ref.at[pl.ds(pl.multiple_of((jx * tm + t) * 8, 8), 8), :]
        return pltpu.make_async_copy(ys_hbm.at[slot_ref[0, 0, t * TOP_K + jx]], dst, sem)

    def issue(t, c):
        for jx in range(TOP_K):
            copy(t, jx).start(priority=jx % 2)
        return c

    def drain(t, c):
        for jx in range(TOP_K):
            copy(t, jx).wait()
        return c

    lax.fori_loop(0, tm, issue, 0)
    lax.fori_loop(0, tm, drain, 0)

    gate = gate_ref[0]
    h2 = h2_ref[0]
    cols = []
    for k in range(D_MODEL // LANES):
        y = h2[:, k * LANES:(k + 1) * LANES]
        for jx in range(TOP_K):
            y = y + gate[:, jx:jx + 1] * buf_ref[pl.ds(jx * tm * 8 + k, tm, stride=8), :]
        cols.append(y)
    ho = jnp.concatenate(cols, axis=1)
    o_ref[0] = ho * lax.rsqrt(jnp.mean(ho * ho, axis=-1, keepdims=True) + EPS) * g_ref[...]


def _combine(slot_s, gate, h2, g, ys, tm):
    B, Lp, D = h2.shape
    nt = Lp // tm
    row = lambda w: pl.BlockSpec((1, tm, w), lambda b, i: (b, i, 0))
    return pl.pallas_call(
        functools.partial(_combine_kernel, tm=tm),
        grid=(B, nt),
        in_specs=[pl.BlockSpec((1, 1, tm * TOP_K), lambda b, i: (b * nt + i, 0, 0), memory_space=pltpu.SMEM),
                  row(LANES), row(D),
                  pl.BlockSpec((1, D), lambda b, i: (0, 0)),
                  pl.BlockSpec(memory_space=pl.ANY)],
        out_specs=row(D),
        out_shape=jax.ShapeDtypeStruct((B, Lp, D), F32),
        scratch_shapes=[pltpu.VMEM((TOP_K * tm * 8, LANES), F32), pltpu.SemaphoreType.DMA(())],
        compiler_params=pltpu.CompilerParams(
            dimension_semantics=("arbitrary", "arbitrary"), vmem_limit_bytes=VMEM_LIMIT),
        name="combine",
    )(slot_s, gate, h2, g.reshape(1, D), ys)


def _moe(h2, xrow, topi, gate, rank, cnt, tm, ffn_w, final_g):
    B, Lp, D = h2.shape
    T = B * Lp
    nt_total = cnt.shape[0]
    blk = FFN_BLOCK
    nblk = -(-(T * TOP_K) // blk) + N_EXPERTS
    cap = nblk * blk
    cnt2 = cnt.reshape(nt_total, LANES)
    totals = jnp.sum(cnt2, axis=0)
    padded = (totals + blk - 1) // blk * blk
    pad_ends = jnp.cumsum(padded)
    pad_starts = pad_ends - padded
    base = (pad_starts[None, :] + jnp.cumsum(cnt2, axis=0) - cnt2).astype(I32).reshape(nt_total, 1, LANES)
    block_start = (jnp.arange(nblk) * blk)[:, None]
    block_expert = jnp.minimum(
        jnp.sum(pad_ends[None, :N_EXPERTS] <= block_start, axis=1), N_EXPERTS - 1).astype(I32)
    n_used = (pad_ends[N_EXPERTS - 1] // blk).astype(I32).reshape(1)
    slot = _slots(topi, rank, base, tm)
    slot_s = slot[:, :, :TOP_K].reshape(nt_total, 1, tm * TOP_K)

    pad_lo = (pad_starts + totals)[:N_EXPERTS].astype(I32)
    pad_n = (padded - totals)[:N_EXPERTS].astype(I32)
    xs = _dispatch(pad_lo, pad_n, n_used, slot_s, xrow.reshape(T, 8, LANES), cap, tm, blk)
    ys = _ffn(block_expert, n_used, xs, *ffn_w, blk)
    return _combine(slot_s, gate, h2, final_g, ys, tm)


def kernel(x, meta_tokens, attn_norm_g, w_in, hgrn_norm_g, w_up_attn, w_up_hgrn, w_out,
           hgrn_lb_logits, ffn_norm_g, router_w, router_b, w_gate, b_gate, w_up, b_up,
           w_down, b_down, final_norm_g):
    B, S, D = x.shape
    L = S + N_META
    Lp = -(-L // Q_BLOCK) * Q_BLOCK
    k_sel = min(TOPK_MAX, L // 4)
    meta = jnp.broadcast_to(meta_tokens[None].astype(x.dtype), (B, N_META, D))
    h = jnp.concatenate([meta, x, jnp.zeros((B, Lp - L, D), x.dtype)], axis=1)
    lb = jnp.cumsum(jax.nn.softmax(hgrn_lb_logits.astype(F32), axis=0), axis=0)[0]

    (q, kx, vx, iq, ikf, ikb, hq, hf, hi, hg, ga, gh) = _inproj(h, attn_norm_g[0], _arrange_w_in(w_in[0]))
    att = _attn(q, iq, ikf, ikb, kx, vx, k_sel)
    hgr = _hgrn(hq, hf, hi, hg, lb, hgrn_norm_g[0])

    wr = jnp.pad(router_w[0], ((0, 0), (0, LANES - N_EXPERTS)))
    wr_hi = wr.astype(BF16)
    wr_lo = (wr - wr_hi.astype(F32)).astype(BF16)
    br = jnp.pad(router_b[0], (0, LANES - N_EXPERTS), constant_values=-1e30).reshape(1, LANES)
    (h2, xrow, topi, gate, rank, cnt), tm = _merge(
        att, hgr, ga, gh, h, w_up_attn[0].astype(BF16), w_up_hgrn[0].astype(BF16), w_out[0].astype(BF16),
        ffn_norm_g[0], jnp.concatenate([wr_hi, wr_lo], axis=1), br)

    ffn_w = (w_gate[0], b_gate[0].reshape(N_EXPERTS, 1, D_FF),
             w_up[0], b_up[0].reshape(N_EXPERTS, 1, D_FF),
             w_down[0], b_down[0].reshape(N_EXPERTS, 1, D_MODEL))
    out = _moe(h2, xrow, topi, gate, rank, cnt, tm, ffn_w, final_norm_g)
    return out[:, N_META:L]
```
